```python
import jax, jax.numpy as jnp
from jax import lax
import numpy as np

D_MODEL = 1024
BATCH = 2
SEQ = 8192
DEPTH = 2
DEC_BATCH = 128
DEC_SEQ = 1
PAST_LEN = 2048
PAGE_SIZE = 128

HEAD_DIM = 64
D_MIX = D_MODEL
D_A = D_MIX // 2
D_B = D_MIX - D_A
A_GROUPS = D_A // HEAD_DIM
B_HEADS = D_B // HEAD_DIM
B_KV_HEADS = 2
GQA_R = B_HEADS // B_KV_HEADS
KV_W = B_KV_HEADS * HEAD_DIM
N_BRANCH = 3
P_IN = 2 * D_A + D_B + N_BRANCH * 2 * KV_W + N_BRANCH * B_HEADS
CHUNK = 128
CMP_BLOCK = 32
CMP_STRIDE = 16
SLC_BLOCK = 64
N_SELECT = 16
WINDOW = 512
Q_BLOCK = 128
ROPE_THETA = 10000.0
D_FF = 2816
N_EXPERTS = 8
TOP_K = 2
MOE_BLOCK = 128
N_DENSE = (DEPTH + 1) // 2
N_MOE = DEPTH // 2
NEG_INF = -1e30
FORCE_SCORE = 1e30
EPS = 1e-6

kernel_name = 'hymba_gmlp_nsa_adaln_decoder_step'


def rms_norm(x, g):
    xf = x.astype(jnp.float32)
    y = xf * lax.rsqrt(jnp.mean(xf * xf, axis=-1, keepdims=True) + EPS)
    return (y * g.astype(jnp.float32)).astype(x.dtype)


def group_layer_norm(x):
    xf = x.astype(jnp.float32)
    mu = jnp.mean(xf, axis=-1, keepdims=True)
    xc = xf - mu
    return (xc * lax.rsqrt(jnp.mean(xc * xc, axis=-1, keepdims=True) + EPS)).astype(x.dtype)


def rope(x, pos):
    half = HEAD_DIM // 2
    inv = ROPE_THETA ** (-jnp.arange(half, dtype=jnp.float32) / half)
    ang = pos.astype(jnp.float32)[:, None] * inv[None, :]
    cos = jnp.cos(ang)[:, None, :]
    sin = jnp.sin(ang)[:, None, :]
    xf = x.astype(jnp.float32)
    x1, x2 = xf[..., :half], xf[..., half:]
    return jnp.concatenate([x1 * cos - x2 * sin, x2 * cos + x1 * sin], axis=-1).astype(x.dtype)


def adaln(c, w, b):
    m = (jax.nn.silu(c) @ w + b)[:, None, :]
    return jnp.split(m, 6, axis=-1)


def modulate(h, shift, scale):
    return h * (1 + scale) + shift


def project(h, pos, w_in, q_g, k_g):
    B, T, _ = h.shape
    z = h @ w_in
    uv = jax.nn.gelu(z[..., :2 * D_A])
    u = uv[..., :D_A].reshape(B, T, A_GROUPS, HEAD_DIM)
    v = group_layer_norm(uv[..., D_A:].reshape(B, T, A_GROUPS, HEAD_DIM))
    o = 2 * D_A
    q = z[..., o:o + D_B].reshape(B, T, B_HEADS, HEAD_DIM)
    o += D_B
    kv = z[..., o:o + N_BRANCH * 2 * KV_W].reshape(B, T, N_BRANCH, 2, B_KV_HEADS, HEAD_DIM)
    o += N_BRANCH * 2 * KV_W
    gates = jax.nn.sigmoid(z[..., o:].astype(jnp.float32)).reshape(B, T, B_HEADS, N_BRANCH)
    q = rope(rms_norm(q, q_g), pos) * (HEAD_DIM ** -0.5)
    k = rms_norm(kv[:, :, :, 0], k_g[:, None, :]).reshape(B, T, N_BRANCH * B_KV_HEADS, HEAD_DIM)
    k = rope(k, pos).reshape(B, T, N_BRANCH, B_KV_HEADS, HEAD_DIM)
    rows = jnp.stack([k, kv[:, :, :, 1]], axis=3)
    return u, v, q, rows, gates


def spatial_gate(u, v, w_s, b_s):
    B, T = v.shape[:2]
    n = -(-T // CHUNK)
    tp = n * CHUNK
    vp = jnp.pad(v, ((0, 0), (0, tp - T), (0, 0), (0, 0))).reshape(B, n, CHUNK, A_GROUPS, HEAD_DIM)
    w = jnp.where(jnp.tril(jnp.ones((CHUNK, CHUNK), dtype=bool)), w_s, 0)
    mixed = jnp.einsum('gts,bnsgd->bntgd', w, vp) + b_s.T[:, :, None]
    return u * mixed.reshape(B, tp, A_GROUPS, HEAD_DIM)[:, :T]


def compress(rows, pe, w):
    B, L = rows.shape[:2]
    n_piece = L // CMP_STRIDE
    per = CMP_BLOCK // CMP_STRIDE
    nc = n_piece - per + 1
    r = rows[:, :n_piece * CMP_STRIDE].reshape(B, n_piece, CMP_STRIDE, B_KV_HEADS, HEAD_DIM)
    blocks = jnp.concatenate([r[:, j:j + nc] for j in range(per)], axis=2)
    return jnp.einsum('bnlgd,lde->bnge', blocks + pe[:, None, :], w)


def nsa_core(q, gates, qpos, kc, vc, ks_blk, vs_blk, kw, vw, wpos):
    B, Tq = q.shape[:2]
    qg = q.reshape(B, Tq, B_KV_HEADS, GQA_R, HEAD_DIM)
    NC = kc.shape[1]
    c_last = jnp.arange(NC) * CMP_STRIDE + CMP_BLOCK - 1
    cv = (c_last[None, :] <= qpos[:, None])[None, :, None, None, :]
    s = jnp.einsum('btgrd,bngd->btgrn', qg, kc).astype(jnp.float32)
    p_cmp = jnp.where(cv, jax.nn.softmax(jnp.where(cv, s, NEG_INF), axis=-1), 0.0)
    o_cmp = jnp.einsum('btgrn,bngd->btgrd', p_cmp.astype(vc.dtype), vc)
    NS = ks_blk.shape[2]
    n_sel = min(N_SELECT, NS)
    cs = jnp.arange(NC) * CMP_STRIDE
    ss = jnp.arange(NS) * SLC_BLOCK
    cover = jnp.clip(jnp.minimum(cs[:, None] + CMP_BLOCK, ss[None, :] + SLC_BLOCK)
                     - jnp.maximum(cs[:, None], ss[None, :]), 0, None).astype(jnp.float32) / CMP_BLOCK
    imp = jnp.einsum('btgn,ns->btgs', p_cmp.sum(axis=3), cover)
    cur = qpos // SLC_BLOCK
    blk = jnp.arange(NS)
    forced = (blk[None, :] == 0) | (blk[None, :] == cur[:, None]) | (blk[None, :] == cur[:, None] - 1)
    imp = jnp.where(forced[None, :, None, :], FORCE_SCORE, imp)
    imp = jnp.where((blk[None, :] <= cur[:, None])[None, :, None, :], imp, NEG_INF)
    _, idx = lax.top_k(imp, n_sel)
    bi = jnp.arange(B)[:, None, None, None]
    gi = jnp.arange(B_KV_HEADS)[None, None, :, None]
    ksel = ks_blk[bi, gi, idx]
    vsel = vs_blk[bi, gi, idx].reshape(B, Tq, B_KV_HEADS, n_sel * SLC_BLOCK, HEAD_DIM)
    kpos = idx[..., None] * SLC_BLOCK + jnp.arange(SLC_BLOCK)
    sv = (kpos <= qpos[None, :, None, None, None]).reshape(B, Tq, B_KV_HEADS, 1, n_sel * SLC_BLOCK)
    s = jnp.einsum('btgrd,btgksd->btgrks', qg, ksel).reshape(
        B, Tq, B_KV_HEADS, GQA_R, n_sel * SLC_BLOCK).astype(jnp.float32)
    p = jax.nn.softmax(jnp.where(sv, s, NEG_INF), axis=-1)
    o_slc = jnp.einsum('btgrm,btgmd->btgrd', p.astype(vsel.dtype), vsel)
    wv = ((wpos[None, :] <= qpos[:, None]) & (wpos[None, :] > qpos[:, None] - WINDOW)
          & (wpos[None, :] >= 0))[None, :, None, None, :]
    s = jnp.einsum('btgrd,blgd->btgrl', qg, kw).astype(jnp.float32)
    p = jax.nn.softmax(jnp.where(wv, s, NEG_INF), axis=-1)
    o_win = jnp.einsum('btgrl,blgd->btgrd', p.astype(vw.dtype), vw)
    gg = gates.reshape(B, Tq, B_KV_HEADS, GQA_R, N_BRANCH).astype(o_cmp.dtype)
    o = gg[..., 0:1] * o_cmp + gg[..., 1:2] * o_slc + gg[..., 2:3] * o_win
    return o.reshape(B, Tq, B_HEADS, HEAD_DIM)


def nsa_prompt(q, gates, rows, pe, w):
    B, T = q.shape[:2]
    kc = compress(rows[:, :, 0, 0], pe[0], w[0])
    vc = compress(rows[:, :, 0, 1], pe[1], w[1])
    ns = T // SLC_BLOCK
    slc = rows[:, :, 1].reshape(B, ns, SLC_BLOCK, 2, B_KV_HEADS, HEAD_DIM).transpose(3, 0, 4, 1, 2, 5)
    win = jnp.pad(rows[:, :, 2], ((0, 0), (WINDOW, 0), (0, 0), (0, 0), (0, 0)))
    nq = T // Q_BLOCK
    qb = q.reshape(B, nq, Q_BLOCK, B_HEADS, HEAD_DIM).swapaxes(0, 1)
    gb = gates.reshape(B, nq, Q_BLOCK, B_HEADS, N_BRANCH).swapaxes(0, 1)
    starts = jnp.arange(nq, dtype=jnp.int32) * Q_BLOCK

    def one_block(args):
        qi, gi, s0 = args
        w_kv = lax.dynamic_slice_in_dim(win, s0, WINDOW + Q_BLOCK, axis=1)
        qpos = s0 + jnp.arange(Q_BLOCK, dtype=jnp.int32)
        wpos = s0 - WINDOW + jnp.arange(WINDOW + Q_BLOCK, dtype=jnp.int32)
        return nsa_core(qi, gi, qpos, kc, vc, slc[0], slc[1], w_kv[:, :, 0], w_kv[:, :, 1], wpos)

    o = lax.map(one_block, (qb, gb, starts))
    return o.swapaxes(0, 1).reshape(B, T, B_HEADS, HEAD_DIM)


def nsa_sample(q, gates, rows, cache_cmp_l, cache_slc_l, win_l, page_table, pe, w):
    DB, Tn = q.shape[:2]
    past = page_table.shape[1] * cache_cmp_l.shape[1]
    cmp_all = jnp.concatenate(
        [cache_cmp_l[page_table].reshape(DB, past, 2, B_KV_HEADS, HEAD_DIM), rows[:, :, 0]], axis=1)
    kc = compress(cmp_all[:, :, 0], pe[0], w[0])
    vc = compress(cmp_all[:, :, 1], pe[1], w[1])
    L = past + Tn
    ns = -(-L // SLC_BLOCK)
    slc_all = jnp.concatenate(
        [cache_slc_l[page_table].reshape(DB, past, 2, B_KV_HEADS, HEAD_DIM), rows[:, :, 1]], axis=1)
    slc_all = jnp.pad(slc_all, ((0, 0), (0, ns * SLC_BLOCK - L), (0, 0), (0, 0), (0, 0)))
    slc = slc_all.reshape(DB, ns, SLC_BLOCK, 2, B_KV_HEADS, HEAD_DIM).transpose(3, 0, 4, 1, 2, 5)
    wb = win_l.shape[1]
    win_all = jnp.concatenate([win_l, rows[:, :, 2]], axis=1)
    qpos = past + jnp.arange(Tn, dtype=jnp.int32)
    wpos = past - wb + jnp.arange(wb + Tn, dtype=jnp.int32)
    o = nsa_core(q, gates, qpos, kc, vc, slc[0], slc[1], win_all[:, :, 0], win_all[:, :, 1], wpos)
    return o, win_all[:, Tn:]


def mix_out(a, o, g, w_out):
    B, T = a.shape[:2]
    heads = rms_norm(jnp.concatenate([a, o], axis=2), g)
    return heads.reshape(B, T, D_MIX) @ w_out


def swiglu(h, wg, wu, wd):
    return (jax.nn.silu(h @ wg) * (h @ wu)) @ wd


def moe_ffn(h, rw, rb, wg, wu, wd):
    shp = h.shape
    x = h.reshape(-1, D_MODEL)
    N = x.shape[0]
    logits = (x @ rw).astype(jnp.float32) + rb.astype(jnp.float32)
    top_l, top_e = lax.top_k(logits, TOP_K)
    top_w = jax.nn.softmax(top_l, axis=-1)
    flat_e = top_e.reshape(-1)
    order = jnp.argsort(flat_e)
    se = flat_e[order]
    counts = jnp.bincount(flat_e, length=N_EXPERTS)
    padded = (counts + MOE_BLOCK - 1) // MOE_BLOCK * MOE_BLOCK
    pad_end = jnp.cumsum(padded)
    pad_start = pad_end - padded
    start = jnp.cumsum(counts) - counts
    dest = pad_start[se] + jnp.arange(N * TOP_K) - start[se]
    nb = -(-(N * TOP_K) // MOE_BLOCK) + N_EXPERTS
    src = jnp.full((nb * MOE_BLOCK,), N, dtype=jnp.int32).at[dest].set((order // TOP_K).astype(jnp.int32))
    xb = jnp.concatenate([x, jnp.zeros((1, D_MODEL), x.dtype)], axis=0)[src].reshape(nb, MOE_BLOCK, D_MODEL)
    blk_e = jnp.minimum(jnp.sum(pad_end[None, :] <= (jnp.arange(nb) * MOE_BLOCK)[:, None], axis=-1),
                        N_EXPERTS - 1)
    yb = lax.map(lambda a: swiglu(a[0], wg[a[1]], wu[a[1]], wd[a[1]]), (xb, blk_e))
    y_sorted = yb.reshape(-1, D_MODEL)[dest]
    y_assign = jnp.zeros((N * TOP_K, D_MODEL), x.dtype).at[order].set(y_sorted).reshape(N, TOP_K, D_MODEL)
    y = jnp.einsum('nk,nkd->nd', top_w.astype(x.dtype), y_assign)
    return y.reshape(shp)


def channel_mixer(h, l, ffn_w_gate, ffn_w_up, ffn_w_down, router_w, router_b, moe_w_gate, moe_w_up, moe_w_down):
    i = l // 2
    if l % 2 == 0:
        return swiglu(h, ffn_w_gate[i], ffn_w_up[i], ffn_w_down[i])
    return moe_ffn(h, router_w[i], router_b[i], moe_w_gate[i], moe_w_up[i], moe_w_down[i])


def setup_inputs(seed: int = 0) -> dict:
    key = jax.random.key(seed)
    ks = jax.random.split(key, 32)
    n_pages = PAST_LEN // PAGE_SIZE
    n_used = DEC_BATCH * n_pages
    n_pool = n_used + max(1, n_used // 4)
    wb = min(WINDOW, PAST_LEN)

    def nrm(k, shape, s=1.0):
        return jax.random.normal(k, shape, jnp.float32) * s

    page_table = jax.random.permutation(ks[0], n_pool)[:n_used].reshape(DEC_BATCH, n_pages).astype(jnp.int32)
    return {
        'x_prompt': nrm(ks[1], (BATCH, SEQ, D_MODEL)),
        'x_sample': nrm(ks[2], (DEC_BATCH, DEC_SEQ, D_MODEL)),
        'cache_cmp': nrm(ks[3], (DEPTH, n_pool, PAGE_SIZE, 2, B_KV_HEADS, HEAD_DIM)),
        'cache_slc': nrm(ks[4], (DEPTH, n_pool, PAGE_SIZE, 2, B_KV_HEADS, HEAD_DIM)),
        'state_win': nrm(ks[5], (DEPTH, DEC_BATCH, wb, 2, B_KV_HEADS, HEAD_DIM)),
        'page_table': page_table,
        'c_prompt': nrm(ks[6], (BATCH, D_MODEL)),
        'c_sample': nrm(ks[7], (DEC_BATCH, D_MODEL)),
        'norm_mix_g': 1.0 + nrm(ks[8], (DEPTH, D_MODEL), 0.02),
        'norm_ffn_g': 1.0 + nrm(ks[9], (DEPTH, D_MODEL), 0.02),
        'w_ada': nrm(ks[10], (DEPTH, D_MODEL, 6 * D_MODEL), 0.02),
        'b_ada': nrm(ks[11], (DEPTH, 6 * D_MODEL), 0.02),
        'w_in': nrm(ks[12], (DEPTH, D_MODEL, P_IN), D_MODEL ** -0.5),
        'w_spatial': nrm(ks[13], (DEPTH, A_GROUPS, CHUNK, CHUNK), CHUNK ** -0.5),
        'b_spatial': 1.0 + nrm(ks[14], (DEPTH, A_GROUPS, CHUNK), 0.02),
        'q_norm_g': 1.0 + nrm(ks[15], (DEPTH, HEAD_DIM), 0.02),
        'k_norm_g': 1.0 + nrm(ks[16], (DEPTH, N_BRANCH, HEAD_DIM), 0.02),
        'cmp_pe': nrm(ks[17], (DEPTH, 2, CMP_BLOCK, HEAD_DIM), 0.1),
        'cmp_w': nrm(ks[18], (DEPTH, 2, CMP_BLOCK, HEAD_DIM, HEAD_DIM), (CMP_BLOCK * HEAD_DIM) ** -0.5),
        'out_norm_g': 1.0 + nrm(ks[19], (DEPTH, D_MIX // HEAD_DIM, HEAD_DIM), 0.02),
        'w_out': nrm(ks[20], (DEPTH, D_MIX, D_MODEL), D_MIX ** -0.5),
        'ffn_w_gate': nrm(ks[21], (N_DENSE, D_MODEL, D_FF), D_MODEL ** -0.5),
        'ffn_w_up': nrm(ks[22], (N_DENSE, D_MODEL, D_FF), D_MODEL ** -0.5),
        'ffn_w_down': nrm(ks[23], (N_DENSE, D_FF, D_MODEL), D_FF ** -0.5),
        'router_w': nrm(ks[24], (N_MOE, D_MODEL, N_EXPERTS), D_MODEL ** -0.5),
        'router_b': nrm(ks[25], (N_MOE, N_EXPERTS), 0.01),
        'moe_w_gate': nrm(ks[26], (N_MOE, N_EXPERTS, D_MODEL, D_FF), D_MODEL ** -0.5),
        'moe_w_up': nrm(ks[27], (N_MOE, N_EXPERTS, D_MODEL, D_FF), D_MODEL ** -0.5),
        'moe_w_down': nrm(ks[28], (N_MOE, N_EXPERTS, D_FF, D_MODEL), D_FF ** -0.5),
    }


def reference(x_prompt, x_sample, cache_cmp, cache_slc, state_win, page_table, c_prompt, c_sample,
              norm_mix_g, norm_ffn_g, w_ada, b_ada, w_in, w_spatial, b_spatial, q_norm_g, k_norm_g,
              cmp_pe, cmp_w, out_norm_g, w_out, ffn_w_gate, ffn_w_up, ffn_w_down, router_w, router_b,
              moe_w_gate, moe_w_up, moe_w_down):
    T = x_prompt.shape[1]
    Tn = x_sample.shape[1]
    DB = x_sample.shape[0]
    past = page_table.shape[1] * cache_cmp.shape[2]
    pos_p = jnp.arange(T, dtype=jnp.int32)
    pos_s = past + jnp.arange(Tn, dtype=jnp.int32)
    wb_p = min(WINDOW, T)
    yp, ys = x_prompt, x_sample
    cmp_p, cmp_s, slc_p, slc_s, win_p, win_s, chv_s = [], [], [], [], [], [], []
    for l in range(DEPTH):
        mp = adaln(c_prompt, w_ada[l], b_ada[l])
        ms = adaln(c_sample, w_ada[l], b_ada[l])
        hp = modulate(rms_norm(yp, norm_mix_g[l]), mp[0], mp[1])
        u, v, q, rows, gates = project(hp, pos_p, w_in[l], q_norm_g[l], k_norm_g[l])
        a = spatial_gate(u, v, w_spatial[l], b_spatial[l])
        o = nsa_prompt(q, gates, rows, cmp_pe[l], cmp_w[l])
        yp = yp + mp[2] * mix_out(a, o, out_norm_g[l], w_out[l])
        cmp_p.append(rows[:, :, 0])
        slc_p.append(rows[:, :, 1])
        win_p.append(rows[:, T - wb_p:, 2])
        hs = modulate(rms_norm(ys, norm_mix_g[l]), ms[0], ms[1])
        u, v, q, rows, gates = project(hs, pos_s, w_in[l], q_norm_g[l], k_norm_g[l])
        a = spatial_gate(u, v, w_spatial[l], b_spatial[l])
        o, win_new = nsa_sample(q, gates, rows, cache_cmp[l], cache_slc[l], state_win[l], page_table,
                                cmp_pe[l], cmp_w[l])
        ys = ys + ms[2] * mix_out(a, o, out_norm_g[l], w_out[l])
        cmp_s.append(rows[:, :, 0])
        slc_s.append(rows[:, :, 1])
        win_s.append(win_new)
        chv_s.append(v.reshape(DB, Tn, D_A))
        yp = yp + mp[5] * channel_mixer(modulate(rms_norm(yp, norm_ffn_g[l]), mp[3], mp[4]), l,
                                        ffn_w_gate, ffn_w_up, ffn_w_down, router_w, router_b,
                                        moe_w_gate, moe_w_up, moe_w_down)
        ys = ys + ms[5] * channel_mixer(modulate(rms_norm(ys, norm_ffn_g[l]), ms[3], ms[4]), l,
                                        ffn_w_gate, ffn_w_up, ffn_w_down, router_w, router_b,
                                        moe_w_gate, moe_w_up, moe_w_down)
    return (yp, ys, jnp.stack(cmp_p), jnp.stack(cmp_s), jnp.stack(slc_p), jnp.stack(slc_s),
            jnp.stack(win_p), jnp.stack(win_s), jnp.stack(chv_s))
```

```python
import functools

import numpy as np
import jax
import jax.numpy as jnp
from jax import lax
from jax.experimental import pallas as pl
from jax.experimental.pallas import tpu as pltpu

F32 = jnp.float32
BF = jnp.bfloat16

HEAD_DIM = 64
B_KV_HEADS = 2
GQA_R = 4
N_BRANCH = 3
CHUNK = 128
CMP_BLOCK = 32
CMP_STRIDE = 16
SLC_BLOCK = 64
N_SELECT = 16
WINDOW = 512
Q_BLOCK = 128
ROPE_THETA = 10000.0
TOP_K = 2
MOE_BLOCK = 128
NEG_INF = -1e30
FORCE_SCORE = 1e30
EPS = 1e-6

LANES = 128
KEY_TILE = 512
VMEM_LIMIT = 56 * 1024 * 1024


def _cparams(sem):
    return pltpu.CompilerParams(dimension_semantics=sem, vmem_limit_bytes=VMEM_LIMIT)


def _dot(a, b):
    return jnp.dot(a, b, preferred_element_type=F32)


def _dot_nt(a, b):
    return lax.dot_general(a, b, (((1,), (1,)), ((), ())), preferred_element_type=F32)


def _split_dot(x, w):
    hi = x.astype(BF)
    lo = (x - hi.astype(F32)).astype(BF)
    return _dot(hi, w) + _dot(lo, w)


def _ada_kernel(c_ref, w_ref, b_ref, o_ref):
    c = c_ref[...]
    s = c * jax.nn.sigmoid(c)
    o_ref[0] = _dot(s.astype(BF), w_ref[0].astype(BF)) + b_ref[0]


def _adaln(c_all, w_ada, b_ada):
    depth, d, n6 = w_ada.shape
    m = c_all.shape[0]
    tn = 1536
    return pl.pallas_call(
        _ada_kernel,
        grid=(depth, n6 // tn),
        in_specs=[pl.BlockSpec((m, d), lambda l, j: (0, 0)),
                  pl.BlockSpec((1, d, tn), lambda l, j: (l, 0, j)),
                  pl.BlockSpec((1, 1, tn), lambda l, j: (l, 0, j))],
        out_specs=pl.BlockSpec((1, m, tn), lambda l, j: (l, 0, j)),
        out_shape=jax.ShapeDtypeStruct((depth, m, n6), F32),
        compiler_params=_cparams(("arbitrary", "arbitrary")),
        name="adaln",
    )(c_all, w_ada, b_ada.reshape(depth, 1, n6))


def _pre_kernel(x_ref, g_ref, sh_ref, sc_ref, w_ref, cos_ref, sin_ref, bd_ref, qg_ref, kg_ref,
                ws_ref, bs_ref, *out_refs, chunked, d_a, d_b):
    if chunked:
        a_ref, q_ref, cmp_ref, slc_ref, win_ref, slcb_ref, winb_ref, gate_ref = out_refs
    else:
        a_ref, q_ref, cmp_ref, slc_ref, win_ref, gate_ref, v_ref = out_refs
    tm = x_ref.shape[0]
    x = x_ref[...]
    ms = jnp.mean(x * x, axis=-1, keepdims=True)
    h = x * lax.rsqrt(ms + EPS) * g_ref[...]
    h = h * (1.0 + sc_ref[...]) + sh_ref[...]
    z = _dot(h.astype(BF), w_ref[...])
    uv = jax.nn.gelu(z[:, :2 * d_a])
    u = uv[:, :d_a]
    v = uv[:, d_a:]
    bd = bd_ref[...]
    mu = _split_dot(v, bd)
    vc = v - mu
    vn = vc * lax.rsqrt(_split_dot(vc * vc, bd) + EPS)

    cos = cos_ref[...]
    sin = sin_ref[...]
    lane = lax.broadcasted_iota(jnp.int32, (1, LANES), 1)
    first_half = (lane % HEAD_DIM) < (HEAD_DIM // 2)

    def rope(t):
        swapped = jnp.where(first_half, pltpu.roll(t, LANES - HEAD_DIM // 2, 1),
                            pltpu.roll(t, HEAD_DIM // 2, 1))
        return t * cos + swapped * sin

    o = 2 * d_a
    zq = z[:, o:o + d_b]
    qn = zq * lax.rsqrt(_split_dot(zq * zq, bd) + EPS) * qg_ref[...]
    for c in range(d_b // LANES):
        q_ref[:, c * LANES:(c + 1) * LANES] = rope(qn[:, c * LANES:(c + 1) * LANES]) * (HEAD_DIM ** -0.5)
    o += d_b
    bd1 = bd_ref[0:LANES, 0:LANES]
    row_refs = (cmp_ref, slc_ref, win_ref)
    bf_refs = (None, slcb_ref, winb_ref) if chunked else (None, None, None)
    for br in range(N_BRANCH):
        kp = z[:, o:o + LANES]
        vp = z[:, o + LANES:o + 2 * LANES]
        kn = kp * lax.rsqrt(_split_dot(kp * kp, bd1) + EPS) * kg_ref[br:br + 1, :]
        kr = rope(kn)
        row_refs[br][:, 0:LANES] = kr
        row_refs[br][:, LANES:2 * LANES] = vp
        if bf_refs[br] is not None:
            bf_refs[br][:, 0:LANES] = kr.astype(BF)
            bf_refs[br][:, LANES:2 * LANES] = vp.astype(BF)
        o += 2 * LANES
    gate_ref[...] = jax.nn.sigmoid(z[:, o:o + LANES])

    if chunked:
        low = lane < HEAD_DIM
        for c in range(tm // CHUNK):
            rs = slice(c * CHUNK, (c + 1) * CHUNK)
            for j in range(d_a // LANES):
                cs = slice(j * LANES, (j + 1) * LANES)
                vpair = vn[rs, cs].astype(BF)
                m0 = _dot(ws_ref[2 * j], vpair)
                m1 = _dot(ws_ref[2 * j + 1], vpair)
                mixed = jnp.where(low, m0, m1) + bs_ref[:, cs]
                a_ref[rs, cs] = u[rs, cs] * mixed
    else:
        a_ref[...] = u * (vn * ws_ref[...] + bs_ref[...])
        v_ref[...] = vn


def _pre_mix(x, norm_g, shift, scale, w_in_bf, cos, sin, bd, qg, kg, ws, bs, *, chunked, tokens_per_batch):
    n, d = x.shape
    d_a = d_b = d // 2
    tm = 256 if chunked else n
    nt = n // tm
    pw = w_in_bf.shape[1]
    full = lambda shp: pl.BlockSpec(shp, lambda i: (0,) * len(shp))
    row = lambda w: pl.BlockSpec((tm, w), lambda i: (i, 0))
    if chunked:
        tpb = tokens_per_batch // tm
        mod_spec = pl.BlockSpec((None, 1, d), lambda i: (i // tpb, 0, 0))
        rot_spec = pl.BlockSpec((tm, LANES), lambda i: (i % tpb, 0))
        ws_spec, bs_spec = full(ws.shape), full(bs.shape)
        out_shape = [jax.ShapeDtypeStruct((n, d_a), F32), jax.ShapeDtypeStruct((n, d_b), F32),
                     jax.ShapeDtypeStruct((n, 2 * LANES), F32), jax.ShapeDtypeStruct((n, 2 * LANES), F32),
                     jax.ShapeDtypeStruct((n, 2 * LANES), F32), jax.ShapeDtypeStruct((n, 2 * LANES), BF),
                     jax.ShapeDtypeStruct((n, 2 * LANES), BF), jax.ShapeDtypeStruct((n, LANES), F32)]
        out_specs = [row(d_a), row(d_b), row(2 * LANES), row(2 * LANES), row(2 * LANES), row(2 * LANES),
                     row(2 * LANES), row(LANES)]
    else:
        mod_spec = row(d)
        rot_spec = full((1, LANES))
        ws_spec, bs_spec = full(ws.shape), full(bs.shape)
        out_shape = [jax.ShapeDtypeStruct((n, d_a), F32), jax.ShapeDtypeStruct((n, d_b), F32),
                     jax.ShapeDtypeStruct((n, 2 * LANES), F32), jax.ShapeDtypeStruct((n, 2 * LANES), F32),
                     jax.ShapeDtypeStruct((n, 2 * LANES), F32), jax.ShapeDtypeStruct((n, LANES), F32),
                     jax.ShapeDtypeStruct((n, d_a), F32)]
        out_specs = [row(d_a), row(d_b), row(2 * LANES), row(2 * LANES), row(2 * LANES), row(LANES), row(d_a)]
    return pl.pallas_call(
        functools.partial(_pre_kernel, chunked=chunked, d_a=d_a, d_b=d_b),
        grid=(nt,),
        in_specs=[row(d), full((1, d)), mod_spec, mod_spec, full((d, pw)), rot_spec, rot_spec,
                  full(bd.shape), full(qg.shape), full(kg.shape), ws_spec, bs_spec],
        out_specs=out_specs,
        out_shape=out_shape,
        compiler_params=_cparams(("arbitrary",)),
        name="pre_mix_prompt" if chunked else "pre_mix_sample",
    )(x, norm_g, shift, scale, w_in_bf, cos, sin, bd, qg, kg, ws, bs)


def _compress_pieces(load_piece_rows, w_ref, pe_ref, kv, n_piece):
    assert CMP_BLOCK == 2 * CMP_STRIDE
    acc_a = jnp.zeros((n_piece, LANES), F32)
    acc_b = jnp.zeros((n_piece, LANES), F32)
    for l in range(CMP_STRIDE):
        xl = load_piece_rows(l)
        l2 = CMP_STRIDE + l
        acc_a = acc_a + _dot((xl + pe_ref[kv, l:l + 1, :]).astype(BF), w_ref[kv, l])
        acc_b = acc_b + _dot((xl + pe_ref[kv, l2:l2 + 1, :]).astype(BF), w_ref[kv, l2])
    shifted = pltpu.roll(acc_b, n_piece - 1, 0)
    rows = lax.broadcasted_iota(jnp.int32, (n_piece, 1), 0)
    return jnp.where(rows < n_piece - 1, acc_a + shifted, 0.0)


def _compress_kernel(xk_ref, xv_ref, w_ref, pe_ref, kc_ref, vc_ref):
    n_piece = kc_ref.shape[0]
    for kv, (x_ref, o_ref) in enumerate(((xk_ref, kc_ref), (xv_ref, vc_ref))):
        out = _compress_pieces(lambda l: x_ref[pl.ds(l, n_piece, stride=CMP_STRIDE), :], w_ref, pe_ref, kv, n_piece)
        o_ref[...] = out.astype(BF)


def _compress_prompt(cmp_rows, wbig, pebig, batch):
    n = cmp_rows.shape[0]
    t = n // batch
    n_piece = t // CMP_STRIDE
    out = jax.ShapeDtypeStruct((batch * n_piece, LANES), BF)
    return pl.pallas_call(
        _compress_kernel,
        grid=(batch,),
        in_specs=[pl.BlockSpec((t, LANES), lambda b: (b, 0)),
                  pl.BlockSpec((t, LANES), lambda b: (b, 1)),
                  pl.BlockSpec(wbig.shape, lambda b: (0, 0, 0, 0)),
                  pl.BlockSpec(pebig.shape, lambda b: (0, 0, 0))],
        out_specs=[pl.BlockSpec((n_piece, LANES), lambda b: (b, 0))] * 2,
        out_shape=[out, out],
        compiler_params=_cparams(("arbitrary",)),
        name="compress_prompt",
    )(cmp_rows, cmp_rows, wbig, pebig)


def _group_queries(q, g):
    m = q.shape[0]
    low = lax.broadcasted_iota(jnp.int32, (m, LANES), 1) < HEAD_DIM
    keep = low if g == 0 else jnp.logical_not(low)
    parts = []
    for r in range(GQA_R):
        h = GQA_R * g + r
        t = q[:, (h // 2) * LANES:(h // 2 + 1) * LANES]
        if h % 2 != g:
            t = pltpu.roll(t, HEAD_DIM, 1)
        parts.append(jnp.where(keep, t, 0.0))
    return parts


def _top_select(imp, blk_f):
    sel = jnp.zeros(imp.shape, F32)
    work = imp
    for _ in range(N_SELECT):
        mx = jnp.max(work, axis=-1, keepdims=True)
        first = jnp.min(jnp.where(work == mx, blk_f, float(LANES)), axis=-1, keepdims=True)
        hit = blk_f == first
        sel = jnp.where(hit, 1.0, sel)
        work = jnp.where(hit, -jnp.inf, work)
    return sel


def _block_scores(imp, qpos):
    blk = lax.broadcasted_iota(jnp.int32, imp.shape, 1)
    cur = qpos // SLC_BLOCK
    forced = (blk == 0) | (blk == cur) | (blk == cur - 1)
    valid = blk <= cur
    imp = jnp.where(forced, FORCE_SCORE, imp)
    imp = jnp.where(valid, imp, NEG_INF)
    return imp, blk.astype(F32), valid


def _merge_heads(res, m):
    low = lax.broadcasted_iota(jnp.int32, (m, LANES), 1) < HEAD_DIM
    cols = []
    for c in range(4):
        a, b = res[2 * c], res[2 * c + 1]
        if c // 2 == 0:
            cols.append(jnp.where(low, a, pltpu.roll(b, HEAD_DIM, 1)))
        else:
            cols.append(jnp.where(low, pltpu.roll(a, HEAD_DIM, 1), b))
    return jnp.concatenate(cols, axis=1)


def _nsa_prompt_kernel(q_ref, gate_ref, kc_ref, vc_ref, slc_ref, win_ref, cover_ref, e_ref, o_ref):
    i = pl.program_id(1)
    s0 = i * Q_BLOCK
    n_cmp = kc_ref.shape[0]
    q = q_ref[...]
    gates = gate_ref[...]
    tpos = s0 + lax.broadcasted_iota(jnp.int32, (Q_BLOCK, 1), 0)
    tpos4 = jnp.concatenate([tpos] * GQA_R, axis=0)
    kc = kc_ref[...]
    vc = vc_ref[...]
    c_last = lax.broadcasted_iota(jnp.int32, (1, n_cmp), 1) * CMP_STRIDE + (CMP_BLOCK - 1)
    cv = c_last <= tpos4
    n_tiles = (s0 + Q_BLOCK + KEY_TILE - 1) // KEY_TILE
    w_start = pl.multiple_of(jnp.maximum(s0 - WINDOW, 0), Q_BLOCK)
    w_len = WINDOW + Q_BLOCK
    wpos = w_start + lax.broadcasted_iota(jnp.int32, (1, w_len), 1)
    w_bias = jnp.where((wpos <= tpos) & (wpos > tpos - WINDOW), 0.0, NEG_INF)
    w_bias4 = jnp.concatenate([w_bias] * GQA_R, axis=0)
    res = []
    for g in range(B_KV_HEADS):
        qg = jnp.concatenate(_group_queries(q, g), axis=0).astype(BF)
        s = jnp.where(cv, _dot_nt(qg, kc), NEG_INF)
        e = jnp.where(cv, jnp.exp(s - jnp.max(s, axis=-1, keepdims=True)), 0.0)
        p = e / jnp.maximum(jnp.sum(e, axis=-1, keepdims=True), 1e-30)
        o_cmp = _dot(p.astype(BF), vc)
        p_sum = p[0:Q_BLOCK]
        for r in range(1, GQA_R):
            p_sum = p_sum + p[r * Q_BLOCK:(r + 1) * Q_BLOCK]
        imp, blk_f, valid = _block_scores(_split_dot(p_sum, cover_ref[...]), tpos)
        sel = _top_select(imp, blk_f)
        sel_b = jnp.where(valid, sel, 0.0).astype(BF)

        def body(kt, carry):
            m, l, acc = carry
            k0 = pl.multiple_of(kt * KEY_TILE, KEY_TILE)
            kk = slc_ref[pl.ds(k0, KEY_TILE), 0:LANES]
            vv = slc_ref[pl.ds(k0, KEY_TILE), LANES:2 * LANES]
            member = _dot(sel_b, e_ref[:, pl.ds(k0, KEY_TILE)])
            kpos = k0 + lax.broadcasted_iota(jnp.int32, (1, KEY_TILE), 1)
            bias = jnp.where((member > 0.5) & (kpos <= tpos), 0.0, NEG_INF)
            sc = _dot_nt(qg, kk) + jnp.concatenate([bias] * GQA_R, axis=0)
            m_new = jnp.maximum(m, jnp.max(sc, axis=-1, keepdims=True))
            alpha = jnp.exp(m - m_new)
            pe = jnp.exp(sc - m_new)
            l = alpha * l + jnp.sum(pe, axis=-1, keepdims=True)
            acc = alpha * acc + _dot(pe.astype(BF), vv)
            return m_new, l, acc

        m0 = jnp.full((GQA_R * Q_BLOCK, 1), NEG_INF, F32)
        l0 = jnp.zeros((GQA_R * Q_BLOCK, 1), F32)
        a0 = jnp.zeros((GQA_R * Q_BLOCK, LANES), F32)
        _, l_s, acc_s = lax.fori_loop(0, n_tiles, body, (m0, l0, a0))
        o_slc = acc_s / l_s
        kw = win_ref[pl.ds(w_start, w_len), 0:LANES]
        vw = win_ref[pl.ds(w_start, w_len), LANES:2 * LANES]
        sw = _dot_nt(qg, kw) + w_bias4
        ew = jnp.exp(sw - jnp.max(sw, axis=-1, keepdims=True))
        o_win = _dot(ew.astype(BF), vw) / jnp.sum(ew, axis=-1, keepdims=True)
        for r in range(GQA_R):
            h = GQA_R * g + r
            rs = slice(r * Q_BLOCK, (r + 1) * Q_BLOCK)
            res.append(gates[:, 3 * h:3 * h + 1] * o_cmp[rs] + gates[:, 3 * h + 1:3 * h + 2] * o_slc[rs]
                       + gates[:, 3 * h + 2:3 * h + 3] * o_win[rs])
    o_ref[...] = _merge_heads(res, Q_BLOCK)


def _nsa_prompt(q, gates, kc, vc, slc_bf, win_bf, cover, expand, batch):
    n, d_b = q.shape
    t = n // batch
    nq = t // Q_BLOCK
    n_cmp = kc.shape[0] // batch
    return pl.pallas_call(
        _nsa_prompt_kernel,
        grid=(batch, nq),
        in_specs=[pl.BlockSpec((Q_BLOCK, d_b), lambda b, i: (b * nq + i, 0)),
                  pl.BlockSpec((Q_BLOCK, LANES), lambda b, i: (b * nq + i, 0)),
                  pl.BlockSpec((n_cmp, LANES), lambda b, i: (b, 0)),
                  pl.BlockSpec((n_cmp, LANES), lambda b, i: (b, 0)),
                  pl.BlockSpec((t, 2 * LANES), lambda b, i: (b, 0)),
                  pl.BlockSpec((t, 2 * LANES), lambda b, i: (b, 0)),
                  pl.BlockSpec(cover.shape, lambda b, i: (0, 0)),
                  pl.BlockSpec(expand.shape, lambda b, i: (0, 0))],
        out_specs=pl.BlockSpec((Q_BLOCK, d_b), lambda b, i: (b * nq + i, 0)),
        out_shape=jax.ShapeDtypeStruct((n, d_b), F32),
        compiler_params=_cparams(("arbitrary", "arbitrary")),
        name="nsa_prompt",
    )(q, gates, kc, vc, slc_bf, win_bf, cover, expand)


def _sample_queries(q_row):
    sub = lax.broadcasted_iota(jnp.int32, (8, LANES), 0)
    out = jnp.zeros((8, LANES), F32)
    for g in range(B_KV_HEADS):
        for r, part in enumerate(_group_queries(q_row, g)):
            out = jnp.where(sub == GQA_R * g + r, jnp.broadcast_to(part, (8, LANES)), out)
    return out


def _s_cmp_kernel(pt_ref, q_ref, *refs, n_pages, qpos):
    k_pages = refs[:n_pages]
    v_pages = refs[n_pages:2 * n_pages]
    wbig_ref, pe_ref, cover_ref, ocmp_ref, imp_ref = refs[2 * n_pages:]
    per_page = k_pages[0].shape[0] // CMP_STRIDE
    n_piece = n_pages * per_page

    def loader(pages):
        return lambda l: jnp.concatenate([pg[pl.ds(l, per_page, stride=CMP_STRIDE), :] for pg in pages], axis=0)

    kc = _compress_pieces(loader(k_pages), wbig_ref, pe_ref, 0, n_piece).astype(BF)
    vc = _compress_pieces(loader(v_pages), wbig_ref, pe_ref, 1, n_piece).astype(BF)
    qa = _sample_queries(q_ref[...])
    c_last = lax.broadcasted_iota(jnp.int32, (1, n_piece), 1) * CMP_STRIDE + (CMP_BLOCK - 1)
    cv = c_last <= qpos
    s = jnp.where(cv, _dot_nt(qa.astype(BF), kc), NEG_INF)
    e = jnp.where(cv, jnp.exp(s - jnp.max(s, axis=-1, keepdims=True)), 0.0)
    p = e / jnp.maximum(jnp.sum(e, axis=-1, keepdims=True), 1e-30)
    ocmp_ref[...] = _dot(p.astype(BF), vc)
    sub = lax.broadcasted_iota(jnp.int32, (8, n_piece), 0)
    ps0 = jnp.sum(jnp.where(sub < GQA_R, p, 0.0), axis=0, keepdims=True)
    ps1 = jnp.sum(jnp.where(sub >= GQA_R, p, 0.0), axis=0, keepdims=True)
    p_sum = jnp.where(sub == 0, ps0, jnp.where(sub == 1, ps1, 0.0))
    imp_ref[...] = _split_dot(p_sum, cover_ref[...])


def _s_cmp(page_table, q3, cache_l, layer, wbig, pebig, cover, qpos):
    db = q3.shape[0]
    n_pages = page_table.shape[1]
    page = cache_l.shape[2]
    n_piece = n_pages * page // CMP_STRIDE
    page_specs = [pl.BlockSpec((None, None, page, LANES),
                               functools.partial(lambda b, pt, j, kv: (layer, pt[b, j], 0, kv), j=j, kv=kv))
                  for kv in range(2) for j in range(n_pages)]
    gs = pltpu.PrefetchScalarGridSpec(
        num_scalar_prefetch=1,
        grid=(db,),
        in_specs=[pl.BlockSpec((None, 1, q3.shape[2]), lambda b, pt: (b, 0, 0))] + page_specs + [
            pl.BlockSpec(wbig.shape, lambda b, pt: (0, 0, 0, 0)),
            pl.BlockSpec(pebig.shape, lambda b, pt: (0, 0, 0)),
            pl.BlockSpec(cover.shape, lambda b, pt: (0, 0))],
        out_specs=[pl.BlockSpec((None, 8, LANES), lambda b, pt: (b, 0, 0)),
                   pl.BlockSpec((None, 8, LANES), lambda b, pt: (b, 0, 0))],
    )
    assert cover.shape[0] == n_piece
    return pl.pallas_call(
        functools.partial(_s_cmp_kernel, n_pages=n_pages, qpos=qpos),
        grid_spec=gs,
        out_shape=[jax.ShapeDtypeStruct((db, 8, LANES), F32), jax.ShapeDtypeStruct((db, 8, LANES), F32)],
        compiler_params=_cparams(("arbitrary",)),
        name="nsa_sample_cmp",
    )(page_table, q3, *([cache_l] * (2 * n_pages)), wbig, pebig, cover)


def _s_select_kernel(imp_ref, e_ref, o_ref, *, qpos):
    imp = imp_ref[...]
    pos = jnp.full((imp.shape[0], 1), qpos, jnp.int32)
    imp, blk_f, valid = _block_scores(imp, pos)
    sel = _top_select(imp, blk_f)
    sel_b = jnp.where(valid, sel, 0.0).astype(BF)
    o_ref[...] = _dot(sel_b, e_ref[...])


def _s_select(imp2, expand, qpos):
    m = imp2.shape[0]
    past = expand.shape[1]
    return pl.pallas_call(
        functools.partial(_s_select_kernel, qpos=qpos),
        grid=(1,),
        in_specs=[pl.BlockSpec(imp2.shape, lambda i: (0, 0)), pl.BlockSpec(expand.shape, lambda i: (0, 0))],
        out_specs=pl.BlockSpec((m, past), lambda i: (0, 0)),
        out_shape=jax.ShapeDtypeStruct((m, past), F32),
        compiler_params=_cparams(("arbitrary",)),
        name="nsa_sample_select",
    )(imp2, expand)


def _s_att_kernel(pt_ref, q_ref, gate_ref, mask_ref, ocmp_ref, slcn_ref, winn_ref, win_ref, *refs,
                  n_pages, qpos, past):
    pages = refs[:n_pages]
    o_ref = refs[n_pages]
    page = pages[0].shape[0]
    qa = _sample_queries(q_ref[...])
    qb = qa.astype(BF)
    sub = lax.broadcasted_iota(jnp.int32, (8, 1), 0)
    mrow = jnp.where(sub < GQA_R, mask_ref[0:1, :], mask_ref[1:2, :])
    bias = jnp.where(mrow > 0.5, 0.0, NEG_INF)
    sc = jnp.concatenate([_dot_nt(qb, pg[:, 0:LANES].astype(BF)) for pg in pages], axis=1) + bias
    kn = slcn_ref[:, 0:LANES]
    vn = slcn_ref[:, LANES:2 * LANES]
    s_new = jnp.sum(qa * kn, axis=-1, keepdims=True)
    m = jnp.maximum(jnp.max(sc, axis=-1, keepdims=True), s_new)
    e = jnp.exp(sc - m)
    e_new = jnp.exp(s_new - m)
    acc = e_new * vn
    for j, pg in enumerate(pages):
        acc = acc + _dot(e[:, j * page:(j + 1) * page].astype(BF), pg[:, LANES:2 * LANES].astype(BF))
    o_slc = acc / (jnp.sum(e, axis=-1, keepdims=True) + e_new)
    wb = win_ref.shape[0]
    wpos = (past - wb) + lax.broadcasted_iota(jnp.int32, (1, wb), 1)
    wvalid = (wpos <= qpos) & (wpos > qpos - WINDOW) & (wpos >= 0)
    sw = _dot_nt(qb, win_ref[:, 0:LANES].astype(BF)) + jnp.where(wvalid, 0.0, NEG_INF)
    kwn = winn_ref[:, 0:LANES]
    vwn = winn_ref[:, LANES:2 * LANES]
    sw_new = jnp.sum(qa * kwn, axis=-1, keepdims=True)
    mw = jnp.maximum(jnp.max(sw, axis=-1, keepdims=True), sw_new)
    ew = jnp.exp(sw - mw)
    ew_new = jnp.exp(sw_new - mw)
    o_win = (_dot(ew.astype(BF), win_ref[:, LANES:2 * LANES].astype(BF)) + ew_new * vwn) / (
        jnp.sum(ew, axis=-1, keepdims=True) + ew_new)
    lane = lax.broadcasted_iota(jnp.int32, (8, LANES), 1)
    gt = jnp.broadcast_to(gate_ref[...], (8, LANES))
    gsel = [jnp.sum(jnp.where(lane == 3 * sub + br, gt, 0.0), axis=-1, keepdims=True) for br in range(N_BRANCH)]
    res = gsel[0] * ocmp_ref[...] + gsel[1] * o_slc + gsel[2] * o_win
    o_ref[...] = _merge_heads([res[h:h + 1, :] for h in range(8)], 1)


def _s_att(page_table, q3, gates3, mask3, ocmp, slc_new3, win_new3, win_l, cache_l, layer, qpos, past):
    db = q3.shape[0]
    n_pages = page_table.shape[1]
    page = cache_l.shape[2]
    wb = win_l.shape[2]
    page_specs = [pl.BlockSpec((None, None, page, 2 * LANES),
                               functools.partial(lambda b, pt, j: (layer, pt[b, j], 0, 0), j=j))
                  for j in range(n_pages)]
    per_b = lambda r, w: pl.BlockSpec((None, r, w), lambda b, pt: (b, 0, 0))
    gs = pltpu.PrefetchScalarGridSpec(
        num_scalar_prefetch=1,
        grid=(db,),
        in_specs=[per_b(1, q3.shape[2]), per_b(1, LANES), per_b(2, past), per_b(8, LANES),
                  per_b(1, 2 * LANES), per_b(1, 2 * LANES),
                  pl.BlockSpec((None, None, wb, 2 * LANES), lambda b, pt: (layer, b, 0, 0))] + page_specs,
        out_specs=per_b(1, q3.shape[2]),
    )
    return pl.pallas_call(
        functools.partial(_s_att_kernel, n_pages=n_pages, qpos=qpos, past=past),
        grid_spec=gs,
        out_shape=jax.ShapeDtypeStruct(q3.shape, F32),
        compiler_params=_cparams(("arbitrary",)),
        name="nsa_sample_att",
    )(page_table, q3, gates3, mask3, ocmp, slc_new3, win_new3, win_l, *([cache_l] * n_pages))


def _post_kernel(a_ref, o_ref, x_ref, g1_ref, og_ref, wout_ref, bd_ref, ng_ref, sh2_ref, sc2_ref, y_ref, h2_ref):
    bd = bd_ref[...]
    d_a = a_ref.shape[1]

    def head_norm(t, g):
        return t * lax.rsqrt(_split_dot(t * t, bd) + EPS) * g

    a = head_norm(a_ref[...], og_ref[:, :d_a])
    o = head_norm(o_ref[...], og_ref[:, d_a:])
    mix = _dot(a.astype(BF), wout_ref[:d_a, :]) + _dot(o.astype(BF), wout_ref[d_a:, :])
    y = x_ref[...] + g1_ref[...] * mix
    y_ref[...] = y
    ms = jnp.mean(y * y, axis=-1, keepdims=True)
    h2 = y * lax.rsqrt(ms + EPS) * ng_ref[...]
    h2_ref[...] = h2 * (1.0 + sc2_ref[...]) + sh2_ref[...]


def _mod_spec(per_batch, tm, d, tokens_per_batch):
    if per_batch:
        tpb = tokens_per_batch // tm
        return pl.BlockSpec((None, 1, d), lambda i: (i // tpb, 0, 0))
    return pl.BlockSpec((tm, d), lambda i: (i, 0))


def _post_mix(a, o, x, gate1, out_g, wout_bf, bd, ffn_g, shift2, scale2, *, per_batch, tokens_per_batch):
    n, d = x.shape
    tm = 512 if per_batch else n
    full = lambda shp: pl.BlockSpec(shp, lambda i: (0,) * len(shp))
    row = lambda w: pl.BlockSpec((tm, w), lambda i: (i, 0))
    ms = _mod_spec(per_batch, tm, d, tokens_per_batch)
    return pl.pallas_call(
        _post_kernel,
        grid=(n // tm,),
        in_specs=[row(a.shape[1]), row(o.shape[1]), row(d), ms, full((1, d)), full(wout_bf.shape), full(bd.shape),
                  full((1, d)), ms, ms],
        out_specs=[row(d), row(d)],
        out_shape=[jax.ShapeDtypeStruct((n, d), F32), jax.ShapeDtypeStruct((n, d), F32)],
        compiler_params=_cparams(("arbitrary",)),
        name="post_mix",
    )(a, o, x, gate1, out_g, wout_bf, bd, ffn_g, shift2, scale2)


def _ffn_kernel(h_ref, y_ref, g2_ref, wg_ref, wu_ref, wd_ref, o_ref, *, f_tile):
    hb = h_ref[...].astype(BF)
    d_ff = wg_ref.shape[1]
    acc = jnp.zeros(o_ref.shape, F32)
    for f in range(d_ff // f_tile):
        fs = slice(f * f_tile, (f + 1) * f_tile)
        gp = _dot(hb, wg_ref[:, fs])
        up = _dot(hb, wu_ref[:, fs])
        act = gp * jax.nn.sigmoid(gp) * up
        acc = acc + _dot(act.astype(BF), wd_ref[fs, :])
    o_ref[...] = y_ref[...] + g2_ref[...] * acc


def _ff_tile(d_ff):
    for cand in (1408, 1024, 512, 256, 128):
        if d_ff % cand == 0:
            return cand
    return d_ff


def _ffn_dense(h2, y, gate2, wg, wu, wd, *, per_batch, tokens_per_batch):
    n, d = y.shape
    tm = 512 if per_batch else n
    full = lambda shp: pl.BlockSpec(shp, lambda i: (0,) * len(shp))
    row = lambda w: pl.BlockSpec((tm, w), lambda i: (i, 0))
    return pl.pallas_call(
        functools.partial(_ffn_kernel, f_tile=_ff_tile(wg.shape[1])),
        grid=(n // tm,),
        in_specs=[row(d), row(d), _mod_spec(per_batch, tm, d, tokens_per_batch), full(wg.shape), full(wu.shape),
                  full(wd.shape)],
        out_specs=row(d),
        out_shape=jax.ShapeDtypeStruct((n, d), F32),
        compiler_params=_cparams(("arbitrary",)),
        name="ffn_dense",
    )(h2, y, gate2, wg, wu, wd)


def _router_kernel(h_ref, rwh_ref, rwl_ref, rb_ref, o_ref, *, n_experts):
    h = h_ref[...]
    hi = h.astype(BF)
    lo = (h - hi.astype(F32)).astype(BF)
    logits = _dot(hi, rwh_ref[...]) + _dot(lo, rwh_ref[...]) + _dot(hi, rwl_ref[...]) + rb_ref[...]
    lane = lax.broadcasted_iota(jnp.int32, logits.shape, 1)
    lane_f = lane.astype(F32)
    l1 = jnp.where(lane < n_experts, logits, -jnp.inf)
    m1 = jnp.max(l1, axis=-1, keepdims=True)
    i1 = jnp.min(jnp.where(l1 == m1, lane_f, float(LANES)), axis=-1, keepdims=True)
    l2 = jnp.where(lane_f == i1, -jnp.inf, l1)
    m2 = jnp.max(l2, axis=-1, keepdims=True)
    i2 = jnp.min(jnp.where(l2 == m2, lane_f, float(LANES)), axis=-1, keepdims=True)
    e2 = jnp.exp(m2 - m1)
    den = 1.0 + e2
    o_ref[...] = jnp.where(lane == 0, i1, jnp.where(lane == 1, i2, jnp.where(lane == 2, 1.0 / den,
                           jnp.where(lane == 3, e2 / den, 0.0))))


def _router(h2, rw_hi, rw_lo, rb, n_experts):
    n, d = h2.shape
    tm = 512 if n % 512 == 0 else n
    full = lambda shp: pl.BlockSpec(shp, lambda i: (0,) * len(shp))
    return pl.pallas_call(
        functools.partial(_router_kernel, n_experts=n_experts),
        grid=(n // tm,),
        in_specs=[pl.BlockSpec((tm, d), lambda i: (i, 0)), full(rw_hi.shape), full(rw_lo.shape), full(rb.shape)],
        out_specs=pl.BlockSpec((tm, LANES), lambda i: (i, 0)),
        out_shape=jax.ShapeDtypeStruct((n, LANES), F32),
        compiler_params=_cparams(("arbitrary",)),
        name="moe_router",
    )(h2, rw_hi, rw_lo, rb)


def _gather_kernel(idx_ref, src_ref, out_ref, sem, *, rows_per_step):
    base = pl.program_id(0) * rows_per_step

    def row_copy(j):
        return pltpu.make_async_copy(src_ref.at[pl.ds(idx_ref[base + j], 1)], out_ref.at[pl.ds(base + j, 1)], sem)

    def start(j, c):
        row_copy(j).start()
        return c

    def wait(j, c):
        row_copy(j).wait()
        return c

    lax.fori_loop(0, rows_per_step, start, 0)
    lax.fori_loop(0, rows_per_step, wait, 0)


def _gather_rows(src, idx, rows_per_step=512):
    m = idx.shape[0]
    while m % rows_per_step:
        rows_per_step //= 2
    gs = pltpu.PrefetchScalarGridSpec(
        num_scalar_prefetch=1,
        grid=(m // rows_per_step,),
        in_specs=[pl.BlockSpec(memory_space=pl.ANY)],
        out_specs=pl.BlockSpec(memory_space=pl.ANY),
        scratch_shapes=[pltpu.SemaphoreType.DMA(())],
    )
    return pl.pallas_call(
        functools.partial(_gather_kernel, rows_per_step=rows_per_step),
        grid_spec=gs,
        out_shape=jax.ShapeDtypeStruct((m, src.shape[1]), src.dtype),
        compiler_params=pltpu.CompilerParams(dimension_semantics=("arbitrary",), has_side_effects=True),
        name="gather_rows",
    )(idx, src)


def _moe_block_kernel(be_ref, nu_ref, x_ref, wg_ref, wu_ref, wd_ref, o_ref, *, f_tile):
    i = pl.program_id(0)

    @pl.when(i < nu_ref[0])
    def _():
        hb = x_ref[...].astype(BF)
        d_ff = wg_ref.shape[1]
        acc = jnp.zeros(o_ref.shape, F32)
        for f in range(d_ff // f_tile):
            fs = slice(f * f_tile, (f + 1) * f_tile)
            gp = _dot(hb, wg_ref[:, fs])
            up = _dot(hb, wu_ref[:, fs])
            act = gp * jax.nn.sigmoid(gp) * up
            acc = acc + _dot(act.astype(BF), wd_ref[fs, :])
        o_ref[...] = acc

    @pl.when(i >= nu_ref[0])
    def _():
        o_ref[...] = jnp.zeros(o_ref.shape, F32)


def _moe_blocks(blk_e, n_used, xb, wg, wu, wd):
    m, d = xb.shape
    nb = m // MOE_BLOCK
    d_ff = wg.shape[2]
    gs = pltpu.PrefetchScalarGridSpec(
        num_scalar_prefetch=2,
        grid=(nb,),
        in_specs=[pl.BlockSpec((MOE_BLOCK, d), lambda i, be, nu: (i, 0)),
                  pl.BlockSpec((None, d, d_ff), lambda i, be, nu: (be[i], 0, 0)),
                  pl.BlockSpec((None, d, d_ff), lambda i, be, nu: (be[i], 0, 0)),
                  pl.BlockSpec((None, d_ff, d), lambda i, be, nu: (be[i], 0, 0))],
        out_specs=pl.BlockSpec((MOE_BLOCK, d), lambda i, be, nu: (i, 0)),
    )
    return pl.pallas_call(
        functools.partial(_moe_block_kernel, f_tile=_ff_tile(d_ff)),
        grid_spec=gs,
        out_shape=jax.ShapeDtypeStruct((m, d), F32),
        compiler_params=_cparams(("arbitrary",)),
        name="moe_blocks",
    )(blk_e, n_used, xb, wg, wu, wd)


def _combine_kernel(ya0_ref, ya1_ref, r_ref, y_ref, g2_ref, o_ref):
    w1 = r_ref[:, 2:3]
    w2 = r_ref[:, 3:4]
    o_ref[...] = y_ref[...] + g2_ref[...] * (w1 * ya0_ref[...] + w2 * ya1_ref[...])


def _moe_combine(ya, route, y, gate2, *, per_batch, tokens_per_batch):
    n, d = y.shape
    tm = 512 if per_batch else n
    nt = n // tm
    row = lambda w: pl.BlockSpec((tm, w), lambda i: (i, 0))
    return pl.pallas_call(
        _combine_kernel,
        grid=(nt,),
        in_specs=[row(d), pl.BlockSpec((tm, d), lambda i: (i + nt, 0)), row(LANES), row(d),
                  _mod_spec(per_batch, tm, d, tokens_per_batch)],
        out_specs=row(d),
        out_shape=jax.ShapeDtypeStruct((n, d), F32),
        compiler_params=_cparams(("arbitrary",)),
        name="moe_combine",
    )(ya, ya, route, y, gate2)


def _moe_ffn(h2, y, gate2, rw_hi, rw_lo, rb, wg, wu, wd, *, per_batch, tokens_per_batch):
    n, d = h2.shape
    n_experts = wg.shape[0]
    route = _router(h2, rw_hi, rw_lo, rb, n_experts)
    flat_e = route[:, 0:TOP_K].astype(jnp.int32).reshape(-1)
    onehot = (flat_e[:, None] == jnp.arange(n_experts, dtype=jnp.int32)[None, :]).astype(jnp.int32)
    csum = jnp.cumsum(onehot, axis=0)
    rank = jnp.sum(onehot * csum, axis=1) - 1
    counts = csum[-1]
    padded = (counts + MOE_BLOCK - 1) // MOE_BLOCK * MOE_BLOCK
    pad_end = jnp.cumsum(padded)
    pad_start = pad_end - padded
    slot = (pad_start[flat_e] + rank).astype(jnp.int32)
    nb = -(-(n * TOP_K) // MOE_BLOCK) + n_experts
    src = jnp.zeros((nb * MOE_BLOCK,), jnp.int32).at[slot].set(jnp.arange(n * TOP_K, dtype=jnp.int32) // TOP_K)
    blk_e = jnp.minimum(jnp.sum(pad_end[None, :] <= (jnp.arange(nb) * MOE_BLOCK)[:, None], axis=-1),
                        n_experts - 1).astype(jnp.int32)
    n_used = (pad_end[-1:] // MOE_BLOCK).astype(jnp.int32)
    xb = _gather_rows(h2, src)
    yb = _moe_blocks(blk_e, n_used, xb, wg, wu, wd)
    ya = _gather_rows(yb, jnp.concatenate([slot[0::TOP_K], slot[1::TOP_K]]))
    return _moe_combine(ya, route, y, gate2, per_batch=per_batch, tokens_per_batch=tokens_per_batch)


def _rope_tables(pos):
    half = HEAD_DIM // 2
    inv = ROPE_THETA ** (-jnp.arange(half, dtype=F32) / half)
    ang = pos.astype(F32)[:, None] * inv[None, :]
    cos, sin = jnp.cos(ang), jnp.sin(ang)
    cos_l = jnp.concatenate([cos, cos] * (LANES // HEAD_DIM), axis=1)
    sin_l = jnp.concatenate([-sin, sin] * (LANES // HEAD_DIM), axis=1)
    return cos_l, sin_l


def _cover_matrix(n_rows, n_cmp, n_slc):
    cs = np.arange(n_cmp) * CMP_STRIDE
    ss = np.arange(n_slc) * SLC_BLOCK
    cov = np.clip(np.minimum(cs[:, None] + CMP_BLOCK, ss[None, :] + SLC_BLOCK)
                  - np.maximum(cs[:, None], ss[None, :]), 0, None).astype(np.float32) / CMP_BLOCK
    out = np.zeros((n_rows, LANES), np.float32)
    out[:n_cmp, :n_slc] = cov
    return jnp.asarray(out, BF)


def _expand_matrix(n_keys):
    blk = np.arange(LANES)[:, None]
    key = np.arange(n_keys)[None, :] // SLC_BLOCK
    return jnp.asarray((blk == key).astype(np.float32), BF)


def _block_diag_mean(width):
    idx = np.arange(width) // HEAD_DIM
    return jnp.asarray((idx[:, None] == idx[None, :]).astype(np.float32) / HEAD_DIM, BF)


def _compress_weights(cmp_w_l, cmp_pe_l):
    z = jnp.zeros_like(cmp_w_l)
    wbig = jnp.concatenate([jnp.concatenate([cmp_w_l, z], axis=3), jnp.concatenate([z, cmp_w_l], axis=3)],
                           axis=2).astype(BF)
    pebig = jnp.concatenate([cmp_pe_l, cmp_pe_l], axis=2)
    return wbig, pebig


def kernel(x_prompt, x_sample, cache_cmp, cache_slc, state_win, page_table, c_prompt, c_sample, norm_mix_g, norm_ffn_g, w_ada, b_ada, w_in, w_spatial, b_spatial, q_norm_g, k_norm_g, cmp_pe, cmp_w, out_norm_g, w_out, ffn_w_gate, ffn_w_up, ffn_w_down, router_w, router_b, moe_w_gate, moe_w_up, moe_w_down):
    batch, t, d = x_prompt.shape
    db, tn, _ = x_sample.shape
    depth = w_ada.shape[0]
    n_pool, page = cache_cmp.shape[1], cache_cmp.shape[2]
    n_pages = page_table.shape[1]
    past = n_pages * page
    wb = state_win.shape[2]
    d_a = d // 2
    a_groups = d_a // HEAD_DIM
    assert tn == 1 and past % CHUNK == 0 and t % KEY_TILE == 0 and d_a % LANES == 0
    assert d - d_a == B_KV_HEADS * GQA_R * HEAD_DIM and B_KV_HEADS * HEAD_DIM == LANES
    n_p = batch * t
    kvw = 2 * LANES
    p_main = 2 * d_a + (d - d_a) + N_BRANCH * kvw
    n_gate = w_in.shape[2] - p_main
    qpos_s = past

    m_rows = batch + db
    m_pad = -(-m_rows // 8) * 8
    c_all = jnp.concatenate([c_prompt, c_sample, jnp.zeros((m_pad - m_rows, d), F32)], axis=0)
    mods = _adaln(c_all, w_ada, b_ada)

    bd = _block_diag_mean(d_a)
    cos_p, sin_p = _rope_tables(jnp.arange(t, dtype=jnp.int32))
    cos_s, sin_s = _rope_tables(jnp.full((1,), qpos_s, jnp.int32))
    np_p = t // CMP_STRIDE
    cover_p = _cover_matrix(np_p, np_p - 1, t // SLC_BLOCK)
    expand_p = _expand_matrix(t)
    np_s = past // CMP_STRIDE
    ns_s = -(-(past + tn) // SLC_BLOCK)
    cover_s = _cover_matrix(np_s, np_s - 1, ns_s)
    expand_s = _expand_matrix(past)
    tril = jnp.tril(jnp.ones((CHUNK, CHUNK), dtype=bool))

    cache_cmp4 = cache_cmp.reshape(depth, n_pool, page, kvw)
    cache_slc4 = cache_slc.reshape(depth, n_pool, page, kvw)
    state_win4 = state_win.reshape(depth, db, wb, kvw)

    yp = x_prompt.reshape(n_p, d)
    ys = x_sample.reshape(db, d)
    outs = {k: [] for k in ("cmp_p", "cmp_s", "slc_p", "slc_s", "win_p", "win_s", "chv_s")}
    for l in range(depth):
        mod_p = [mods[l, :batch, k * d:(k + 1) * d].reshape(batch, 1, d) for k in range(6)]
        mod_s = [mods[l, batch:batch + db, k * d:(k + 1) * d] for k in range(6)]
        w_in_bf = jnp.concatenate([w_in[l], jnp.zeros((d, LANES - n_gate), F32)], axis=1).astype(BF)
        qg = jnp.tile(q_norm_g[l], (d - d_a) // HEAD_DIM)[None, :]
        kg = jnp.tile(k_norm_g[l], (1, LANES // HEAD_DIM))
        ws_p = jnp.where(tril, w_spatial[l], 0.0).astype(BF)
        bs_p = jnp.repeat(b_spatial[l].T, HEAD_DIM, axis=1)
        ws_s = jnp.repeat(w_spatial[l][:, 0, 0], HEAD_DIM)[None, :]
        bs_s = jnp.repeat(b_spatial[l][:, 0], HEAD_DIM)[None, :]
        wbig, pebig = _compress_weights(cmp_w[l], cmp_pe[l])
        out_g = out_norm_g[l].reshape(1, d)
        wout_bf = w_out[l].astype(BF)
        ng1 = norm_mix_g[l][None, :]
        ng2 = norm_ffn_g[l][None, :]

        a, q, cmp_r, slc_r, win_r, slc_bf, win_bf, gates = _pre_mix(
            yp, ng1, mod_p[0], mod_p[1], w_in_bf, cos_p, sin_p, bd, qg, kg, ws_p, bs_p,
            chunked=True, tokens_per_batch=t)
        kc, vc = _compress_prompt(cmp_r, wbig, pebig, batch)
        o = _nsa_prompt(q, gates, kc, vc, slc_bf, win_bf, cover_p, expand_p, batch)
        yp, h2p = _post_mix(a, o, yp, mod_p[2], out_g, wout_bf, bd, ng2, mod_p[3], mod_p[4],
                            per_batch=True, tokens_per_batch=t)
        outs["cmp_p"].append(cmp_r.reshape(batch, t, 2, B_KV_HEADS, HEAD_DIM))
        outs["slc_p"].append(slc_r.reshape(batch, t, 2, B_KV_HEADS, HEAD_DIM))
        wbp = min(WINDOW, t)
        outs["win_p"].append(win_r.reshape(batch, t, 2, B_KV_HEADS, HEAD_DIM)[:, t - wbp:])

        a_s, q_s, cmp_s, slc_s, win_s, gates_s, v_s = _pre_mix(
            ys, ng1, mod_s[0], mod_s[1], w_in_bf, cos_s, sin_s, bd, qg, kg, ws_s, bs_s,
            chunked=False, tokens_per_batch=1)
        q3 = q_s.reshape(db, 1, d - d_a)
        ocmp, imp = _s_cmp(page_table, q3, cache_cmp4, l, wbig, pebig, cover_s, qpos_s)
        mask = _s_select(imp[:, 0:B_KV_HEADS, :].reshape(db * B_KV_HEADS, LANES), expand_s, qpos_s)
        o_s = _s_att(page_table, q3, gates_s.reshape(db, 1, LANES), mask.reshape(db, B_KV_HEADS, past), ocmp,
                     slc_s.reshape(db, 1, kvw), win_s.reshape(db, 1, kvw), state_win4, cache_slc4, l,
                     qpos_s, past)
        ys, h2s = _post_mix(a_s, o_s.reshape(db, d - d_a), ys, mod_s[2], out_g, wout_bf, bd, ng2, mod_s[3],
                            mod_s[4], per_batch=False, tokens_per_batch=1)
        outs["cmp_s"].append(cmp_s.reshape(db, tn, 2, B_KV_HEADS, HEAD_DIM))
        outs["slc_s"].append(slc_s.reshape(db, tn, 2, B_KV_HEADS, HEAD_DIM))
        win_all = jnp.concatenate([state_win[l], win_s.reshape(db, tn, 2, B_KV_HEADS, HEAD_DIM)], axis=1)
        outs["win_s"].append(win_all[:, tn:])
        outs["chv_s"].append(v_s.reshape(db, tn, d_a))

        i = l // 2
        if l % 2 == 0:
            wg, wu, wd = ffn_w_gate[i].astype(BF), ffn_w_up[i].astype(BF), ffn_w_down[i].astype(BF)
            yp = _ffn_dense(h2p, yp, mod_p[5], wg, wu, wd, per_batch=True, tokens_per_batch=t)
            ys = _ffn_dense(h2s, ys, mod_s[5], wg, wu, wd, per_batch=False, tokens_per_batch=1)
        else:
            wg, wu, wd = moe_w_gate[i].astype(BF), moe_w_up[i].astype(BF), moe_w_down[i].astype(BF)
            n_exp = router_w.shape[2]
            rw = jnp.concatenate([router_w[i], jnp.zeros((d, LANES - n_exp), F32)], axis=1)
            rw_hi = rw.astype(BF)
            rw_lo = (rw - rw_hi.astype(F32)).astype(BF)
            rb = jnp.concatenate([router_b[i], jnp.zeros((LANES - n_exp,), F32)])[None, :]
            yp = _moe_ffn(h2p, yp, mod_p[5], rw_hi, rw_lo, rb, wg, wu, wd, per_batch=True, tokens_per_batch=t)
            ys = _moe_ffn(h2s, ys, mod_s[5], rw_hi, rw_lo, rb, wg, wu, wd, per_batch=False, tokens_per_batch=1)

    st = lambda k: jnp.stack(outs[k])
    return (yp.reshape(batch, t, d), ys.reshape(db, tn, d), st("cmp_p"), st("cmp_s"), st("slc_p"), st("slc_s"),
            st("win_p"), st("win_s"), st("chv_s"))
```

```python
import functools

import numpy as np
import jax
import jax.numpy as jnp
from jax import lax
from jax.experimental import pallas as pl
from jax.experimental.pallas import tpu as pltpu

F32 = jnp.float32
BF = jnp.bfloat16

HEAD_DIM = 64
B_KV_HEADS = 2
GQA_R = 4
N_BRANCH = 3
CHUNK = 128
CMP_BLOCK = 32
CMP_STRIDE = 16
SLC_BLOCK = 64
N_SELECT = 16
WINDOW = 512
Q_BLOCK = 128
ROPE_THETA = 10000.0
TOP_K = 2
MOE_BLOCK = 128
NEG_INF = -1e30
FORCE_SCORE = 1e30
EPS = 1e-6

LANES = 128
TOKEN_TILE_ROWS = 8
KEY_TILE = 512
VMEM_LIMIT = 56 * 1024 * 1024


def _cparams(sem):
    return pltpu.CompilerParams(dimension_semantics=sem, vmem_limit_bytes=VMEM_LIMIT)


def _dot(a, b):
    return jnp.dot(a, b, preferred_element_type=F32)


def _dot_nt(a, b):
    return lax.dot_general(a, b, (((1,), (1,)), ((), ())), preferred_element_type=F32)


def _split_dot(x, w):
    hi = x.astype(BF)
    lo = (x - hi.astype(F32)).astype(BF)
    return _dot(hi, w) + _dot(lo, w)


def _ada_kernel(c_ref, w_ref, b_ref, o_ref):
    c = c_ref[...]
    s = c * jax.nn.sigmoid(c)
    o_ref[0] = _dot(s.astype(BF), w_ref[0].astype(BF)) + b_ref[0]


def _adaln(c_all, w_ada, b_ada):
    depth, d, n6 = w_ada.shape
    m = c_all.shape[0]
    tn = 1536
    return pl.pallas_call(
        _ada_kernel,
        grid=(depth, n6 // tn),
        in_specs=[pl.BlockSpec((m, d), lambda l, j: (0, 0)),
                  pl.BlockSpec((1, d, tn), lambda l, j: (l, 0, j)),
                  pl.BlockSpec((1, 1, tn), lambda l, j: (l, 0, j))],
        out_specs=pl.BlockSpec((1, m, tn), lambda l, j: (l, 0, j)),
        out_shape=jax.ShapeDtypeStruct((depth, m, n6), F32),
        compiler_params=_cparams(("arbitrary", "arbitrary")),
        name="adaln",
    )(c_all, w_ada, b_ada.reshape(depth, 1, n6))


def _pre_kernel(x_ref, g_ref, sh_ref, sc_ref, w_ref, cos_ref, sin_ref, bd_ref, qg_ref, kg_ref,
                ws_ref, bs_ref, *out_refs, chunked, d_a, d_b):
    if chunked:
        a_ref, q_ref, cmp_ref, slc_ref, win_ref, slcb_ref, winb_ref, gate_ref = out_refs
    else:
        a_ref, q_ref, cmp_ref, slc_ref, win_ref, gate_ref, v_ref = out_refs
    tm = x_ref.shape[0]
    x = x_ref[...]
    ms = jnp.mean(x * x, axis=-1, keepdims=True)
    h = x * lax.rsqrt(ms + EPS) * g_ref[...]
    h = h * (1.0 + sc_ref[...]) + sh_ref[...]
    z = _dot(h.astype(BF), w_ref[...])
    uv = jax.nn.gelu(z[:, :2 * d_a])
    u = uv[:, :d_a]
    v = uv[:, d_a:]
    bd = bd_ref[...]
    mu = _split_dot(v, bd)
    vc = v - mu
    vn = vc * lax.rsqrt(_split_dot(vc * vc, bd) + EPS)

    cos = cos_ref[...]
    sin = sin_ref[...]
    lane = lax.broadcasted_iota(jnp.int32, (1, LANES), 1)
    first_half = (lane % HEAD_DIM) < (HEAD_DIM // 2)

    def rope(t):
        swapped = jnp.where(first_half, pltpu.roll(t, LANES - HEAD_DIM // 2, 1),
                            pltpu.roll(t, HEAD_DIM // 2, 1))
        return t * cos + swapped * sin

    o = 2 * d_a
    zq = z[:, o:o + d_b]
    qn = zq * lax.rsqrt(_split_dot(zq * zq, bd) + EPS) * qg_ref[...]
    for c in range(d_b // LANES):
        q_ref[:, c * LANES:(c + 1) * LANES] = rope(qn[:, c * LANES:(c + 1) * LANES]) * (HEAD_DIM ** -0.5)
    o += d_b
    bd1 = bd_ref[0:LANES, 0:LANES]
    row_refs = (cmp_ref, slc_ref, win_ref)
    bf_refs = (None, slcb_ref, winb_ref) if chunked else (None, None, None)
    for br in range(N_BRANCH):
        kp = z[:, o:o + LANES]
        vp = z[:, o + LANES:o + 2 * LANES]
        kn = kp * lax.rsqrt(_split_dot(kp * kp, bd1) + EPS) * kg_ref[br:br + 1, :]
        kr = rope(kn)
        row_refs[br][:, 0:LANES] = kr
        row_refs[br][:, LANES:2 * LANES] = vp
        if bf_refs[br] is not None:
            bf_refs[br][:, 0:LANES] = kr.astype(BF)
            bf_refs[br][:, LANES:2 * LANES] = vp.astype(BF)
        o += 2 * LANES
    gate_ref[...] = jax.nn.sigmoid(z[:, o:o + LANES])

    if chunked:
        low = lane < HEAD_DIM
        for c in range(tm // CHUNK):
            rs = slice(c * CHUNK, (c + 1) * CHUNK)
            for j in range(d_a // LANES):
                cs = slice(j * LANES, (j + 1) * LANES)
                vpair = vn[rs, cs].astype(BF)
                m0 = _dot(ws_ref[2 * j], vpair)
                m1 = _dot(ws_ref[2 * j + 1], vpair)
                mixed = jnp.where(low, m0, m1) + bs_ref[:, cs]
                a_ref[rs, cs] = u[rs, cs] * mixed
    else:
        a_ref[...] = u * (vn * ws_ref[...] + bs_ref[...])
        v_ref[...] = vn


def _pre_mix(x, norm_g, shift, scale, w_in_bf, cos, sin, bd, qg, kg, ws, bs, *, chunked, tokens_per_batch):
    n, d = x.shape
    d_a = d_b = d // 2
    tm = 256 if chunked else n
    nt = n // tm
    pw = w_in_bf.shape[1]
    full = lambda shp: pl.BlockSpec(shp, lambda i: (0,) * len(shp))
    row = lambda w: pl.BlockSpec((tm, w), lambda i: (i, 0))
    if chunked:
        tpb = tokens_per_batch // tm
        mod_spec = pl.BlockSpec((None, 1, d), lambda i: (i // tpb, 0, 0))
        rot_spec = pl.BlockSpec((tm, LANES), lambda i: (i % tpb, 0))
        ws_spec, bs_spec = full(ws.shape), full(bs.shape)
        out_shape = [jax.ShapeDtypeStruct((n, d_a), F32), jax.ShapeDtypeStruct((n, d_b), F32),
                     jax.ShapeDtypeStruct((n, 2 * LANES), F32), jax.ShapeDtypeStruct((n, 2 * LANES), F32),
                     jax.ShapeDtypeStruct((n, 2 * LANES), F32), jax.ShapeDtypeStruct((n, 2 * LANES), BF),
                     jax.ShapeDtypeStruct((n, 2 * LANES), BF), jax.ShapeDtypeStruct((n, LANES), F32)]
        out_specs = [row(d_a), row(d_b), row(2 * LANES), row(2 * LANES), row(2 * LANES), row(2 * LANES),
                     row(2 * LANES), row(LANES)]
    else:
        mod_spec = row(d)
        rot_spec = full((1, LANES))
        ws_spec, bs_spec = full(ws.shape), full(bs.shape)
        out_shape = [jax.ShapeDtypeStruct((n, d_a), F32), jax.ShapeDtypeStruct((n, d_b), F32),
                     jax.ShapeDtypeStruct((n, 2 * LANES), F32), jax.ShapeDtypeStruct((n, 2 * LANES), F32),
                     jax.ShapeDtypeStruct((n, 2 * LANES), F32), jax.ShapeDtypeStruct((n, LANES), F32),
                     jax.ShapeDtypeStruct((n, d_a), F32)]
        out_specs = [row(d_a), row(d_b), row(2 * LANES), row(2 * LANES), row(2 * LANES), row(LANES), row(d_a)]
    return pl.pallas_call(
        functools.partial(_pre_kernel, chunked=chunked, d_a=d_a, d_b=d_b),
        grid=(nt,),
        in_specs=[row(d), full((1, d)), mod_spec, mod_spec, full((d, pw)), rot_spec, rot_spec,
                  full(bd.shape), full(qg.shape), full(kg.shape), ws_spec, bs_spec],
        out_specs=out_specs,
        out_shape=out_shape,
        compiler_params=_cparams(("arbitrary",)),
        name="pre_mix_prompt" if chunked else "pre_mix_sample",
    )(x, norm_g, shift, scale, w_in_bf, cos, sin, bd, qg, kg, ws, bs)


def _compress_pieces(load_piece_rows, w_ref, pe_ref, kv, n_piece):
    assert CMP_BLOCK == 2 * CMP_STRIDE
    acc_a = jnp.zeros((n_piece, LANES), F32)
    acc_b = jnp.zeros((n_piece, LANES), F32)
    for l in range(CMP_STRIDE):
        xl = load_piece_rows(l)
        l2 = CMP_STRIDE + l
        acc_a = acc_a + _dot((xl + pe_ref[kv, l:l + 1, :]).astype(BF), w_ref[kv, l])
        acc_b = acc_b + _dot((xl + pe_ref[kv, l2:l2 + 1, :]).astype(BF), w_ref[kv, l2])
    shifted = pltpu.roll(acc_b, n_piece - 1, 0)
    rows = lax.broadcasted_iota(jnp.int32, (n_piece, 1), 0)
    return jnp.where(rows < n_piece - 1, acc_a + shifted, 0.0)


def _compress_kernel(xk_ref, xv_ref, w_ref, pe_ref, kc_ref, vc_ref):
    n_piece = kc_ref.shape[0]
    for kv, (x_ref, o_ref) in enumerate(((xk_ref, kc_ref), (xv_ref, vc_ref))):
        out = _compress_pieces(lambda l: x_ref[pl.ds(l, n_piece, stride=CMP_STRIDE), :], w_ref, pe_ref, kv, n_piece)
        o_ref[...] = out.astype(BF)


def _compress_prompt(cmp_rows, wbig, pebig, batch):
    n = cmp_rows.shape[0]
    t = n // batch
    n_piece = t // CMP_STRIDE
    out = jax.ShapeDtypeStruct((batch * n_piece, LANES), BF)
    return pl.pallas_call(
        _compress_kernel,
        grid=(batch,),
        in_specs=[pl.BlockSpec((t, LANES), lambda b: (b, 0)),
                  pl.BlockSpec((t, LANES), lambda b: (b, 1)),
                  pl.BlockSpec(wbig.shape, lambda b: (0, 0, 0, 0)),
                  pl.BlockSpec(pebig.shape, lambda b: (0, 0, 0))],
        out_specs=[pl.BlockSpec((n_piece, LANES), lambda b: (b, 0))] * 2,
        out_shape=[out, out],
        compiler_params=_cparams(("arbitrary",)),
        name="compress_prompt",
    )(cmp_rows, cmp_rows, wbig, pebig)


def _group_queries(q, g):
    m = q.shape[0]
    low = lax.broadcasted_iota(jnp.int32, (m, LANES), 1) < HEAD_DIM
    keep = low if g == 0 else jnp.logical_not(low)
    parts = []
    for r in range(GQA_R):
        h = GQA_R * g + r
        t = q[:, (h // 2) * LANES:(h // 2 + 1) * LANES]
        if h % 2 != g:
            t = pltpu.roll(t, HEAD_DIM, 1)
        parts.append(jnp.where(keep, t, 0.0))
    return parts


def _top_select(imp, blk_f):
    sel = jnp.zeros(imp.shape, F32)
    work = imp
    for _ in range(N_SELECT):
        mx = jnp.max(work, axis=-1, keepdims=True)
        first = jnp.min(jnp.where(work == mx, blk_f, float(LANES)), axis=-1, keepdims=True)
        hit = blk_f == first
        sel = jnp.where(hit, 1.0, sel)
        work = jnp.where(hit, -jnp.inf, work)
    return sel


def _block_scores(imp, qpos):
    blk = lax.broadcasted_iota(jnp.int32, imp.shape, 1)
    cur = qpos // SLC_BLOCK
    forced = (blk == 0) | (blk == cur) | (blk == cur - 1)
    valid = blk <= cur
    imp = jnp.where(forced, FORCE_SCORE, imp)
    imp = jnp.where(valid, imp, NEG_INF)
    return imp, blk.astype(F32), valid


def _merge_heads(res, m):
    low = lax.broadcasted_iota(jnp.int32, (m, LANES), 1) < HEAD_DIM
    cols = []
    for c in range(4):
        a, b = res[2 * c], res[2 * c + 1]
        if c // 2 == 0:
            cols.append(jnp.where(low, a, pltpu.roll(b, HEAD_DIM, 1)))
        else:
            cols.append(jnp.where(low, pltpu.roll(a, HEAD_DIM, 1), b))
    return jnp.concatenate(cols, axis=1)


def _nsa_prompt_kernel(q_ref, gate_ref, kc_ref, vc_ref, slc_ref, win_ref, cover_ref, e_ref, o_ref):
    i = pl.program_id(1)
    s0 = i * Q_BLOCK
    n_cmp = kc_ref.shape[0]
    q = q_ref[...]
    gates = gate_ref[...]
    tpos = s0 + lax.broadcasted_iota(jnp.int32, (Q_BLOCK, 1), 0)
    tpos4 = jnp.concatenate([tpos] * GQA_R, axis=0)
    kc = kc_ref[...]
    vc = vc_ref[...]
    c_last = lax.broadcasted_iota(jnp.int32, (1, n_cmp), 1) * CMP_STRIDE + (CMP_BLOCK - 1)
    cv = c_last <= tpos4
    n_tiles = (s0 + Q_BLOCK + KEY_TILE - 1) // KEY_TILE
    w_start = pl.multiple_of(jnp.maximum(s0 - WINDOW, 0), Q_BLOCK)
    w_len = WINDOW + Q_BLOCK
    wpos = w_start + lax.broadcasted_iota(jnp.int32, (1, w_len), 1)
    w_bias = jnp.where((wpos <= tpos) & (wpos > tpos - WINDOW), 0.0, NEG_INF)
    w_bias4 = jnp.concatenate([w_bias] * GQA_R, axis=0)
    res = []
    for g in range(B_KV_HEADS):
        qg = jnp.concatenate(_group_queries(q, g), axis=0).astype(BF)
        s = jnp.where(cv, _dot_nt(qg, kc), NEG_INF)
        e = jnp.where(cv, jnp.exp(s - jnp.max(s, axis=-1, keepdims=True)), 0.0)
        p = e / jnp.maximum(jnp.sum(e, axis=-1, keepdims=True), 1e-30)
        o_cmp = _dot(p.astype(BF), vc)
        p_sum = p[0:Q_BLOCK]
        for r in range(1, GQA_R):
            p_sum = p_sum + p[r * Q_BLOCK:(r + 1) * Q_BLOCK]
        imp, blk_f, valid = _block_scores(_split_dot(p_sum, cover_ref[...]), tpos)
        sel = _top_select(imp, blk_f)
        sel_b = jnp.where(valid, sel, 0.0).astype(BF)

        def body(kt, carry):
            m, l, acc = carry
            k0 = pl.multiple_of(kt * KEY_TILE, KEY_TILE)
            kk = slc_ref[pl.ds(k0, KEY_TILE), 0:LANES]
            vv = slc_ref[pl.ds(k0, KEY_TILE), LANES:2 * LANES]
            member = _dot(sel_b, e_ref[:, pl.ds(k0, KEY_TILE)])
            kpos = k0 + lax.broadcasted_iota(jnp.int32, (1, KEY_TILE), 1)
            bias = jnp.where((member > 0.5) & (kpos <= tpos), 0.0, NEG_INF)
            sc = _dot_nt(qg, kk) + jnp.concatenate([bias] * GQA_R, axis=0)
            m_new = jnp.maximum(m, jnp.max(sc, axis=-1, keepdims=True))
            alpha = jnp.exp(m - m_new)
            pe = jnp.exp(sc - m_new)
            l = alpha * l + jnp.sum(pe, axis=-1, keepdims=True)
            acc = alpha * acc + _dot(pe.astype(BF), vv)
            return m_new, l, acc

        m0 = jnp.full((GQA_R * Q_BLOCK, 1), NEG_INF, F32)
        l0 = jnp.zeros((GQA_R * Q_BLOCK, 1), F32)
        a0 = jnp.zeros((GQA_R * Q_BLOCK, LANES), F32)
        _, l_s, acc_s = lax.fori_loop(0, n_tiles, body, (m0, l0, a0))
        o_slc = acc_s / l_s
        kw = win_ref[pl.ds(w_start, w_len), 0:LANES]
        vw = win_ref[pl.ds(w_start, w_len), LANES:2 * LANES]
        sw = _dot_nt(qg, kw) + w_bias4
        ew = jnp.exp(sw - jnp.max(sw, axis=-1, keepdims=True))
        o_win = _dot(ew.astype(BF), vw) / jnp.sum(ew, axis=-1, keepdims=True)
        for r in range(GQA_R):
            h = GQA_R * g + r
            rs = slice(r * Q_BLOCK, (r + 1) * Q_BLOCK)
            res.append(gates[:, 3 * h:3 * h + 1] * o_cmp[rs] + gates[:, 3 * h + 1:3 * h + 2] * o_slc[rs]
                       + gates[:, 3 * h + 2:3 * h + 3] * o_win[rs])
    o_ref[...] = _merge_heads(res, Q_BLOCK)


def _nsa_prompt(q, gates, kc, vc, slc_bf, win_bf, cover, expand, batch):
    n, d_b = q.shape
    t = n // batch
    nq = t // Q_BLOCK
    n_cmp = kc.shape[0] // batch
    return pl.pallas_call(
        _nsa_prompt_kernel,
        grid=(batch, nq),
        in_specs=[pl.BlockSpec((Q_BLOCK, d_b), lambda b, i: (b * nq + i, 0)),
                  pl.BlockSpec((Q_BLOCK, LANES), lambda b, i: (b * nq + i, 0)),
                  pl.BlockSpec((n_cmp, LANES), lambda b, i: (b, 0)),
                  pl.BlockSpec((n_cmp, LANES), lambda b, i: (b, 0)),
                  pl.BlockSpec((t, 2 * LANES), lambda b, i: (b, 0)),
                  pl.BlockSpec((t, 2 * LANES), lambda b, i: (b, 0)),
                  pl.BlockSpec(cover.shape, lambda b, i: (0, 0)),
                  pl.BlockSpec(expand.shape, lambda b, i: (0, 0))],
        out_specs=pl.BlockSpec((Q_BLOCK, d_b), lambda b, i: (b * nq + i, 0)),
        out_shape=jax.ShapeDtypeStruct((n, d_b), F32),
        compiler_params=_cparams(("arbitrary", "arbitrary")),
        name="nsa_prompt",
    )(q, gates, kc, vc, slc_bf, win_bf, cover, expand)


def _sample_queries(q_row):
    sub = lax.broadcasted_iota(jnp.int32, (8, LANES), 0)
    out = jnp.zeros((8, LANES), F32)
    for g in range(B_KV_HEADS):
        for r, part in enumerate(_group_queries(q_row, g)):
            out = jnp.where(sub == GQA_R * g + r, jnp.broadcast_to(part, (8, LANES)), out)
    return out


def _s_cmp_kernel(pt_ref, q_ref, *refs, n_pages, qpos):
    k_pages = refs[:n_pages]
    v_pages = refs[n_pages:2 * n_pages]
    wbig_ref, pe_ref, cover_ref, ocmp_ref, imp_ref = refs[2 * n_pages:]
    per_page = k_pages[0].shape[0] // CMP_STRIDE
    n_piece = n_pages * per_page

    def loader(pages):
        return lambda l: jnp.concatenate([pg[pl.ds(l, per_page, stride=CMP_STRIDE), :] for pg in pages], axis=0)

    kc = _compress_pieces(loader(k_pages), wbig_ref, pe_ref, 0, n_piece).astype(BF)
    vc = _compress_pieces(loader(v_pages), wbig_ref, pe_ref, 1, n_piece).astype(BF)
    qa = _sample_queries(q_ref[...])
    c_last = lax.broadcasted_iota(jnp.int32, (1, n_piece), 1) * CMP_STRIDE + (CMP_BLOCK - 1)
    cv = c_last <= qpos
    s = jnp.where(cv, _dot_nt(qa.astype(BF), kc), NEG_INF)
    e = jnp.where(cv, jnp.exp(s - jnp.max(s, axis=-1, keepdims=True)), 0.0)
    p = e / jnp.maximum(jnp.sum(e, axis=-1, keepdims=True), 1e-30)
    ocmp_ref[...] = _dot(p.astype(BF), vc)
    sub = lax.broadcasted_iota(jnp.int32, (8, n_piece), 0)
    ps0 = jnp.sum(jnp.where(sub < GQA_R, p, 0.0), axis=0, keepdims=True)
    ps1 = jnp.sum(jnp.where(sub >= GQA_R, p, 0.0), axis=0, keepdims=True)
    p_sum = jnp.where(sub == 0, ps0, jnp.where(sub == 1, ps1, 0.0))
    imp_ref[...] = _split_dot(p_sum, cover_ref[...])


def _s_cmp(page_table, q3, cache_l, layer, wbig, pebig, cover, qpos):
    db = q3.shape[0]
    n_pages = page_table.shape[1]
    page = cache_l.shape[2]
    n_piece = n_pages * page // CMP_STRIDE
    page_specs = [pl.BlockSpec((None, None, page, LANES),
                               functools.partial(lambda b, pt, j, kv: (layer, pt[b, j], 0, kv), j=j, kv=kv))
                  for kv in range(2) for j in range(n_pages)]
    gs = pltpu.PrefetchScalarGridSpec(
        num_scalar_prefetch=1,
        grid=(db,),
        in_specs=[pl.BlockSpec((None, 1, q3.shape[2]), lambda b, pt: (b, 0, 0))] + page_specs + [
            pl.BlockSpec(wbig.shape, lambda b, pt: (0, 0, 0, 0)),
            pl.BlockSpec(pebig.shape, lambda b, pt: (0, 0, 0)),
            pl.BlockSpec(cover.shape, lambda b, pt: (0, 0))],
        out_specs=[pl.BlockSpec((None, 8, LANES), lambda b, pt: (b, 0, 0)),
                   pl.BlockSpec((None, 8, LANES), lambda b, pt: (b, 0, 0))],
    )
    assert cover.shape[0] == n_piece
    return pl.pallas_call(
        functools.partial(_s_cmp_kernel, n_pages=n_pages, qpos=qpos),
        grid_spec=gs,
        out_shape=[jax.ShapeDtypeStruct((db, 8, LANES), F32), jax.ShapeDtypeStruct((db, 8, LANES), F32)],
        compiler_params=_cparams(("arbitrary",)),
        name="nsa_sample_cmp",
    )(page_table, q3, *([cache_l] * (2 * n_pages)), wbig, pebig, cover)


def _s_select_kernel(imp_ref, e_ref, o_ref, *, qpos):
    imp = imp_ref[...]
    pos = jnp.full((imp.shape[0], 1), qpos, jnp.int32)
    imp, blk_f, valid = _block_scores(imp, pos)
    sel = _top_select(imp, blk_f)
    sel_b = jnp.where(valid, sel, 0.0).astype(BF)
    o_ref[...] = _dot(sel_b, e_ref[...])


def _s_select(imp2, expand, qpos):
    m = imp2.shape[0]
    past = expand.shape[1]
    return pl.pallas_call(
        functools.partial(_s_select_kernel, qpos=qpos),
        grid=(1,),
        in_specs=[pl.BlockSpec(imp2.shape, lambda i: (0, 0)), pl.BlockSpec(expand.shape, lambda i: (0, 0))],
        out_specs=pl.BlockSpec((m, past), lambda i: (0, 0)),
        out_shape=jax.ShapeDtypeStruct((m, past), F32),
        compiler_params=_cparams(("arbitrary",)),
        name="nsa_sample_select",
    )(imp2, expand)


def _s_att_kernel(pt_ref, q_ref, gate_ref, mask_ref, ocmp_ref, slcn_ref, winn_ref, win_ref, *refs,
                  n_pages, qpos, past):
    pages = refs[:n_pages]
    o_ref = refs[n_pages]
    page = pages[0].shape[0]
    qa = _sample_queries(q_ref[...])
    qb = qa.astype(BF)
    sub = lax.broadcasted_iota(jnp.int32, (8, 1), 0)
    mrow = jnp.where(sub < GQA_R, mask_ref[0:1, :], mask_ref[1:2, :])
    bias = jnp.where(mrow > 0.5, 0.0, NEG_INF)
    sc = jnp.concatenate([_dot_nt(qb, pg[:, 0:LANES].astype(BF)) for pg in pages], axis=1) + bias
    kn = slcn_ref[:, 0:LANES]
    vn = slcn_ref[:, LANES:2 * LANES]
    s_new = jnp.sum(qa * kn, axis=-1, keepdims=True)
    m = jnp.maximum(jnp.max(sc, axis=-1, keepdims=True), s_new)
    e = jnp.exp(sc - m)
    e_new = jnp.exp(s_new - m)
    acc = e_new * vn
    for j, pg in enumerate(pages):
        acc = acc + _dot(e[:, j * page:(j + 1) * page].astype(BF), pg[:, LANES:2 * LANES].astype(BF))
    o_slc = acc / (jnp.sum(e, axis=-1, keepdims=True) + e_new)
    wb = win_ref.shape[0]
    wpos = (past - wb) + lax.broadcasted_iota(jnp.int32, (1, wb), 1)
    wvalid = (wpos <= qpos) & (wpos > qpos - WINDOW) & (wpos >= 0)
    sw = _dot_nt(qb, win_ref[:, 0:LANES].astype(BF)) + jnp.where(wvalid, 0.0, NEG_INF)
    kwn = winn_ref[:, 0:LANES]
    vwn = winn_ref[:, LANES:2 * LANES]
    sw_new = jnp.sum(qa * kwn, axis=-1, keepdims=True)
    mw = jnp.maximum(jnp.max(sw, axis=-1, keepdims=True), sw_new)
    ew = jnp.exp(sw - mw)
    ew_new = jnp.exp(sw_new - mw)
    o_win = (_dot(ew.astype(BF), win_ref[:, LANES:2 * LANES].astype(BF)) + ew_new * vwn) / (
        jnp.sum(ew, axis=-1, keepdims=True) + ew_new)
    lane = lax.broadcasted_iota(jnp.int32, (8, LANES), 1)
    gt = jnp.broadcast_to(gate_ref[...], (8, LANES))
    gsel = [jnp.sum(jnp.where(lane == 3 * sub + br, gt, 0.0), axis=-1, keepdims=True) for br in range(N_BRANCH)]
    res = gsel[0] * ocmp_ref[...] + gsel[1] * o_slc + gsel[2] * o_win
    o_ref[...] = _merge_heads([res[h:h + 1, :] for h in range(8)], 1)


def _s_att(page_table, q3, gates3, mask3, ocmp, slc_new3, win_new3, win_l, cache_l, layer, qpos, past):
    db = q3.shape[0]
    n_pages = page_table.shape[1]
    page = cache_l.shape[2]
    wb = win_l.shape[2]
    page_specs = [pl.BlockSpec((None, None, page, 2 * LANES),
                               functools.partial(lambda b, pt, j: (layer, pt[b, j], 0, 0), j=j))
                  for j in range(n_pages)]
    per_b = lambda r, w: pl.BlockSpec((None, r, w), lambda b, pt: (b, 0, 0))
    gs = pltpu.PrefetchScalarGridSpec(
        num_scalar_prefetch=1,
        grid=(db,),
        in_specs=[per_b(1, q3.shape[2]), per_b(1, LANES), per_b(2, past), per_b(8, LANES),
                  per_b(1, 2 * LANES), per_b(1, 2 * LANES),
                  pl.BlockSpec((None, None, wb, 2 * LANES), lambda b, pt: (layer, b, 0, 0))] + page_specs,
        out_specs=per_b(1, q3.shape[2]),
    )
    return pl.pallas_call(
        functools.partial(_s_att_kernel, n_pages=n_pages, qpos=qpos, past=past),
        grid_spec=gs,
        out_shape=jax.ShapeDtypeStruct(q3.shape, F32),
        compiler_params=_cparams(("arbitrary",)),
        name="nsa_sample_att",
    )(page_table, q3, gates3, mask3, ocmp, slc_new3, win_new3, win_l, *([cache_l] * n_pages))


def _untile(ref, tm):
    return jnp.concatenate([ref[pl.ds(c, tm, stride=TOKEN_TILE_ROWS), :] for c in range(TOKEN_TILE_ROWS)], axis=1)


def _store_tiled(ref, val):
    tm = val.shape[0]
    for c in range(TOKEN_TILE_ROWS):
        ref[pl.ds(c, tm, stride=TOKEN_TILE_ROWS), :] = val[:, c * LANES:(c + 1) * LANES]


def _post_kernel(a_ref, o_ref, x_ref, g1_ref, og_ref, wout_ref, bd_ref, ng_ref, sh2_ref, sc2_ref, y_ref, h2_ref,
                 *, tiled_h2):
    bd = bd_ref[...]
    d_a = a_ref.shape[1]

    def head_norm(t, g):
        return t * lax.rsqrt(_split_dot(t * t, bd) + EPS) * g

    a = head_norm(a_ref[...], og_ref[:, :d_a])
    o = head_norm(o_ref[...], og_ref[:, d_a:])
    mix = _dot(a.astype(BF), wout_ref[:d_a, :]) + _dot(o.astype(BF), wout_ref[d_a:, :])
    y = x_ref[...] + g1_ref[...] * mix
    y_ref[...] = y
    ms = jnp.mean(y * y, axis=-1, keepdims=True)
    h2 = y * lax.rsqrt(ms + EPS) * ng_ref[...]
    h2 = h2 * (1.0 + sc2_ref[...]) + sh2_ref[...]
    if tiled_h2:
        _store_tiled(h2_ref, h2)
    else:
        h2_ref[...] = h2


def _mod_spec(per_batch, tm, d, tokens_per_batch):
    if per_batch:
        tpb = tokens_per_batch // tm
        return pl.BlockSpec((None, 1, d), lambda i: (i // tpb, 0, 0))
    return pl.BlockSpec((tm, d), lambda i: (i, 0))


def _post_mix(a, o, x, gate1, out_g, wout_bf, bd, ffn_g, shift2, scale2, *, per_batch, tokens_per_batch, tiled_h2):
    n, d = x.shape
    tm = 512 if per_batch else n
    full = lambda shp: pl.BlockSpec(shp, lambda i: (0,) * len(shp))
    row = lambda w: pl.BlockSpec((tm, w), lambda i: (i, 0))
    ms = _mod_spec(per_batch, tm, d, tokens_per_batch)
    if tiled_h2:
        assert d == TOKEN_TILE_ROWS * LANES
        h2_spec = pl.BlockSpec((tm * TOKEN_TILE_ROWS, LANES), lambda i: (i, 0))
        h2_shape = jax.ShapeDtypeStruct((n * TOKEN_TILE_ROWS, LANES), F32)
    else:
        h2_spec, h2_shape = row(d), jax.ShapeDtypeStruct((n, d), F32)
    return pl.pallas_call(
        functools.partial(_post_kernel, tiled_h2=tiled_h2),
        grid=(n // tm,),
        in_specs=[row(a.shape[1]), row(o.shape[1]), row(d), ms, full((1, d)), full(wout_bf.shape), full(bd.shape),
                  full((1, d)), ms, ms],
        out_specs=[row(d), h2_spec],
        out_shape=[jax.ShapeDtypeStruct((n, d), F32), h2_shape],
        compiler_params=_cparams(("arbitrary",)),
        name="post_mix",
    )(a, o, x, gate1, out_g, wout_bf, bd, ffn_g, shift2, scale2)


def _ffn_kernel(h_ref, y_ref, g2_ref, wg_ref, wu_ref, wd_ref, o_ref, *, f_tile):
    hb = h_ref[...].astype(BF)
    d_ff = wg_ref.shape[1]
    acc = jnp.zeros(o_ref.shape, F32)
    for f in range(d_ff // f_tile):
        fs = slice(f * f_tile, (f + 1) * f_tile)
        gp = _dot(hb, wg_ref[:, fs])
        up = _dot(hb, wu_ref[:, fs])
        act = gp * jax.nn.sigmoid(gp) * up
        acc = acc + _dot(act.astype(BF), wd_ref[fs, :])
    o_ref[...] = y_ref[...] + g2_ref[...] * acc


def _ff_tile(d_ff):
    for cand in (1408, 1024, 512, 256, 128):
        if d_ff % cand == 0:
            return cand
    return d_ff


def _ffn_dense(h2, y, gate2, wg, wu, wd, *, per_batch, tokens_per_batch):
    n, d = y.shape
    tm = 512 if per_batch else n
    full = lambda shp: pl.BlockSpec(shp, lambda i: (0,) * len(shp))
    row = lambda w: pl.BlockSpec((tm, w), lambda i: (i, 0))
    return pl.pallas_call(
        functools.partial(_ffn_kernel, f_tile=_ff_tile(wg.shape[1])),
        grid=(n // tm,),
        in_specs=[row(d), row(d), _mod_spec(per_batch, tm, d, tokens_per_batch), full(wg.shape), full(wu.shape),
                  full(wd.shape)],
        out_specs=row(d),
        out_shape=jax.ShapeDtypeStruct((n, d), F32),
        compiler_params=_cparams(("arbitrary",)),
        name="ffn_dense",
    )(h2, y, gate2, wg, wu, wd)


def _router_kernel(h_ref, rwh_ref, rwl_ref, rb_ref, o_ref, *, n_experts):
    h = _untile(h_ref, o_ref.shape[0])
    hi = h.astype(BF)
    lo = (h - hi.astype(F32)).astype(BF)
    logits = _dot(hi, rwh_ref[...]) + _dot(lo, rwh_ref[...]) + _dot(hi, rwl_ref[...]) + rb_ref[...]
    lane = lax.broadcasted_iota(jnp.int32, logits.shape, 1)
    lane_f = lane.astype(F32)
    l1 = jnp.where(lane < n_experts, logits, -jnp.inf)
    m1 = jnp.max(l1, axis=-1, keepdims=True)
    i1 = jnp.min(jnp.where(l1 == m1, lane_f, float(LANES)), axis=-1, keepdims=True)
    l2 = jnp.where(lane_f == i1, -jnp.inf, l1)
    m2 = jnp.max(l2, axis=-1, keepdims=True)
    i2 = jnp.min(jnp.where(l2 == m2, lane_f, float(LANES)), axis=-1, keepdims=True)
    e2 = jnp.exp(m2 - m1)
    den = 1.0 + e2
    o_ref[...] = jnp.where(lane == 0, i1, jnp.where(lane == 1, i2, jnp.where(lane == 2, 1.0 / den,
                           jnp.where(lane == 3, e2 / den, 0.0))))


def _router(h2t, rw_hi, rw_lo, rb, n_experts):
    n = h2t.shape[0] // TOKEN_TILE_ROWS
    tm = 512 if n % 512 == 0 else n
    full = lambda shp: pl.BlockSpec(shp, lambda i: (0,) * len(shp))
    return pl.pallas_call(
        functools.partial(_router_kernel, n_experts=n_experts),
        grid=(n // tm,),
        in_specs=[pl.BlockSpec((tm * TOKEN_TILE_ROWS, LANES), lambda i: (i, 0)), full(rw_hi.shape),
                  full(rw_lo.shape), full(rb.shape)],
        out_specs=pl.BlockSpec((tm, LANES), lambda i: (i, 0)),
        out_shape=jax.ShapeDtypeStruct((n, LANES), F32),
        compiler_params=_cparams(("arbitrary",)),
        name="moe_router",
    )(h2t, rw_hi, rw_lo, rb)


def _gather_kernel(idx_ref, src_ref, out_ref, sem, *, rows_per_step):
    base = pl.program_id(0) * rows_per_step

    def row_copy(j):
        return pltpu.make_async_copy(src_ref.at[idx_ref[base + j]], out_ref.at[j], sem)

    def start(j, c):
        row_copy(j).start()
        return c

    def wait(j, c):
        row_copy(j).wait()
        return c

    lax.fori_loop(0, rows_per_step, start, 0)
    lax.fori_loop(0, rows_per_step, wait, 0)


def _gather_rows(src_t, idx, rows_per_step=512):
    m = idx.shape[0]
    while m % rows_per_step:
        rows_per_step //= 2
    src3 = src_t.reshape(-1, TOKEN_TILE_ROWS, LANES)
    gs = pltpu.PrefetchScalarGridSpec(
        num_scalar_prefetch=1,
        grid=(m // rows_per_step,),
        in_specs=[pl.BlockSpec(memory_space=pl.ANY)],
        out_specs=pl.BlockSpec((rows_per_step, TOKEN_TILE_ROWS, LANES), lambda i, idx_ref: (i, 0, 0)),
        scratch_shapes=[pltpu.SemaphoreType.DMA(())],
    )
    out = pl.pallas_call(
        functools.partial(_gather_kernel, rows_per_step=rows_per_step),
        grid_spec=gs,
        out_shape=jax.ShapeDtypeStruct((m, TOKEN_TILE_ROWS, LANES), src_t.dtype),
        compiler_params=pltpu.CompilerParams(dimension_semantics=("arbitrary",), vmem_limit_bytes=VMEM_LIMIT),
        name="gather_rows",
    )(idx, src3)
    return out.reshape(m * TOKEN_TILE_ROWS, LANES)


def _moe_block_kernel(be_ref, nu_ref, x_ref, wg_ref, wu_ref, wd_ref, o_ref, *, f_tile):
    i = pl.program_id(0)

    @pl.when(i < nu_ref[0])
    def _():
        hb = _untile(x_ref, MOE_BLOCK).astype(BF)
        d_ff = wg_ref.shape[1]
        acc = jnp.zeros((MOE_BLOCK, wd_ref.shape[1]), F32)
        for f in range(d_ff // f_tile):
            fs = slice(f * f_tile, (f + 1) * f_tile)
            gp = _dot(hb, wg_ref[:, fs])
            up = _dot(hb, wu_ref[:, fs])
            act = gp * jax.nn.sigmoid(gp) * up
            acc = acc + _dot(act.astype(BF), wd_ref[fs, :])
        _store_tiled(o_ref, acc)

    @pl.when(i >= nu_ref[0])
    def _():
        o_ref[...] = jnp.zeros(o_ref.shape, F32)


def _moe_blocks(blk_e, n_used, xb_t, wg, wu, wd):
    nb = xb_t.shape[0] // (MOE_BLOCK * TOKEN_TILE_ROWS)
    d, d_ff = wg.shape[1], wg.shape[2]
    blk = pl.BlockSpec((MOE_BLOCK * TOKEN_TILE_ROWS, LANES), lambda i, be, nu: (i, 0))
    gs = pltpu.PrefetchScalarGridSpec(
        num_scalar_prefetch=2,
        grid=(nb,),
        in_specs=[blk,
                  pl.BlockSpec((None, d, d_ff), lambda i, be, nu: (be[i], 0, 0)),
                  pl.BlockSpec((None, d, d_ff), lambda i, be, nu: (be[i], 0, 0)),
                  pl.BlockSpec((None, d_ff, d), lambda i, be, nu: (be[i], 0, 0))],
        out_specs=blk,
    )
    return pl.pallas_call(
        functools.partial(_moe_block_kernel, f_tile=_ff_tile(d_ff)),
        grid_spec=gs,
        out_shape=jax.ShapeDtypeStruct(xb_t.shape, F32),
        compiler_params=_cparams(("arbitrary",)),
        name="moe_blocks",
    )(blk_e, n_used, xb_t, wg, wu, wd)


def _combine_kernel(ya0_ref, ya1_ref, r_ref, y_ref, g2_ref, o_ref):
    tm = o_ref.shape[0]
    w1 = r_ref[:, 2:3]
    w2 = r_ref[:, 3:4]
    o_ref[...] = y_ref[...] + g2_ref[...] * (w1 * _untile(ya0_ref, tm) + w2 * _untile(ya1_ref, tm))


def _moe_combine(ya_t, route, y, gate2, *, per_batch, tokens_per_batch):
    n, d = y.shape
    tm = 512 if per_batch else n
    nt = n // tm
    row = lambda w: pl.BlockSpec((tm, w), lambda i: (i, 0))
    return pl.pallas_call(
        _combine_kernel,
        grid=(nt,),
        in_specs=[pl.BlockSpec((tm * TOKEN_TILE_ROWS, LANES), lambda i: (i, 0)),
                  pl.BlockSpec((tm * TOKEN_TILE_ROWS, LANES), lambda i: (i + nt, 0)), row(LANES), row(d),
                  _mod_spec(per_batch, tm, d, tokens_per_batch)],
        out_specs=row(d),
        out_shape=jax.ShapeDtypeStruct((n, d), F32),
        compiler_params=_cparams(("arbitrary",)),
        name="moe_combine",
    )(ya_t, ya_t, route, y, gate2)


def _moe_ffn(h2t, y, gate2, rw_hi, rw_lo, rb, wg, wu, wd, *, per_batch, tokens_per_batch):
    n, d = y.shape
    n_experts = wg.shape[0]
    route = _router(h2t, rw_hi, rw_lo, rb, n_experts)
    flat_e = route[:, 0:TOP_K].astype(jnp.int32).reshape(-1)
    onehot = (flat_e[:, None] == jnp.arange(n_experts, dtype=jnp.int32)[None, :]).astype(jnp.int32)
    csum = jnp.cumsum(onehot, axis=0)
    rank = jnp.sum(onehot * csum, axis=1) - 1
    counts = csum[-1]
    padded = (counts + MOE_BLOCK - 1) // MOE_BLOCK * MOE_BLOCK
    pad_end = jnp.cumsum(padded)
    pad_start = pad_end - padded
    slot = (pad_start[flat_e] + rank).astype(jnp.int32)
    nb = -(-(n * TOP_K) // MOE_BLOCK) + n_experts
    src = jnp.zeros((nb * MOE_BLOCK,), jnp.int32).at[slot].set(jnp.arange(n * TOP_K, dtype=jnp.int32) // TOP_K)
    blk_e = jnp.minimum(jnp.sum(pad_end[None, :] <= (jnp.arange(nb) * MOE_BLOCK)[:, None], axis=-1),
                        n_experts - 1).astype(jnp.int32)
    n_used = (pad_end[-1:] // MOE_BLOCK).astype(jnp.int32)
    xb = _gather_rows(h2t, src)
    yb = _moe_blocks(blk_e, n_used, xb, wg, wu, wd)
    ya = _gather_rows(yb, jnp.concatenate([slot[0::TOP_K], slot[1::TOP_K]]))
    return _moe_combine(ya, route, y, gate2, per_batch=per_batch, tokens_per_batch=tokens_per_batch)


def _rope_tables(pos):
    half = HEAD_DIM // 2
    inv = ROPE_THETA ** (-jnp.arange(half, dtype=F32) / half)
    ang = pos.astype(F32)[:, None] * inv[None, :]
    cos, sin = jnp.cos(ang), jnp.sin(ang)
    cos_l = jnp.concatenate([cos, cos] * (LANES // HEAD_DIM), axis=1)
    sin_l = jnp.concatenate([-sin, sin] * (LANES // HEAD_DIM), axis=1)
    return cos_l, sin_l


def _cover_matrix(n_rows, n_cmp, n_slc):
    cs = np.arange(n_cmp) * CMP_STRIDE
    ss = np.arange(n_slc) * SLC_BLOCK
    cov = np.clip(np.minimum(cs[:, None] + CMP_BLOCK, ss[None, :] + SLC_BLOCK)
                  - np.maximum(cs[:, None], ss[None, :]), 0, None).astype(np.float32) / CMP_BLOCK
    out = np.zeros((n_rows, LANES), np.float32)
    out[:n_cmp, :n_slc] = cov
    return jnp.asarray(out, BF)


def _expand_matrix(n_keys):
    blk = np.arange(LANES)[:, None]
    key = np.arange(n_keys)[None, :] // SLC_BLOCK
    return jnp.asarray((blk == key).astype(np.float32), BF)


def _block_diag_mean(width):
    idx = np.arange(width) // HEAD_DIM
    return jnp.asarray((idx[:, None] == idx[None, :]).astype(np.float32) / HEAD_DIM, BF)


def _compress_weights(cmp_w_l, cmp_pe_l):
    z = jnp.zeros_like(cmp_w_l)
    wbig = jnp.concatenate([jnp.concatenate([cmp_w_l, z], axis=3), jnp.concatenate([z, cmp_w_l], axis=3)],
                           axis=2).astype(BF)
    pebig = jnp.concatenate([cmp_pe_l, cmp_pe_l], axis=2)
    return wbig, pebig


def kernel(x_prompt, x_sample, cache_cmp, cache_slc, state_win, page_table, c_prompt, c_sample, norm_mix_g, norm_ffn_g, w_ada, b_ada, w_in, w_spatial, b_spatial, q_norm_g, k_norm_g, cmp_pe, cmp_w, out_norm_g, w_out, ffn_w_gate, ffn_w_up, ffn_w_down, router_w, router_b, moe_w_gate, moe_w_up, moe_w_down):
    batch, t, d = x_prompt.shape
    db, tn, _ = x_sample.shape
    depth = w_ada.shape[0]
    n_pool, page = cache_cmp.shape[1], cache_cmp.shape[2]
    n_pages = page_table.shape[1]
    past = n_pages * page
    wb = state_win.shape[2]
    d_a = d // 2
    a_groups = d_a // HEAD_DIM
    assert tn == 1 and past % CHUNK == 0 and t % KEY_TILE == 0 and d_a % LANES == 0
    assert d - d_a == B_KV_HEADS * GQA_R * HEAD_DIM and B_KV_HEADS * HEAD_DIM == LANES
    n_p = batch * t
    kvw = 2 * LANES
    p_main = 2 * d_a + (d - d_a) + N_BRANCH * kvw
    n_gate = w_in.shape[2] - p_main
    qpos_s = past

    m_rows = batch + db
    m_pad = -(-m_rows // 8) * 8
    c_all = jnp.concatenate([c_prompt, c_sample, jnp.zeros((m_pad - m_rows, d), F32)], axis=0)
    mods = _adaln(c_all, w_ada, b_ada)

    bd = _block_diag_mean(d_a)
    cos_p, sin_p = _rope_tables(jnp.arange(t, dtype=jnp.int32))
    cos_s, sin_s = _rope_tables(jnp.full((1,), qpos_s, jnp.int32))
    np_p = t // CMP_STRIDE
    cover_p = _cover_matrix(np_p, np_p - 1, t // SLC_BLOCK)
    expand_p = _expand_matrix(t)
    np_s = past // CMP_STRIDE
    ns_s = -(-(past + tn) // SLC_BLOCK)
    cover_s = _cover_matrix(np_s, np_s - 1, ns_s)
    expand_s = _expand_matrix(past)
    tril = jnp.tril(jnp.ones((CHUNK, CHUNK), dtype=bool))

    cache_cmp4 = cache_cmp.reshape(depth, n_pool, page, kvw)
    cache_slc4 = cache_slc.reshape(depth, n_pool, page, kvw)
    state_win4 = state_win.reshape(depth, db, wb, kvw)

    yp = x_prompt.reshape(n_p, d)
    ys = x_sample.reshape(db, d)
    outs = {k: [] for k in ("cmp_p", "cmp_s", "slc_p", "slc_s", "win_p", "win_s", "chv_s")}
    for l in range(depth):
        mod_p = [mods[l, :batch, k * d:(k + 1) * d].reshape(batch, 1, d) for k in range(6)]
        mod_s = [mods[l, batch:batch + db, k * d:(k + 1) * d] for k in range(6)]
        w_in_bf = jnp.concatenate([w_in[l], jnp.zeros((d, LANES - n_gate), F32)], axis=1).astype(BF)
        qg = jnp.tile(q_norm_g[l], (d - d_a) // HEAD_DIM)[None, :]
        kg = jnp.tile(k_norm_g[l], (1, LANES // HEAD_DIM))
        ws_p = jnp.where(tril, w_spatial[l], 0.0).astype(BF)
        bs_p = jnp.repeat(b_spatial[l].T, HEAD_DIM, axis=1)
        ws_s = jnp.repeat(w_spatial[l][:, 0, 0], HEAD_DIM)[None, :]
        bs_s = jnp.repeat(b_spatial[l][:, 0], HEAD_DIM)[None, :]
        wbig, pebig = _compress_weights(cmp_w[l], cmp_pe[l])
        out_g = out_norm_g[l].reshape(1, d)
        wout_bf = w_out[l].astype(BF)
        ng1 = norm_mix_g[l][None, :]
        ng2 = norm_ffn_g[l][None, :]

        a, q, cmp_r, slc_r, win_r, slc_bf, win_bf, gates = _pre_mix(
            yp, ng1, mod_p[0], mod_p[1], w_in_bf, cos_p, sin_p, bd, qg, kg, ws_p, bs_p,
            chunked=True, tokens_per_batch=t)
        kc, vc = _compress_prompt(cmp_r, wbig, pebig, batch)
        o = _nsa_prompt(q, gates, kc, vc, slc_bf, win_bf, cover_p, expand_p, batch)
        yp, h2p = _post_mix(a, o, yp, mod_p[2], out_g, wout_bf, bd, ng2, mod_p[3], mod_p[4],
                            per_batch=True, tokens_per_batch=t, tiled_h2=(l % 2 == 1))
        outs["cmp_p"].append(cmp_r.reshape(batch, t, 2, B_KV_HEADS, HEAD_DIM))
        outs["slc_p"].append(slc_r.reshape(batch, t, 2, B_KV_HEADS, HEAD_DIM))
        wbp = min(WINDOW, t)
        outs["win_p"].append(win_r.reshape(batch, t, 2, B_KV_HEADS, HEAD_DIM)[:, t - wbp:])

        a_s, q_s, cmp_s, slc_s, win_s, gates_s, v_s = _pre_mix(
            ys, ng1, mod_s[0], mod_s[1], w_in_bf, cos_s, sin_s, bd, qg, kg, ws_s, bs_s,
            chunked=False, tokens_per_batch=1)
        q3 = q_s.reshape(db, 1, d - d_a)
        ocmp, imp = _s_cmp(page_table, q3, cache_cmp4, l, wbig, pebig, cover_s, qpos_s)
        mask = _s_select(imp[:, 0:B_KV_HEADS, :].reshape(db * B_KV_HEADS, LANES), expand_s, qpos_s)
        o_s = _s_att(page_table, q3, gates_s.reshape(db, 1, LANES), mask.reshape(db, B_KV_HEADS, past), ocmp,
                     slc_s.reshape(db, 1, kvw), win_s.reshape(db, 1, kvw), state_win4, cache_slc4, l,
                     qpos_s, past)
        ys, h2s = _post_mix(a_s, o_s.reshape(db, d - d_a), ys, mod_s[2], out_g, wout_bf, bd, ng2, mod_s[3],
                            mod_s[4], per_batch=False, tokens_per_batch=1, tiled_h2=(l % 2 == 1))
        outs["cmp_s"].append(cmp_s.reshape(db, tn, 2, B_KV_HEADS, HEAD_DIM))
        outs["slc_s"].append(slc_s.reshape(db, tn, 2, B_KV_HEADS, HEAD_DIM))
        win_all = jnp.concatenate([state_win[l], win_s.reshape(db, tn, 2, B_KV_HEADS, HEAD_DIM)], axis=1)
        outs["win_s"].append(win_all[:, tn:])
        outs["chv_s"].append(v_s.reshape(db, tn, d_a))

        i = l // 2
        if l % 2 == 0:
            wg, wu, wd = ffn_w_gate[i].astype(BF), ffn_w_up[i].astype(BF), ffn_w_down[i].astype(BF)
            yp = _ffn_dense(h2p, yp, mod_p[5], wg, wu, wd, per_batch=True, tokens_per_batch=t)
            ys = _ffn_dense(h2s, ys, mod_s[5], wg, wu, wd, per_batch=False, tokens_per_batch=1)
        else:
            wg, wu, wd = moe_w_gate[i].astype(BF), moe_w_up[i].astype(BF), moe_w_down[i].astype(BF)
            n_exp = router_w.shape[2]
            rw = jnp.concatenate([router_w[i], jnp.zeros((d, LANES - n_exp), F32)], axis=1)
            rw_hi = rw.astype(BF)
            rw_lo = (rw - rw_hi.astype(F32)).astype(BF)
            rb = jnp.concatenate([router_b[i], jnp.zeros((LANES - n_exp,), F32)])[None, :]
            yp = _moe_ffn(h2p, yp, mod_p[5], rw_hi, rw_lo, rb, wg, wu, wd, per_batch=True, tokens_per_batch=t)
            ys = _moe_ffn(h2s, ys, mod_s[5], rw_hi, rw_lo, rb, wg, wu, wd, per_batch=False, tokens_per_batch=1)

    st = lambda k: jnp.stack(outs[k])
    return (yp.reshape(batch, t, d), ys.reshape(db, tn, d), st("cmp_p"), st("cmp_s"), st("slc_p"), st("slc_s"),
            st("win_p"), st("win_s"), st("chv_s"))
```

```python
import functools

import numpy as np
import jax
import jax.numpy as jnp
from jax import lax
from jax.experimental import pallas as pl
from jax.experimental.pallas import tpu as pltpu

F32 = jnp.float32
BF = jnp.bfloat16

HEAD_DIM = 64
B_KV_HEADS = 2
GQA_R = 4
N_BRANCH = 3
CHUNK = 128
CMP_BLOCK = 32
CMP_STRIDE = 16
SLC_BLOCK = 64
N_SELECT = 16
WINDOW = 512
Q_BLOCK = 128
ROPE_THETA = 10000.0
TOP_K = 2
MOE_BLOCK = 128
NEG_INF = -1e30
FORCE_SCORE = 1e30
EPS = 1e-6
LOG2_E = 1.4426950408889634

LANES = 128
TOKEN_TILE_ROWS = 8
KEY_TILE = 512
VMEM_LIMIT = 56 * 1024 * 1024


def _cparams(sem):
    return pltpu.CompilerParams(dimension_semantics=sem, vmem_limit_bytes=VMEM_LIMIT)


def _dot(a, b):
    return jnp.dot(a, b, preferred_element_type=F32)


def _dot_nt(a, b):
    return lax.dot_general(a, b, (((1,), (1,)), ((), ())), preferred_element_type=F32)


def _split_dot(x, w):
    hi = x.astype(BF)
    lo = (x - hi.astype(F32)).astype(BF)
    return _dot(hi, w) + _dot(lo, w)


def _ada_kernel(c_ref, w_ref, b_ref, o_ref):
    c = c_ref[...]
    s = c * jax.nn.sigmoid(c)
    o_ref[0] = _dot(s.astype(BF), w_ref[0].astype(BF)) + b_ref[0]


def _adaln(c_all, w_ada, b_ada):
    depth, d, n6 = w_ada.shape
    m = c_all.shape[0]
    tn = 1536
    return pl.pallas_call(
        _ada_kernel,
        grid=(depth, n6 // tn),
        in_specs=[pl.BlockSpec((m, d), lambda l, j: (0, 0)),
                  pl.BlockSpec((1, d, tn), lambda l, j: (l, 0, j)),
                  pl.BlockSpec((1, 1, tn), lambda l, j: (l, 0, j))],
        out_specs=pl.BlockSpec((1, m, tn), lambda l, j: (l, 0, j)),
        out_shape=jax.ShapeDtypeStruct((depth, m, n6), F32),
        compiler_params=_cparams(("arbitrary", "arbitrary")),
        name="adaln",
    )(c_all, w_ada, b_ada.reshape(depth, 1, n6))


def _pre_kernel(x_ref, g_ref, sh_ref, sc_ref, w_ref, cos_ref, sin_ref, bd_ref, qg_ref, kg_ref,
                ws_ref, bs_ref, *out_refs, chunked, d_a, d_b):
    if chunked:
        a_ref, q_ref, cmp_ref, slc_ref, win_ref, slcb_ref, winb_ref, gate_ref = out_refs
    else:
        a_ref, q_ref, cmp_ref, slc_ref, win_ref, gate_ref, v_ref = out_refs
    tm = x_ref.shape[0]
    x = x_ref[...]
    ms = jnp.mean(x * x, axis=-1, keepdims=True)
    h = x * lax.rsqrt(ms + EPS) * g_ref[...]
    h = h * (1.0 + sc_ref[...]) + sh_ref[...]
    z = _dot(h.astype(BF), w_ref[...])
    uv = jax.nn.gelu(z[:, :2 * d_a])
    u = uv[:, :d_a]
    v = uv[:, d_a:]
    bd = bd_ref[...]
    mu = _split_dot(v, bd)
    vc = v - mu
    vn = vc * lax.rsqrt(_split_dot(vc * vc, bd) + EPS)

    cos = cos_ref[...]
    sin = sin_ref[...]
    lane = lax.broadcasted_iota(jnp.int32, (1, LANES), 1)
    first_half = (lane % HEAD_DIM) < (HEAD_DIM // 2)

    def rope(t):
        swapped = jnp.where(first_half, pltpu.roll(t, LANES - HEAD_DIM // 2, 1),
                            pltpu.roll(t, HEAD_DIM // 2, 1))
        return t * cos + swapped * sin

    o = 2 * d_a
    zq = z[:, o:o + d_b]
    qn = zq * lax.rsqrt(_split_dot(zq * zq, bd) + EPS) * qg_ref[...]
    for c in range(d_b // LANES):
        q_ref[:, c * LANES:(c + 1) * LANES] = rope(qn[:, c * LANES:(c + 1) * LANES]) * (HEAD_DIM ** -0.5)
    o += d_b
    bd1 = bd_ref[0:LANES, 0:LANES]
    row_refs = (cmp_ref, slc_ref, win_ref)
    bf_refs = (None, slcb_ref, winb_ref) if chunked else (None, None, None)
    for br in range(N_BRANCH):
        kp = z[:, o:o + LANES]
        vp = z[:, o + LANES:o + 2 * LANES]
        kn = kp * lax.rsqrt(_split_dot(kp * kp, bd1) + EPS) * kg_ref[br:br + 1, :]
        kr = rope(kn)
        row_refs[br][:, 0:LANES] = kr
        row_refs[br][:, LANES:2 * LANES] = vp
        if bf_refs[br] is not None:
            bf_refs[br][:, 0:LANES] = kr.astype(BF)
            bf_refs[br][:, LANES:2 * LANES] = vp.astype(BF)
        o += 2 * LANES
    gate_ref[...] = jax.nn.sigmoid(z[:, o:o + LANES])

    if chunked:
        low = lane < HEAD_DIM
        for c in range(tm // CHUNK):
            rs = slice(c * CHUNK, (c + 1) * CHUNK)
            for j in range(d_a // LANES):
                cs = slice(j * LANES, (j + 1) * LANES)
                vpair = vn[rs, cs].astype(BF)
                m0 = _dot(ws_ref[2 * j], vpair)
                m1 = _dot(ws_ref[2 * j + 1], vpair)
                mixed = jnp.where(low, m0, m1) + bs_ref[:, cs]
                a_ref[rs, cs] = u[rs, cs] * mixed
    else:
        a_ref[...] = u * (vn * ws_ref[...] + bs_ref[...])
        v_ref[...] = vn


def _pre_mix(x, norm_g, shift, scale, w_in_bf, cos, sin, bd, qg, kg, ws, bs, *, chunked, tokens_per_batch):
    n, d = x.shape
    d_a = d_b = d // 2
    tm = 256 if chunked else n
    nt = n // tm
    pw = w_in_bf.shape[1]
    full = lambda shp: pl.BlockSpec(shp, lambda i: (0,) * len(shp))
    row = lambda w: pl.BlockSpec((tm, w), lambda i: (i, 0))
    if chunked:
        tpb = tokens_per_batch // tm
        mod_spec = pl.BlockSpec((None, 1, d), lambda i: (i // tpb, 0, 0))
        rot_spec = pl.BlockSpec((tm, LANES), lambda i: (i % tpb, 0))
        ws_spec, bs_spec = full(ws.shape), full(bs.shape)
        out_shape = [jax.ShapeDtypeStruct((n, d_a), F32), jax.ShapeDtypeStruct((n, d_b), F32),
                     jax.ShapeDtypeStruct((n, 2 * LANES), F32), jax.ShapeDtypeStruct((n, 2 * LANES), F32),
                     jax.ShapeDtypeStruct((n, 2 * LANES), F32), jax.ShapeDtypeStruct((n, 2 * LANES), BF),
                     jax.ShapeDtypeStruct((n, 2 * LANES), BF), jax.ShapeDtypeStruct((n, LANES), F32)]
        out_specs = [row(d_a), row(d_b), row(2 * LANES), row(2 * LANES), row(2 * LANES), row(2 * LANES),
                     row(2 * LANES), row(LANES)]
    else:
        mod_spec = row(d)
        rot_spec = full((1, LANES))
        ws_spec, bs_spec = full(ws.shape), full(bs.shape)
        out_shape = [jax.ShapeDtypeStruct((n, d_a), F32), jax.ShapeDtypeStruct((n, d_b), F32),
                     jax.ShapeDtypeStruct((n, 2 * LANES), F32), jax.ShapeDtypeStruct((n, 2 * LANES), F32),
                     jax.ShapeDtypeStruct((n, 2 * LANES), F32), jax.ShapeDtypeStruct((n, LANES), F32),
                     jax.ShapeDtypeStruct((n, d_a), F32)]
        out_specs = [row(d_a), row(d_b), row(2 * LANES), row(2 * LANES), row(2 * LANES), row(LANES), row(d_a)]
    return pl.pallas_call(
        functools.partial(_pre_kernel, chunked=chunked, d_a=d_a, d_b=d_b),
        grid=(nt,),
        in_specs=[row(d), full((1, d)), mod_spec, mod_spec, full((d, pw)), rot_spec, rot_spec,
                  full(bd.shape), full(qg.shape), full(kg.shape), ws_spec, bs_spec],
        out_specs=out_specs,
        out_shape=out_shape,
        compiler_params=_cparams(("arbitrary",)),
        name="pre_mix_prompt" if chunked else "pre_mix_sample",
    )(x, norm_g, shift, scale, w_in_bf, cos, sin, bd, qg, kg, ws, bs)


def _compress_pieces(load_piece_rows, w_ref, pe_ref, kv, n_piece):
    assert CMP_BLOCK == 2 * CMP_STRIDE
    acc_a = jnp.zeros((n_piece, LANES), F32)
    acc_b = jnp.zeros((n_piece, LANES), F32)
    for l in range(CMP_STRIDE):
        xl = load_piece_rows(l)
        l2 = CMP_STRIDE + l
        acc_a = acc_a + _dot((xl + pe_ref[kv, l:l + 1, :]).astype(BF), w_ref[kv, l])
        acc_b = acc_b + _dot((xl + pe_ref[kv, l2:l2 + 1, :]).astype(BF), w_ref[kv, l2])
    shifted = pltpu.roll(acc_b, n_piece - 1, 0)
    rows = lax.broadcasted_iota(jnp.int32, (n_piece, 1), 0)
    return jnp.where(rows < n_piece - 1, acc_a + shifted, 0.0)


def _compress_kernel(xk_ref, xv_ref, w_ref, pe_ref, kc_ref, vc_ref):
    n_piece = kc_ref.shape[0]
    for kv, (x_ref, o_ref) in enumerate(((xk_ref, kc_ref), (xv_ref, vc_ref))):
        out = _compress_pieces(lambda l: x_ref[pl.ds(l, n_piece, stride=CMP_STRIDE), :], w_ref, pe_ref, kv, n_piece)
        o_ref[...] = out.astype(BF)


def _compress_prompt(cmp_rows, wbig, pebig, batch):
    n = cmp_rows.shape[0]
    t = n // batch
    n_piece = t // CMP_STRIDE
    out = jax.ShapeDtypeStruct((batch * n_piece, LANES), BF)
    return pl.pallas_call(
        _compress_kernel,
        grid=(batch,),
        in_specs=[pl.BlockSpec((t, LANES), lambda b: (b, 0)),
                  pl.BlockSpec((t, LANES), lambda b: (b, 1)),
                  pl.BlockSpec(wbig.shape, lambda b: (0, 0, 0, 0)),
                  pl.BlockSpec(pebig.shape, lambda b: (0, 0, 0))],
        out_specs=[pl.BlockSpec((n_piece, LANES), lambda b: (b, 0))] * 2,
        out_shape=[out, out],
        compiler_params=_cparams(("arbitrary",)),
        name="compress_prompt",
    )(cmp_rows, cmp_rows, wbig, pebig)


def _group_queries(q, g):
    m = q.shape[0]
    low = lax.broadcasted_iota(jnp.int32, (m, LANES), 1) < HEAD_DIM
    keep = low if g == 0 else jnp.logical_not(low)
    parts = []
    for r in range(GQA_R):
        h = GQA_R * g + r
        t = q[:, (h // 2) * LANES:(h // 2 + 1) * LANES]
        if h % 2 != g:
            t = pltpu.roll(t, HEAD_DIM, 1)
        parts.append(jnp.where(keep, t, 0.0))
    return parts


def _top_select(imp, blk_f):
    sel = jnp.zeros(imp.shape, F32)
    work = imp
    for _ in range(N_SELECT):
        mx = jnp.max(work, axis=-1, keepdims=True)
        first = jnp.min(jnp.where(work == mx, blk_f, float(LANES)), axis=-1, keepdims=True)
        hit = blk_f == first
        sel = jnp.where(hit, 1.0, sel)
        work = jnp.where(hit, -jnp.inf, work)
    return sel


def _block_scores(imp, qpos):
    blk = lax.broadcasted_iota(jnp.int32, imp.shape, 1)
    cur = qpos // SLC_BLOCK
    forced = (blk == 0) | (blk == cur) | (blk == cur - 1)
    valid = blk <= cur
    imp = jnp.where(forced, FORCE_SCORE, imp)
    imp = jnp.where(valid, imp, NEG_INF)
    return imp, blk.astype(F32), valid


def _merge_heads(res, m):
    low = lax.broadcasted_iota(jnp.int32, (m, LANES), 1) < HEAD_DIM
    cols = []
    for c in range(4):
        a, b = res[2 * c], res[2 * c + 1]
        if c // 2 == 0:
            cols.append(jnp.where(low, a, pltpu.roll(b, HEAD_DIM, 1)))
        else:
            cols.append(jnp.where(low, pltpu.roll(a, HEAD_DIM, 1), b))
    return jnp.concatenate(cols, axis=1)


def _nsa_prompt_kernel(q_ref, gate_ref, kc_ref, vc_ref, slc_ref, win_ref, cover_ref, e_ref, o_ref):
    i = pl.program_id(1)
    s0 = i * Q_BLOCK
    n_cmp = kc_ref.shape[0]
    rows4 = GQA_R * Q_BLOCK
    q = q_ref[...] * LOG2_E
    gates = gate_ref[...]
    tpos = s0 + lax.broadcasted_iota(jnp.int32, (Q_BLOCK, 1), 0)
    tpos4 = jnp.concatenate([tpos] * GQA_R, axis=0)
    tile4 = lambda x: jnp.concatenate([x] * GQA_R, axis=0)
    groups = range(B_KV_HEADS)
    qg = [jnp.concatenate(_group_queries(q, g), axis=0).astype(BF) for g in groups]

    kc = kc_ref[...]
    vc = vc_ref[...]
    c_last = lax.broadcasted_iota(jnp.int32, (1, n_cmp), 1) * CMP_STRIDE + (CMP_BLOCK - 1)
    c_bias = jnp.where(c_last <= tpos4, 0.0, NEG_INF)
    has_cmp = jnp.where(tpos4 >= CMP_BLOCK - 1, 1.0, 0.0)
    o_cmp, p_sums = [], []
    for g in groups:
        s = _dot_nt(qg[g], kc) + c_bias
        e = jnp.exp2(s - jnp.max(s, axis=-1, keepdims=True))
        p = e * (has_cmp / jnp.sum(e, axis=-1, keepdims=True))
        o_cmp.append(_dot(p.astype(BF), vc))
        p_sum = p[0:Q_BLOCK]
        for r in range(1, GQA_R):
            p_sum = p_sum + p[r * Q_BLOCK:(r + 1) * Q_BLOCK]
        p_sums.append(p_sum)

    imp = _split_dot(jnp.concatenate(p_sums, axis=0), cover_ref[...])
    imp, blk_f, valid = _block_scores(imp, jnp.concatenate([tpos] * B_KV_HEADS, axis=0))
    sel_all = jnp.where(valid, _top_select(imp, blk_f), 0.0).astype(BF)
    sel_b = [sel_all[g * Q_BLOCK:(g + 1) * Q_BLOCK] for g in groups]

    w_start = pl.multiple_of(jnp.maximum(s0 - WINDOW, 0), Q_BLOCK)
    w_len = WINDOW + Q_BLOCK
    wpos = w_start + lax.broadcasted_iota(jnp.int32, (1, w_len), 1)
    w_bias4 = tile4(jnp.where((wpos <= tpos) & (wpos > tpos - WINDOW), 0.0, NEG_INF))
    kw = win_ref[pl.ds(w_start, w_len), 0:LANES]
    vw = win_ref[pl.ds(w_start, w_len), LANES:2 * LANES]
    o_win = []
    for g in groups:
        sw = _dot_nt(qg[g], kw) + w_bias4
        ew = jnp.exp2(sw - jnp.max(sw, axis=-1, keepdims=True))
        o_win.append(_dot(ew.astype(BF), vw) / jnp.sum(ew, axis=-1, keepdims=True))

    def body(kt, carry):
        k0 = pl.multiple_of(kt * KEY_TILE, KEY_TILE)
        kk = slc_ref[pl.ds(k0, KEY_TILE), 0:LANES]
        vv = slc_ref[pl.ds(k0, KEY_TILE), LANES:2 * LANES]
        et = e_ref[:, pl.ds(k0, KEY_TILE)]
        kpos = k0 + lax.broadcasted_iota(jnp.int32, (1, KEY_TILE), 1)
        causal = kpos <= tpos
        out = []
        for g in groups:
            m, l, acc = carry[g]
            bias = jnp.where((_dot(sel_b[g], et) > 0.5) & causal, 0.0, NEG_INF)
            sc = _dot_nt(qg[g], kk) + tile4(bias)
            m_new = jnp.maximum(m, jnp.max(sc, axis=-1, keepdims=True))
            alpha = jnp.exp2(m - m_new)
            pe = jnp.exp2(sc - m_new)
            l = alpha * l + jnp.sum(pe, axis=-1, keepdims=True)
            acc = alpha * acc + _dot(pe.astype(BF), vv)
            out.append((m_new, l, acc))
        return tuple(out)

    init = tuple((jnp.full((rows4, 1), NEG_INF, F32), jnp.zeros((rows4, 1), F32), jnp.zeros((rows4, LANES), F32))
                 for _ in groups)
    n_tiles = (s0 + Q_BLOCK + KEY_TILE - 1) // KEY_TILE
    fin = lax.fori_loop(0, n_tiles, body, init)

    res = []
    for g in groups:
        o_slc = fin[g][2] / fin[g][1]
        for r in range(GQA_R):
            h = GQA_R * g + r
            rs = slice(r * Q_BLOCK, (r + 1) * Q_BLOCK)
            res.append(gates[:, 3 * h:3 * h + 1] * o_cmp[g][rs] + gates[:, 3 * h + 1:3 * h + 2] * o_slc[rs]
                       + gates[:, 3 * h + 2:3 * h + 3] * o_win[g][rs])
    o_ref[...] = _merge_heads(res, Q_BLOCK)


def _nsa_prompt(q, gates, kc, vc, slc_bf, win_bf, cover, expand, batch):
    n, d_b = q.shape
    t = n // batch
    nq = t // Q_BLOCK
    n_cmp = kc.shape[0] // batch
    return pl.pallas_call(
        _nsa_prompt_kernel,
        grid=(batch, nq),
        in_specs=[pl.BlockSpec((Q_BLOCK, d_b), lambda b, i: (b * nq + i, 0)),
                  pl.BlockSpec((Q_BLOCK, LANES), lambda b, i: (b * nq + i, 0)),
                  pl.BlockSpec((n_cmp, LANES), lambda b, i: (b, 0)),
                  pl.BlockSpec((n_cmp, LANES), lambda b, i: (b, 0)),
                  pl.BlockSpec((t, 2 * LANES), lambda b, i: (b, 0)),
                  pl.BlockSpec((t, 2 * LANES), lambda b, i: (b, 0)),
                  pl.BlockSpec(cover.shape, lambda b, i: (0, 0)),
                  pl.BlockSpec(expand.shape, lambda b, i: (0, 0))],
        out_specs=pl.BlockSpec((Q_BLOCK, d_b), lambda b, i: (b * nq + i, 0)),
        out_shape=jax.ShapeDtypeStruct((n, d_b), F32),
        compiler_params=_cparams(("arbitrary", "arbitrary")),
        name="nsa_prompt",
    )(q, gates, kc, vc, slc_bf, win_bf, cover, expand)


def _sample_queries(q_row):
    sub = lax.broadcasted_iota(jnp.int32, (8, LANES), 0)
    out = jnp.zeros((8, LANES), F32)
    for g in range(B_KV_HEADS):
        for r, part in enumerate(_group_queries(q_row, g)):
            out = jnp.where(sub == GQA_R * g + r, jnp.broadcast_to(part, (8, LANES)), out)
    return out


def _s_cmp_kernel(pt_ref, q_ref, *refs, n_pages, qpos):
    k_pages = refs[:n_pages]
    v_pages = refs[n_pages:2 * n_pages]
    wbig_ref, pe_ref, cover_ref, ocmp_ref, imp_ref = refs[2 * n_pages:]
    per_page = k_pages[0].shape[0] // CMP_STRIDE
    n_piece = n_pages * per_page

    def loader(pages):
        return lambda l: jnp.concatenate([pg[pl.ds(l, per_page, stride=CMP_STRIDE), :] for pg in pages], axis=0)

    kc = _compress_pieces(loader(k_pages), wbig_ref, pe_ref, 0, n_piece).astype(BF)
    vc = _compress_pieces(loader(v_pages), wbig_ref, pe_ref, 1, n_piece).astype(BF)
    qa = _sample_queries(q_ref[...])
    c_last = lax.broadcasted_iota(jnp.int32, (1, n_piece), 1) * CMP_STRIDE + (CMP_BLOCK - 1)
    cv = c_last <= qpos
    s = jnp.where(cv, _dot_nt(qa.astype(BF), kc), NEG_INF)
    e = jnp.where(cv, jnp.exp(s - jnp.max(s, axis=-1, keepdims=True)), 0.0)
    p = e / jnp.maximum(jnp.sum(e, axis=-1, keepdims=True), 1e-30)
    ocmp_ref[...] = _dot(p.astype(BF), vc)
    sub = lax.broadcasted_iota(jnp.int32, (8, n_piece), 0)
    ps0 = jnp.sum(jnp.where(sub < GQA_R, p, 0.0), axis=0, keepdims=True)
    ps1 = jnp.sum(jnp.where(sub >= GQA_R, p, 0.0), axis=0, keepdims=True)
    p_sum = jnp.where(sub == 0, ps0, jnp.where(sub == 1, ps1, 0.0))
    imp_ref[...] = _split_dot(p_sum, cover_ref[...])


def _s_cmp(page_table, q3, cache_l, layer, wbig, pebig, cover, qpos):
    db = q3.shape[0]
    n_pages = page_table.shape[1]
    page = cache_l.shape[2]
    n_piece = n_pages * page // CMP_STRIDE
    page_specs = [pl.BlockSpec((None, None, page, LANES),
                               functools.partial(lambda b, pt, j, kv: (layer, pt[b, j], 0, kv), j=j, kv=kv))
                  for kv in range(2) for j in range(n_pages)]
    gs = pltpu.PrefetchScalarGridSpec(
        num_scalar_prefetch=1,
        grid=(db,),
        in_specs=[pl.BlockSpec((None, 1, q3.shape[2]), lambda b, pt: (b, 0, 0))] + page_specs + [
            pl.BlockSpec(wbig.shape, lambda b, pt: (0, 0, 0, 0)),
            pl.BlockSpec(pebig.shape, lambda b, pt: (0, 0, 0)),
            pl.BlockSpec(cover.shape, lambda b, pt: (0, 0))],
        out_specs=[pl.BlockSpec((None, 8, LANES), lambda b, pt: (b, 0, 0)),
                   pl.BlockSpec((None, 8, LANES), lambda b, pt: (b, 0, 0))],
    )
    assert cover.shape[0] == n_piece
    return pl.pallas_call(
        functools.partial(_s_cmp_kernel, n_pages=n_pages, qpos=qpos),
        grid_spec=gs,
        out_shape=[jax.ShapeDtypeStruct((db, 8, LANES), F32), jax.ShapeDtypeStruct((db, 8, LANES), F32)],
        compiler_params=_cparams(("arbitrary",)),
        name="nsa_sample_cmp",
    )(page_table, q3, *([cache_l] * (2 * n_pages)), wbig, pebig, cover)


def _s_select_kernel(imp_ref, e_ref, o_ref, *, qpos):
    imp = imp_ref[...]
    pos = jnp.full((imp.shape[0], 1), qpos, jnp.int32)
    imp, blk_f, valid = _block_scores(imp, pos)
    sel = _top_select(imp, blk_f)
    sel_b = jnp.where(valid, sel, 0.0).astype(BF)
    o_ref[...] = _dot(sel_b, e_ref[...])


def _s_select(imp2, expand, qpos):
    m = imp2.shape[0]
    past = expand.shape[1]
    return pl.pallas_call(
        functools.partial(_s_select_kernel, qpos=qpos),
        grid=(1,),
        in_specs=[pl.BlockSpec(imp2.shape, lambda i: (0, 0)), pl.BlockSpec(expand.shape, lambda i: (0, 0))],
        out_specs=pl.BlockSpec((m, past), lambda i: (0, 0)),
        out_shape=jax.ShapeDtypeStruct((m, past), F32),
        compiler_params=_cparams(("arbitrary",)),
        name="nsa_sample_select",
    )(imp2, expand)


def _s_att_kernel(pt_ref, q_ref, gate_ref, mask_ref, ocmp_ref, slcn_ref, winn_ref, win_ref, *refs,
                  n_pages, qpos, past):
    pages = refs[:n_pages]
    o_ref = refs[n_pages]
    page = pages[0].shape[1]
    qa = _sample_queries(q_ref[...])
    qb = qa.astype(BF)
    sub = lax.broadcasted_iota(jnp.int32, (8, 1), 0)
    mrow = jnp.where(sub < GQA_R, mask_ref[0:1, :], mask_ref[1:2, :])
    bias = jnp.where(mrow > 0.5, 0.0, NEG_INF)
    sc = jnp.concatenate([_dot(qb, pg[0:LANES, :].astype(BF)) for pg in pages], axis=1) + bias
    kn = slcn_ref[:, 0:LANES]
    vn = slcn_ref[:, LANES:2 * LANES]
    s_new = jnp.sum(qa * kn, axis=-1, keepdims=True)
    m = jnp.maximum(jnp.max(sc, axis=-1, keepdims=True), s_new)
    e = jnp.exp(sc - m)
    e_new = jnp.exp(s_new - m)
    acc = e_new * vn
    for j, pg in enumerate(pages):
        acc = acc + _dot_nt(e[:, j * page:(j + 1) * page].astype(BF), pg[LANES:2 * LANES, :].astype(BF))
    o_slc = acc / (jnp.sum(e, axis=-1, keepdims=True) + e_new)
    wb = win_ref.shape[1]
    wpos = (past - wb) + lax.broadcasted_iota(jnp.int32, (1, wb), 1)
    wvalid = (wpos <= qpos) & (wpos > qpos - WINDOW) & (wpos >= 0)
    sw = _dot(qb, win_ref[0:LANES, :].astype(BF)) + jnp.where(wvalid, 0.0, NEG_INF)
    kwn = winn_ref[:, 0:LANES]
    vwn = winn_ref[:, LANES:2 * LANES]
    sw_new = jnp.sum(qa * kwn, axis=-1, keepdims=True)
    mw = jnp.maximum(jnp.max(sw, axis=-1, keepdims=True), sw_new)
    ew = jnp.exp(sw - mw)
    ew_new = jnp.exp(sw_new - mw)
    o_win = (_dot_nt(ew.astype(BF), win_ref[LANES:2 * LANES, :].astype(BF)) + ew_new * vwn) / (
        jnp.sum(ew, axis=-1, keepdims=True) + ew_new)
    lane = lax.broadcasted_iota(jnp.int32, (8, LANES), 1)
    gt = jnp.broadcast_to(gate_ref[...], (8, LANES))
    gsel = [jnp.sum(jnp.where(lane == 3 * sub + br, gt, 0.0), axis=-1, keepdims=True) for br in range(N_BRANCH)]
    res = gsel[0] * ocmp_ref[...] + gsel[1] * o_slc + gsel[2] * o_win
    o_ref[...] = _merge_heads([res[h:h + 1, :] for h in range(8)], 1)


def _s_att(page_table, q3, gates3, mask3, ocmp, slc_new3, win_new3, win_l, cache_l, layer, qpos, past):
    db = q3.shape[0]
    n_pages = page_table.shape[1]
    page = cache_l.shape[3]
    wb = win_l.shape[3]
    page_specs = [pl.BlockSpec((None, None, 2 * LANES, page),
                               functools.partial(lambda b, pt, j: (layer, pt[b, j], 0, 0), j=j))
                  for j in range(n_pages)]
    per_b = lambda r, w: pl.BlockSpec((None, r, w), lambda b, pt: (b, 0, 0))
    gs = pltpu.PrefetchScalarGridSpec(
        num_scalar_prefetch=1,
        grid=(db,),
        in_specs=[per_b(1, q3.shape[2]), per_b(1, LANES), per_b(2, past), per_b(8, LANES),
                  per_b(1, 2 * LANES), per_b(1, 2 * LANES),
                  pl.BlockSpec((None, None, 2 * LANES, wb), lambda b, pt: (layer, b, 0, 0))] + page_specs,
        out_specs=per_b(1, q3.shape[2]),
    )
    return pl.pallas_call(
        functools.partial(_s_att_kernel, n_pages=n_pages, qpos=qpos, past=past),
        grid_spec=gs,
        out_shape=jax.ShapeDtypeStruct(q3.shape, F32),
        compiler_params=_cparams(("arbitrary",)),
        name="nsa_sample_att",
    )(page_table, q3, gates3, mask3, ocmp, slc_new3, win_new3, win_l, *([cache_l] * n_pages))


def _untile(ref, tm):
    return jnp.concatenate([ref[pl.ds(c, tm, stride=TOKEN_TILE_ROWS), :] for c in range(TOKEN_TILE_ROWS)], axis=1)


def _store_tiled(ref, val):
    tm = val.shape[0]
    for c in range(TOKEN_TILE_ROWS):
        ref[pl.ds(c, tm, stride=TOKEN_TILE_ROWS), :] = val[:, c * LANES:(c + 1) * LANES]


def _post_kernel(a_ref, o_ref, x_ref, g1_ref, og_ref, wout_ref, bd_ref, ng_ref, sh2_ref, sc2_ref, y_ref, h2_ref,
                 *, tiled_h2):
    bd = bd_ref[...]
    d_a = a_ref.shape[1]

    def head_norm(t, g):
        return t * lax.rsqrt(_split_dot(t * t, bd) + EPS) * g

    a = head_norm(a_ref[...], og_ref[:, :d_a])
    o = head_norm(o_ref[...], og_ref[:, d_a:])
    mix = _dot(a.astype(BF), wout_ref[:d_a, :]) + _dot(o.astype(BF), wout_ref[d_a:, :])
    y = x_ref[...] + g1_ref[...] * mix
    y_ref[...] = y
    ms = jnp.mean(y * y, axis=-1, keepdims=True)
    h2 = y * lax.rsqrt(ms + EPS) * ng_ref[...]
    h2 = h2 * (1.0 + sc2_ref[...]) + sh2_ref[...]
    if tiled_h2:
        _store_tiled(h2_ref, h2)
    else:
        h2_ref[...] = h2


def _mod_spec(per_batch, tm, d, tokens_per_batch):
    if per_batch:
        tpb = tokens_per_batch // tm
        return pl.BlockSpec((None, 1, d), lambda i: (i // tpb, 0, 0))
    return pl.BlockSpec((tm, d), lambda i: (i, 0))


def _post_mix(a, o, x, gate1, out_g, wout_bf, bd, ffn_g, shift2, scale2, *, per_batch, tokens_per_batch, tiled_h2):
    n, d = x.shape
    tm = 512 if per_batch else n
    full = lambda shp: pl.BlockSpec(shp, lambda i: (0,) * len(shp))
    row = lambda w: pl.BlockSpec((tm, w), lambda i: (i, 0))
    ms = _mod_spec(per_batch, tm, d, tokens_per_batch)
    if tiled_h2:
        assert d == TOKEN_TILE_ROWS * LANES
        h2_spec = pl.BlockSpec((tm * TOKEN_TILE_ROWS, LANES), lambda i: (i, 0))
        h2_shape = jax.ShapeDtypeStruct((n * TOKEN_TILE_ROWS, LANES), F32)
    else:
        h2_spec, h2_shape = row(d), jax.ShapeDtypeStruct((n, d), F32)
    return pl.pallas_call(
        functools.partial(_post_kernel, tiled_h2=tiled_h2),
        grid=(n // tm,),
        in_specs=[row(a.shape[1]), row(o.shape[1]), row(d), ms, full((1, d)), full(wout_bf.shape), full(bd.shape),
                  full((1, d)), ms, ms],
        out_specs=[row(d), h2_spec],
        out_shape=[jax.ShapeDtypeStruct((n, d), F32), h2_shape],
        compiler_params=_cparams(("arbitrary",)),
        name="post_mix",
    )(a, o, x, gate1, out_g, wout_bf, bd, ffn_g, shift2, scale2)


def _ffn_kernel(h_ref, y_ref, g2_ref, wg_ref, wu_ref, wd_ref, o_ref, *, f_tile):
    hb = h_ref[...].astype(BF)
    d_ff = wg_ref.shape[1]
    acc = jnp.zeros(o_ref.shape, F32)
    for f in range(d_ff // f_tile):
        fs = slice(f * f_tile, (f + 1) * f_tile)
        gp = _dot(hb, wg_ref[:, fs])
        up = _dot(hb, wu_ref[:, fs])
        act = gp * jax.nn.sigmoid(gp) * up
        acc = acc + _dot(act.astype(BF), wd_ref[fs, :])
    o_ref[...] = y_ref[...] + g2_ref[...] * acc


def _ff_tile(d_ff):
    for cand in (1408, 1024, 512, 256, 128):
        if d_ff % cand == 0:
            return cand
    return d_ff


def _ffn_dense(h2, y, gate2, wg, wu, wd, *, per_batch, tokens_per_batch):
    n, d = y.shape
    tm = 512 if per_batch else n
    full = lambda shp: pl.BlockSpec(shp, lambda i: (0,) * len(shp))
    row = lambda w: pl.BlockSpec((tm, w), lambda i: (i, 0))
    return pl.pallas_call(
        functools.partial(_ffn_kernel, f_tile=_ff_tile(wg.shape[1])),
        grid=(n // tm,),
        in_specs=[row(d), row(d), _mod_spec(per_batch, tm, d, tokens_per_batch), full(wg.shape), full(wu.shape),
                  full(wd.shape)],
        out_specs=row(d),
        out_shape=jax.ShapeDtypeStruct((n, d), F32),
        compiler_params=_cparams(("arbitrary",)),
        name="ffn_dense",
    )(h2, y, gate2, wg, wu, wd)


def _router_kernel(h_ref, rwh_ref, rwl_ref, rb_ref, o_ref, *, n_experts):
    h = _untile(h_ref, o_ref.shape[0])
    hi = h.astype(BF)
    lo = (h - hi.astype(F32)).astype(BF)
    logits = _dot(hi, rwh_ref[...]) + _dot(lo, rwh_ref[...]) + _dot(hi, rwl_ref[...]) + rb_ref[...]
    lane = lax.broadcasted_iota(jnp.int32, logits.shape, 1)
    lane_f = lane.astype(F32)
    l1 = jnp.where(lane < n_experts, logits, -jnp.inf)
    m1 = jnp.max(l1, axis=-1, keepdims=True)
    i1 = jnp.min(jnp.where(l1 == m1, lane_f, float(LANES)), axis=-1, keepdims=True)
    l2 = jnp.where(lane_f == i1, -jnp.inf, l1)
    m2 = jnp.max(l2, axis=-1, keepdims=True)
    i2 = jnp.min(jnp.where(l2 == m2, lane_f, float(LANES)), axis=-1, keepdims=True)
    e2 = jnp.exp(m2 - m1)
    den = 1.0 + e2
    o_ref[...] = jnp.where(lane == 0, i1, jnp.where(lane == 1, i2, jnp.where(lane == 2, 1.0 / den,
                           jnp.where(lane == 3, e2 / den, 0.0))))


def _router(h2t, rw_hi, rw_lo, rb, n_experts):
    n = h2t.shape[0] // TOKEN_TILE_ROWS
    tm = 512 if n % 512 == 0 else n
    full = lambda shp: pl.BlockSpec(shp, lambda i: (0,) * len(shp))
    return pl.pallas_call(
        functools.partial(_router_kernel, n_experts=n_experts),
        grid=(n // tm,),
        in_specs=[pl.BlockSpec((tm * TOKEN_TILE_ROWS, LANES), lambda i: (i, 0)), full(rw_hi.shape),
                  full(rw_lo.shape), full(rb.shape)],
        out_specs=pl.BlockSpec((tm, LANES), lambda i: (i, 0)),
        out_shape=jax.ShapeDtypeStruct((n, LANES), F32),
        compiler_params=_cparams(("arbitrary",)),
        name="moe_router",
    )(h2t, rw_hi, rw_lo, rb)


def _gather_kernel(idx_ref, src_ref, out_ref, sem, *, rows_per_step):
    base = pl.program_id(0) * rows_per_step

    def row_copy(j):
        return pltpu.make_async_copy(src_ref.at[idx_ref[base + j]], out_ref.at[j], sem)

    def start(j, c):
        row_copy(j).start()
        return c

    def wait(j, c):
        row_copy(j).wait()
        return c

    lax.fori_loop(0, rows_per_step, start, 0)
    lax.fori_loop(0, rows_per_step, wait, 0)


def _gather_rows(src_t, idx, rows_per_step=512):
    m = idx.shape[0]
    while m % rows_per_step:
        rows_per_step //= 2
    src3 = src_t.reshape(-1, TOKEN_TILE_ROWS, LANES)
    gs = pltpu.PrefetchScalarGridSpec(
        num_scalar_prefetch=1,
        grid=(m // rows_per_step,),
        in_specs=[pl.BlockSpec(memory_space=pl.ANY)],
        out_specs=pl.BlockSpec((rows_per_step, TOKEN_TILE_ROWS, LANES), lambda i, idx_ref: (i, 0, 0)),
        scratch_shapes=[pltpu.SemaphoreType.DMA(())],
    )
    out = pl.pallas_call(
        functools.partial(_gather_kernel, rows_per_step=rows_per_step),
        grid_spec=gs,
        out_shape=jax.ShapeDtypeStruct((m, TOKEN_TILE_ROWS, LANES), src_t.dtype),
        compiler_params=pltpu.CompilerParams(dimension_semantics=("arbitrary",), vmem_limit_bytes=VMEM_LIMIT),
        name="gather_rows",
    )(idx, src3)
    return out.reshape(m * TOKEN_TILE_ROWS, LANES)


def _moe_block_kernel(be_ref, nu_ref, x_ref, wg_ref, wu_ref, wd_ref, o_ref, *, f_tile):
    i = pl.program_id(0)

    @pl.when(i < nu_ref[0])
    def _():
        hb = _untile(x_ref, MOE_BLOCK).astype(BF)
        d_ff = wg_ref.shape[1]
        acc = jnp.zeros((MOE_BLOCK, wd_ref.shape[1]), F32)
        for f in range(d_ff // f_tile):
            fs = slice(f * f_tile, (f + 1) * f_tile)
            gp = _dot(hb, wg_ref[:, fs])
            up = _dot(hb, wu_ref[:, fs])
            act = gp * jax.nn.sigmoid(gp) * up
            acc = acc + _dot(act.astype(BF), wd_ref[fs, :])
        _store_tiled(o_ref, acc)

    @pl.when(i >= nu_ref[0])
    def _():
        o_ref[...] = jnp.zeros(o_ref.shape, F32)


def _moe_blocks(blk_e, n_used, xb_t, wg, wu, wd):
    nb = xb_t.shape[0] // (MOE_BLOCK * TOKEN_TILE_ROWS)
    d, d_ff = wg.shape[1], wg.shape[2]
    blk = pl.BlockSpec((MOE_BLOCK * TOKEN_TILE_ROWS, LANES), lambda i, be, nu: (i, 0))
    gs = pltpu.PrefetchScalarGridSpec(
        num_scalar_prefetch=2,
        grid=(nb,),
        in_specs=[blk,
                  pl.BlockSpec((None, d, d_ff), lambda i, be, nu: (be[i], 0, 0)),
                  pl.BlockSpec((None, d, d_ff), lambda i, be, nu: (be[i], 0, 0)),
                  pl.BlockSpec((None, d_ff, d), lambda i, be, nu: (be[i], 0, 0))],
        out_specs=blk,
    )
    return pl.pallas_call(
        functools.partial(_moe_block_kernel, f_tile=_ff_tile(d_ff)),
        grid_spec=gs,
        out_shape=jax.ShapeDtypeStruct(xb_t.shape, F32),
        compiler_params=_cparams(("arbitrary",)),
        name="moe_blocks",
    )(blk_e, n_used, xb_t, wg, wu, wd)


def _combine_kernel(ya0_ref, ya1_ref, r_ref, y_ref, g2_ref, o_ref):
    tm = o_ref.shape[0]
    w1 = r_ref[:, 2:3]
    w2 = r_ref[:, 3:4]
    o_ref[...] = y_ref[...] + g2_ref[...] * (w1 * _untile(ya0_ref, tm) + w2 * _untile(ya1_ref, tm))


def _moe_combine(ya_t, route, y, gate2, *, per_batch, tokens_per_batch):
    n, d = y.shape
    tm = 512 if per_batch else n
    nt = n // tm
    row = lambda w: pl.BlockSpec((tm, w), lambda i: (i, 0))
    return pl.pallas_call(
        _combine_kernel,
        grid=(nt,),
        in_specs=[pl.BlockSpec((tm * TOKEN_TILE_ROWS, LANES), lambda i: (i, 0)),
                  pl.BlockSpec((tm * TOKEN_TILE_ROWS, LANES), lambda i: (i + nt, 0)), row(LANES), row(d),
                  _mod_spec(per_batch, tm, d, tokens_per_batch)],
        out_specs=row(d),
        out_shape=jax.ShapeDtypeStruct((n, d), F32),
        compiler_params=_cparams(("arbitrary",)),
        name="moe_combine",
    )(ya_t, ya_t, route, y, gate2)


def _moe_ffn(h2t, y, gate2, rw_hi, rw_lo, rb, wg, wu, wd, *, per_batch, tokens_per_batch):
    n, d = y.shape
    n_experts = wg.shape[0]
    route = _router(h2t, rw_hi, rw_lo, rb, n_experts)
    flat_e = route[:, 0:TOP_K].astype(jnp.int32).reshape(-1)
    onehot = (flat_e[:, None] == jnp.arange(n_experts, dtype=jnp.int32)[None, :]).astype(jnp.int32)
    csum = jnp.cumsum(onehot, axis=0)
    rank = jnp.sum(onehot * csum, axis=1) - 1
    counts = csum[-1]
    padded = (counts + MOE_BLOCK - 1) // MOE_BLOCK * MOE_BLOCK
    pad_end = jnp.cumsum(padded)
    pad_start = pad_end - padded
    slot = (pad_start[flat_e] + rank).astype(jnp.int32)
    nb = -(-(n * TOP_K) // MOE_BLOCK) + n_experts
    src = jnp.zeros((nb * MOE_BLOCK,), jnp.int32).at[slot].set(jnp.arange(n * TOP_K, dtype=jnp.int32) // TOP_K)
    blk_e = jnp.minimum(jnp.sum(pad_end[None, :] <= (jnp.arange(nb) * MOE_BLOCK)[:, None], axis=-1),
                        n_experts - 1).astype(jnp.int32)
    n_used = (pad_end[-1:] // MOE_BLOCK).astype(jnp.int32)
    xb = _gather_rows(h2t, src)
    yb = _moe_blocks(blk_e, n_used, xb, wg, wu, wd)
    ya = _gather_rows(yb, jnp.concatenate([slot[0::TOP_K], slot[1::TOP_K]]))
    return _moe_combine(ya, route, y, gate2, per_batch=per_batch, tokens_per_batch=tokens_per_batch)


def _rope_tables(pos):
    half = HEAD_DIM // 2
    inv = ROPE_THETA ** (-jnp.arange(half, dtype=F32) / half)
    ang = pos.astype(F32)[:, None] * inv[None, :]
    cos, sin = jnp.cos(ang), jnp.sin(ang)
    cos_l = jnp.concatenate([cos, cos] * (LANES // HEAD_DIM), axis=1)
    sin_l = jnp.concatenate([-sin, sin] * (LANES // HEAD_DIM), axis=1)
    return cos_l, sin_l


def _cover_matrix(n_rows, n_cmp, n_slc):
    cs = np.arange(n_cmp) * CMP_STRIDE
    ss = np.arange(n_slc) * SLC_BLOCK
    cov = np.clip(np.minimum(cs[:, None] + CMP_BLOCK, ss[None, :] + SLC_BLOCK)
                  - np.maximum(cs[:, None], ss[None, :]), 0, None).astype(np.float32) / CMP_BLOCK
    out = np.zeros((n_rows, LANES), np.float32)
    out[:n_cmp, :n_slc] = cov
    return jnp.asarray(out, BF)


def _expand_matrix(n_keys):
    blk = np.arange(LANES)[:, None]
    key = np.arange(n_keys)[None, :] // SLC_BLOCK
    return jnp.asarray((blk == key).astype(np.float32), BF)


def _block_diag_mean(width):
    idx = np.arange(width) // HEAD_DIM
    return jnp.asarray((idx[:, None] == idx[None, :]).astype(np.float32) / HEAD_DIM, BF)


def _compress_weights(cmp_w_l, cmp_pe_l):
    z = jnp.zeros_like(cmp_w_l)
    wbig = jnp.concatenate([jnp.concatenate([cmp_w_l, z], axis=3), jnp.concatenate([z, cmp_w_l], axis=3)],
                           axis=2).astype(BF)
    pebig = jnp.concatenate([cmp_pe_l, cmp_pe_l], axis=2)
    return wbig, pebig


def kernel(x_prompt, x_sample, cache_cmp, cache_slc, state_win, page_table, c_prompt, c_sample, norm_mix_g, norm_ffn_g, w_ada, b_ada, w_in, w_spatial, b_spatial, q_norm_g, k_norm_g, cmp_pe, cmp_w, out_norm_g, w_out, ffn_w_gate, ffn_w_up, ffn_w_down, router_w, router_b, moe_w_gate, moe_w_up, moe_w_down):
    batch, t, d = x_prompt.shape
    db, tn, _ = x_sample.shape
    depth = w_ada.shape[0]
    n_pool, page = cache_cmp.shape[1], cache_cmp.shape[2]
    n_pages = page_table.shape[1]
    past = n_pages * page
    wb = state_win.shape[2]
    d_a = d // 2
    a_groups = d_a // HEAD_DIM
    assert tn == 1 and past % CHUNK == 0 and t % KEY_TILE == 0 and d_a % LANES == 0
    assert d - d_a == B_KV_HEADS * GQA_R * HEAD_DIM and B_KV_HEADS * HEAD_DIM == LANES
    n_p = batch * t
    kvw = 2 * LANES
    p_main = 2 * d_a + (d - d_a) + N_BRANCH * kvw
    n_gate = w_in.shape[2] - p_main
    qpos_s = past

    m_rows = batch + db
    m_pad = -(-m_rows // 8) * 8
    c_all = jnp.concatenate([c_prompt, c_sample, jnp.zeros((m_pad - m_rows, d), F32)], axis=0)
    mods = _adaln(c_all, w_ada, b_ada)

    bd = _block_diag_mean(d_a)
    cos_p, sin_p = _rope_tables(jnp.arange(t, dtype=jnp.int32))
    cos_s, sin_s = _rope_tables(jnp.full((1,), qpos_s, jnp.int32))
    np_p = t // CMP_STRIDE
    cover_p = _cover_matrix(np_p, np_p - 1, t // SLC_BLOCK)
    expand_p = _expand_matrix(t)
    np_s = past // CMP_STRIDE
    ns_s = -(-(past + tn) // SLC_BLOCK)
    cover_s = _cover_matrix(np_s, np_s - 1, ns_s)
    expand_s = _expand_matrix(past)
    tril = jnp.tril(jnp.ones((CHUNK, CHUNK), dtype=bool))

    cache_cmp4 = cache_cmp.reshape(depth, n_pool, page, kvw)
    cache_slc4 = jnp.transpose(cache_slc, (0, 1, 3, 4, 5, 2)).reshape(depth, n_pool, kvw, page)
    state_win4 = jnp.transpose(state_win, (0, 1, 3, 4, 5, 2)).reshape(depth, db, kvw, wb)

    yp = x_prompt.reshape(n_p, d)
    ys = x_sample.reshape(db, d)
    outs = {k: [] for k in ("cmp_p", "cmp_s", "slc_p", "slc_s", "win_p", "win_s", "chv_s")}
    for l in range(depth):
        mod_p = [mods[l, :batch, k * d:(k + 1) * d].reshape(batch, 1, d) for k in range(6)]
        mod_s = [mods[l, batch:batch + db, k * d:(k + 1) * d] for k in range(6)]
        w_in_bf = jnp.concatenate([w_in[l], jnp.zeros((d, LANES - n_gate), F32)], axis=1).astype(BF)
        qg = jnp.tile(q_norm_g[l], (d - d_a) // HEAD_DIM)[None, :]
        kg = jnp.tile(k_norm_g[l], (1, LANES // HEAD_DIM))
        ws_p = jnp.where(tril, w_spatial[l], 0.0).astype(BF)
        bs_p = jnp.repeat(b_spatial[l].T, HEAD_DIM, axis=1)
        ws_s = jnp.repeat(w_spatial[l][:, 0, 0], HEAD_DIM)[None, :]
        bs_s = jnp.repeat(b_spatial[l][:, 0], HEAD_DIM)[None, :]
        wbig, pebig = _compress_weights(cmp_w[l], cmp_pe[l])
        out_g = out_norm_g[l].reshape(1, d)
        wout_bf = w_out[l].astype(BF)
        ng1 = norm_mix_g[l][None, :]
        ng2 = norm_ffn_g[l][None, :]

        a, q, cmp_r, slc_r, win_r, slc_bf, win_bf, gates = _pre_mix(
            yp, ng1, mod_p[0], mod_p[1], w_in_bf, cos_p, sin_p, bd, qg, kg, ws_p, bs_p,
            chunked=True, tokens_per_batch=t)
        kc, vc = _compress_prompt(cmp_r, wbig, pebig, batch)
        o = _nsa_prompt(q, gates, kc, vc, slc_bf, win_bf, cover_p, expand_p, batch)
        yp, h2p = _post_mix(a, o, yp, mod_p[2], out_g, wout_bf, bd, ng2, mod_p[3], mod_p[4],
                            per_batch=True, tokens_per_batch=t, tiled_h2=(l % 2 == 1))
        outs["cmp_p"].append(cmp_r.reshape(batch, t, 2, B_KV_HEADS, HEAD_DIM))
        outs["slc_p"].append(slc_r.reshape(batch, t, 2, B_KV_HEADS, HEAD_DIM))
        wbp = min(WINDOW, t)
        outs["win_p"].append(win_r.reshape(batch, t, 2, B_KV_HEADS, HEAD_DIM)[:, t - wbp:])

        a_s, q_s, cmp_s, slc_s, win_s, gates_s, v_s = _pre_mix(
            ys, ng1, mod_s[0], mod_s[1], w_in_bf, cos_s, sin_s, bd, qg, kg, ws_s, bs_s,
            chunked=False, tokens_per_batch=1)
        q3 = q_s.reshape(db, 1, d - d_a)
        ocmp, imp = _s_cmp(page_table, q3, cache_cmp4, l, wbig, pebig, cover_s, qpos_s)
        mask = _s_select(imp[:, 0:B_KV_HEADS, :].reshape(db * B_KV_HEADS, LANES), expand_s, qpos_s)
        o_s = _s_att(page_table, q3, gates_s.reshape(db, 1, LANES), mask.reshape(db, B_KV_HEADS, past), ocmp,
                     slc_s.reshape(db, 1, kvw), win_s.reshape(db, 1, kvw), state_win4, cache_slc4, l,
                     qpos_s, past)
        ys, h2s = _post_mix(a_s, o_s.reshape(db, d - d_a), ys, mod_s[2], out_g, wout_bf, bd, ng2, mod_s[3],
                            mod_s[4], per_batch=False, tokens_per_batch=1, tiled_h2=(l % 2 == 1))
        outs["cmp_s"].append(cmp_s.reshape(db, tn, 2, B_KV_HEADS, HEAD_DIM))
        outs["slc_s"].append(slc_s.reshape(db, tn, 2, B_KV_HEADS, HEAD_DIM))
        win_all = jnp.concatenate([state_win[l], win_s.reshape(db, tn, 2, B_KV_HEADS, HEAD_DIM)], axis=1)
        outs["win_s"].append(win_all[:, tn:])
        outs["chv_s"].append(v_s.reshape(db, tn, d_a))

        i = l // 2
        if l % 2 == 0:
            wg, wu, wd = ffn_w_gate[i].astype(BF), ffn_w_up[i].astype(BF), ffn_w_down[i].astype(BF)
            yp = _ffn_dense(h2p, yp, mod_p[5], wg, wu, wd, per_batch=True, tokens_per_batch=t)
            ys = _ffn_dense(h2s, ys, mod_s[5], wg, wu, wd, per_batch=False, tokens_per_batch=1)
        else:
            wg, wu, wd = moe_w_gate[i].astype(BF), moe_w_up[i].astype(BF), moe_w_down[i].astype(BF)
            n_exp = router_w.shape[2]
            rw = jnp.concatenate([router_w[i], jnp.zeros((d, LANES - n_exp), F32)], axis=1)
            rw_hi = rw.astype(BF)
            rw_lo = (rw - rw_hi.astype(F32)).astype(BF)
            rb = jnp.concatenate([router_b[i], jnp.zeros((LANES - n_exp,), F32)])[None, :]
            yp = _moe_ffn(h2p, yp, mod_p[5], rw_hi, rw_lo, rb, wg, wu, wd, per_batch=True, tokens_per_batch=t)
            ys = _moe_ffn(h2s, ys, mod_s[5], rw_hi, rw_lo, rb, wg, wu, wd, per_batch=False, tokens_per_batch=1)

    st = lambda k: jnp.stack(outs[k])
    return (yp.reshape(batch, t, d), ys.reshape(db, tn, d), st("cmp_p"), st("cmp_s"), st("slc_p"), st("slc_s"),
            st("win_p"), st("win_s"), st("chv_s"))
```

```python
import functools

import numpy as np
import jax
import jax.numpy as jnp
from jax import lax
from jax.experimental import pallas as pl
from jax.experimental.pallas import tpu as pltpu

F32 = jnp.float32
BF = jnp.bfloat16

HEAD_DIM = 64
B_KV_HEADS = 2
GQA_R = 4
N_BRANCH = 3
CHUNK = 128
CMP_BLOCK = 32
CMP_STRIDE = 16
SLC_BLOCK = 64
N_SELECT = 16
WINDOW = 512
Q_BLOCK = 128
ROPE_THETA = 10000.0
TOP_K = 2
MOE_BLOCK = 128
NEG_INF = -1e30
FORCE_SCORE = 1e30
EPS = 1e-6
LOG2_E = 1.4426950408889634

LANES = 128
TOKEN_TILE_ROWS = 8
KEY_TILE = 512
VMEM_LIMIT = 56 * 1024 * 1024


def _cparams(sem):
    return pltpu.CompilerParams(dimension_semantics=sem, vmem_limit_bytes=VMEM_LIMIT)


def _dot(a, b):
    return jnp.dot(a, b, preferred_element_type=F32)


def _dot_nt(a, b):
    return lax.dot_general(a, b, (((1,), (1,)), ((), ())), preferred_element_type=F32)


def _split_dot(x, w):
    hi = x.astype(BF)
    lo = (x - hi.astype(F32)).astype(BF)
    return _dot(hi, w) + _dot(lo, w)


def _ada_kernel(c_ref, w_ref, b_ref, o_ref):
    c = c_ref[...]
    s = c * jax.nn.sigmoid(c)
    o_ref[0] = _dot(s.astype(BF), w_ref[0].astype(BF)) + b_ref[0]


def _adaln(c_all, w_ada, b_ada):
    depth, d, n6 = w_ada.shape
    m = c_all.shape[0]
    tn = 1536
    return pl.pallas_call(
        _ada_kernel,
        grid=(depth, n6 // tn),
        in_specs=[pl.BlockSpec((m, d), lambda l, j: (0, 0)),
                  pl.BlockSpec((1, d, tn), lambda l, j: (l, 0, j)),
                  pl.BlockSpec((1, 1, tn), lambda l, j: (l, 0, j))],
        out_specs=pl.BlockSpec((1, m, tn), lambda l, j: (l, 0, j)),
        out_shape=jax.ShapeDtypeStruct((depth, m, n6), F32),
        compiler_params=_cparams(("arbitrary", "arbitrary")),
        name="adaln",
    )(c_all, w_ada, b_ada.reshape(depth, 1, n6))


def _pre_kernel(x_ref, g_ref, sh_ref, sc_ref, w_ref, cos_ref, sin_ref, bd_ref, qg_ref, kg_ref,
                ws_ref, bs_ref, *out_refs, chunked, d_a, d_b):
    if chunked:
        a_ref, q_ref, cmp_ref, slc_ref, win_ref, slcb_ref, winb_ref, gate_ref = out_refs
    else:
        a_ref, q_ref, cmp_ref, slc_ref, win_ref, gate_ref, v_ref = out_refs
    tm = x_ref.shape[0]
    x = x_ref[...]
    ms = jnp.mean(x * x, axis=-1, keepdims=True)
    h = x * lax.rsqrt(ms + EPS) * g_ref[...]
    h = h * (1.0 + sc_ref[...]) + sh_ref[...]
    z = _dot(h.astype(BF), w_ref[...])
    uv = jax.nn.gelu(z[:, :2 * d_a])
    u = uv[:, :d_a]
    v = uv[:, d_a:]
    bd = bd_ref[...]
    mu = _split_dot(v, bd)
    vc = v - mu
    vn = vc * lax.rsqrt(_split_dot(vc * vc, bd) + EPS)

    cos = cos_ref[...]
    sin = sin_ref[...]
    lane = lax.broadcasted_iota(jnp.int32, (1, LANES), 1)
    first_half = (lane % HEAD_DIM) < (HEAD_DIM // 2)

    def rope(t):
        swapped = jnp.where(first_half, pltpu.roll(t, LANES - HEAD_DIM // 2, 1),
                            pltpu.roll(t, HEAD_DIM // 2, 1))
        return t * cos + swapped * sin

    o = 2 * d_a
    zq = z[:, o:o + d_b]
    qn = zq * lax.rsqrt(_split_dot(zq * zq, bd) + EPS) * qg_ref[...]
    for c in range(d_b // LANES):
        q_ref[:, c * LANES:(c + 1) * LANES] = rope(qn[:, c * LANES:(c + 1) * LANES]) * (HEAD_DIM ** -0.5)
    o += d_b
    bd1 = bd_ref[0:LANES, 0:LANES]
    row_refs = (cmp_ref, slc_ref, win_ref)
    bf_refs = (None, slcb_ref, winb_ref) if chunked else (None, None, None)
    for br in range(N_BRANCH):
        kp = z[:, o:o + LANES]
        vp = z[:, o + LANES:o + 2 * LANES]
        kn = kp * lax.rsqrt(_split_dot(kp * kp, bd1) + EPS) * kg_ref[br:br + 1, :]
        kr = rope(kn)
        row_refs[br][:, 0:LANES] = kr
        row_refs[br][:, LANES:2 * LANES] = vp
        if bf_refs[br] is not None:
            bf_refs[br][:, 0:LANES] = kr.astype(BF)
            bf_refs[br][:, LANES:2 * LANES] = vp.astype(BF)
        o += 2 * LANES
    gate_ref[...] = jax.nn.sigmoid(z[:, o:o + LANES])

    if chunked:
        low = lane < HEAD_DIM
        for c in range(tm // CHUNK):
            rs = slice(c * CHUNK, (c + 1) * CHUNK)
            for j in range(d_a // LANES):
                cs = slice(j * LANES, (j + 1) * LANES)
                vpair = vn[rs, cs].astype(BF)
                m0 = _dot(ws_ref[2 * j], vpair)
                m1 = _dot(ws_ref[2 * j + 1], vpair)
                mixed = jnp.where(low, m0, m1) + bs_ref[:, cs]
                a_ref[rs, cs] = u[rs, cs] * mixed
    else:
        a_ref[...] = u * (vn * ws_ref[...] + bs_ref[...])
        v_ref[...] = vn


def _pre_mix(x, norm_g, shift, scale, w_in_bf, cos, sin, bd, qg, kg, ws, bs, *, chunked, tokens_per_batch):
    n, d = x.shape
    d_a = d_b = d // 2
    tm = 256 if chunked else n
    nt = n // tm
    pw = w_in_bf.shape[1]
    full = lambda shp: pl.BlockSpec(shp, lambda i: (0,) * len(shp))
    row = lambda w: pl.BlockSpec((tm, w), lambda i: (i, 0))
    if chunked:
        tpb = tokens_per_batch // tm
        mod_spec = pl.BlockSpec((None, 1, d), lambda i: (i // tpb, 0, 0))
        rot_spec = pl.BlockSpec((tm, LANES), lambda i: (i % tpb, 0))
        ws_spec, bs_spec = full(ws.shape), full(bs.shape)
        out_shape = [jax.ShapeDtypeStruct((n, d_a), F32), jax.ShapeDtypeStruct((n, d_b), F32),
                     jax.ShapeDtypeStruct((n, 2 * LANES), F32), jax.ShapeDtypeStruct((n, 2 * LANES), F32),
                     jax.ShapeDtypeStruct((n, 2 * LANES), F32), jax.ShapeDtypeStruct((n, 2 * LANES), BF),
                     jax.ShapeDtypeStruct((n, 2 * LANES), BF), jax.ShapeDtypeStruct((n, LANES), F32)]
        out_specs = [row(d_a), row(d_b), row(2 * LANES), row(2 * LANES), row(2 * LANES), row(2 * LANES),
                     row(2 * LANES), row(LANES)]
    else:
        mod_spec = row(d)
        rot_spec = full((1, LANES))
        ws_spec, bs_spec = full(ws.shape), full(bs.shape)
        out_shape = [jax.ShapeDtypeStruct((n, d_a), F32), jax.ShapeDtypeStruct((n, d_b), F32),
                     jax.ShapeDtypeStruct((n, 2 * LANES), F32), jax.ShapeDtypeStruct((n, 2 * LANES), F32),
                     jax.ShapeDtypeStruct((n, 2 * LANES), F32), jax.ShapeDtypeStruct((n, LANES), F32),
                     jax.ShapeDtypeStruct((n, d_a), F32)]
        out_specs = [row(d_a), row(d_b), row(2 * LANES), row(2 * LANES), row(2 * LANES), row(LANES), row(d_a)]
    return pl.pallas_call(
        functools.partial(_pre_kernel, chunked=chunked, d_a=d_a, d_b=d_b),
        grid=(nt,),
        in_specs=[row(d), full((1, d)), mod_spec, mod_spec, full((d, pw)), rot_spec, rot_spec,
                  full(bd.shape), full(qg.shape), full(kg.shape), ws_spec, bs_spec],
        out_specs=out_specs,
        out_shape=out_shape,
        compiler_params=_cparams(("arbitrary",)),
        name="pre_mix_prompt" if chunked else "pre_mix_sample",
    )(x, norm_g, shift, scale, w_in_bf, cos, sin, bd, qg, kg, ws, bs)


def _compress_pieces(load_piece_rows, w_ref, pe_ref, kv, n_piece):
    assert CMP_BLOCK == 2 * CMP_STRIDE
    acc_a = jnp.zeros((n_piece, LANES), F32)
    acc_b = jnp.zeros((n_piece, LANES), F32)
    for l in range(CMP_STRIDE):
        xl = load_piece_rows(l)
        l2 = CMP_STRIDE + l
        acc_a = acc_a + _dot((xl + pe_ref[kv, l:l + 1, :]).astype(BF), w_ref[kv, l])
        acc_b = acc_b + _dot((xl + pe_ref[kv, l2:l2 + 1, :]).astype(BF), w_ref[kv, l2])
    shifted = pltpu.roll(acc_b, n_piece - 1, 0)
    rows = lax.broadcasted_iota(jnp.int32, (n_piece, 1), 0)
    return jnp.where(rows < n_piece - 1, acc_a + shifted, 0.0)


def _compress_kernel(xk_ref, xv_ref, w_ref, pe_ref, kc_ref, vc_ref):
    n_piece = kc_ref.shape[0]
    for kv, (x_ref, o_ref) in enumerate(((xk_ref, kc_ref), (xv_ref, vc_ref))):
        out = _compress_pieces(lambda l: x_ref[pl.ds(l, n_piece, stride=CMP_STRIDE), :], w_ref, pe_ref, kv, n_piece)
        o_ref[...] = out.astype(BF)


def _compress_prompt(cmp_rows, wbig, pebig, batch):
    n = cmp_rows.shape[0]
    t = n // batch
    n_piece = t // CMP_STRIDE
    out = jax.ShapeDtypeStruct((batch * n_piece, LANES), BF)
    return pl.pallas_call(
        _compress_kernel,
        grid=(batch,),
        in_specs=[pl.BlockSpec((t, LANES), lambda b: (b, 0)),
                  pl.BlockSpec((t, LANES), lambda b: (b, 1)),
                  pl.BlockSpec(wbig.shape, lambda b: (0, 0, 0, 0)),
                  pl.BlockSpec(pebig.shape, lambda b: (0, 0, 0))],
        out_specs=[pl.BlockSpec((n_piece, LANES), lambda b: (b, 0))] * 2,
        out_shape=[out, out],
        compiler_params=_cparams(("arbitrary",)),
        name="compress_prompt",
    )(cmp_rows, cmp_rows, wbig, pebig)


def _group_queries(q, g):
    m = q.shape[0]
    low = lax.broadcasted_iota(jnp.int32, (m, LANES), 1) < HEAD_DIM
    keep = low if g == 0 else jnp.logical_not(low)
    parts = []
    for r in range(GQA_R):
        h = GQA_R * g + r
        t = q[:, (h // 2) * LANES:(h // 2 + 1) * LANES]
        if h % 2 != g:
            t = pltpu.roll(t, HEAD_DIM, 1)
        parts.append(jnp.where(keep, t, 0.0))
    return parts


def _top_select(imp, blk_f):
    sel = jnp.zeros(imp.shape, F32)
    work = imp
    for _ in range(N_SELECT):
        mx = jnp.max(work, axis=-1, keepdims=True)
        first = jnp.min(jnp.where(work == mx, blk_f, float(LANES)), axis=-1, keepdims=True)
        hit = blk_f == first
        sel = jnp.where(hit, 1.0, sel)
        work = jnp.where(hit, -jnp.inf, work)
    return sel


def _block_scores(imp, qpos):
    blk = lax.broadcasted_iota(jnp.int32, imp.shape, 1)
    cur = qpos // SLC_BLOCK
    forced = (blk == 0) | (blk == cur) | (blk == cur - 1)
    valid = blk <= cur
    imp = jnp.where(forced, FORCE_SCORE, imp)
    imp = jnp.where(valid, imp, NEG_INF)
    return imp, blk.astype(F32), valid


def _merge_heads(res, m):
    low = lax.broadcasted_iota(jnp.int32, (m, LANES), 1) < HEAD_DIM
    cols = []
    for c in range(4):
        a, b = res[2 * c], res[2 * c + 1]
        if c // 2 == 0:
            cols.append(jnp.where(low, a, pltpu.roll(b, HEAD_DIM, 1)))
        else:
            cols.append(jnp.where(low, pltpu.roll(a, HEAD_DIM, 1), b))
    return jnp.concatenate(cols, axis=1)


def _nsa_prompt_kernel(q_ref, gate_ref, kc_ref, vc_ref, slc_ref, win_ref, cover_ref, e_ref, o_ref):
    i = pl.program_id(1)
    s0 = i * Q_BLOCK
    n_cmp = kc_ref.shape[0]
    rows4 = GQA_R * Q_BLOCK
    q = q_ref[...] * LOG2_E
    gates = gate_ref[...]
    tpos = s0 + lax.broadcasted_iota(jnp.int32, (Q_BLOCK, 1), 0)
    tpos4 = jnp.concatenate([tpos] * GQA_R, axis=0)
    tile4 = lambda x: jnp.concatenate([x] * GQA_R, axis=0)
    groups = range(B_KV_HEADS)
    qg = [jnp.concatenate(_group_queries(q, g), axis=0).astype(BF) for g in groups]

    kc = kc_ref[...]
    vc = vc_ref[...]
    c_last = lax.broadcasted_iota(jnp.int32, (1, n_cmp), 1) * CMP_STRIDE + (CMP_BLOCK - 1)
    c_bias = jnp.where(c_last <= tpos4, 0.0, NEG_INF)
    has_cmp = jnp.where(tpos4 >= CMP_BLOCK - 1, 1.0, 0.0)
    o_cmp, p_sums = [], []
    for g in groups:
        s = _dot_nt(qg[g], kc) + c_bias
        e = jnp.exp2(s - jnp.max(s, axis=-1, keepdims=True))
        p = e * (has_cmp / jnp.sum(e, axis=-1, keepdims=True))
        o_cmp.append(_dot(p.astype(BF), vc))
        p_sum = p[0:Q_BLOCK]
        for r in range(1, GQA_R):
            p_sum = p_sum + p[r * Q_BLOCK:(r + 1) * Q_BLOCK]
        p_sums.append(p_sum)

    imp = _split_dot(jnp.concatenate(p_sums, axis=0), cover_ref[...])
    imp, blk_f, valid = _block_scores(imp, jnp.concatenate([tpos] * B_KV_HEADS, axis=0))
    sel_all = jnp.where(valid, _top_select(imp, blk_f), 0.0).astype(BF)
    sel_b = [sel_all[g * Q_BLOCK:(g + 1) * Q_BLOCK] for g in groups]

    w_start = pl.multiple_of(jnp.maximum(s0 - WINDOW, 0), Q_BLOCK)
    w_len = WINDOW + Q_BLOCK
    wpos = w_start + lax.broadcasted_iota(jnp.int32, (1, w_len), 1)
    w_bias4 = tile4(jnp.where((wpos <= tpos) & (wpos > tpos - WINDOW), 0.0, NEG_INF))
    kw = win_ref[pl.ds(w_start, w_len), 0:LANES]
    vw = win_ref[pl.ds(w_start, w_len), LANES:2 * LANES]
    low_w = lax.broadcasted_iota(jnp.int32, (1, LANES), 1) < HEAD_DIM
    keep = [low_w, jnp.logical_not(low_w)]
    with_ones = lambda v, g: jnp.where(keep[g], v, jnp.ones_like(v))
    den_lane = [HEAD_DIM, 0]

    def normalise(acc, g):
        return acc / acc[:, den_lane[g]:den_lane[g] + 1]

    o_win = []
    for g in groups:
        sw = _dot_nt(qg[g], kw) + w_bias4
        ew = jnp.exp2(sw - jnp.max(sw, axis=-1, keepdims=True))
        o_win.append(_dot(ew.astype(BF), vw) / jnp.sum(ew, axis=-1, keepdims=True))

    def body(kt, carry):
        k0 = pl.multiple_of(kt * KEY_TILE, KEY_TILE)
        kk = slc_ref[pl.ds(k0, KEY_TILE), 0:LANES]
        vv = slc_ref[pl.ds(k0, KEY_TILE), LANES:2 * LANES]
        et = e_ref[:, pl.ds(k0, KEY_TILE)]
        kpos = k0 + lax.broadcasted_iota(jnp.int32, (1, KEY_TILE), 1)
        causal = kpos <= tpos
        out = []
        for g in groups:
            m, acc = carry[g]
            bias = jnp.where((_dot(sel_b[g], et) > 0.5) & causal, 0.0, NEG_INF)
            sc = _dot_nt(qg[g], kk) + tile4(bias)
            m_new = jnp.maximum(m, jnp.max(sc, axis=-1, keepdims=True))
            pe = jnp.exp2(sc - m_new)
            acc = jnp.exp2(m - m_new) * acc + _dot(pe.astype(BF), with_ones(vv, g))
            out.append((m_new, acc))
        return tuple(out)

    init = tuple((jnp.full((rows4, 1), NEG_INF, F32), jnp.zeros((rows4, LANES), F32)) for _ in groups)
    n_tiles = (s0 + Q_BLOCK + KEY_TILE - 1) // KEY_TILE
    fin = lax.fori_loop(0, n_tiles, body, init)

    res = []
    for g in groups:
        o_slc = normalise(fin[g][1], g)
        for r in range(GQA_R):
            h = GQA_R * g + r
            rs = slice(r * Q_BLOCK, (r + 1) * Q_BLOCK)
            res.append(gates[:, 3 * h:3 * h + 1] * o_cmp[g][rs] + gates[:, 3 * h + 1:3 * h + 2] * o_slc[rs]
                       + gates[:, 3 * h + 2:3 * h + 3] * o_win[g][rs])
    o_ref[...] = _merge_heads(res, Q_BLOCK)


def _nsa_prompt(q, gates, kc, vc, slc_bf, win_bf, cover, expand, batch):
    n, d_b = q.shape
    t = n // batch
    nq = t // Q_BLOCK
    n_cmp = kc.shape[0] // batch
    return pl.pallas_call(
        _nsa_prompt_kernel,
        grid=(batch, nq),
        in_specs=[pl.BlockSpec((Q_BLOCK, d_b), lambda b, i: (b * nq + i, 0)),
                  pl.BlockSpec((Q_BLOCK, LANES), lambda b, i: (b * nq + i, 0)),
                  pl.BlockSpec((n_cmp, LANES), lambda b, i: (b, 0)),
                  pl.BlockSpec((n_cmp, LANES), lambda b, i: (b, 0)),
                  pl.BlockSpec((t, 2 * LANES), lambda b, i: (b, 0)),
                  pl.BlockSpec((t, 2 * LANES), lambda b, i: (b, 0)),
                  pl.BlockSpec(cover.shape, lambda b, i: (0, 0)),
                  pl.BlockSpec(expand.shape, lambda b, i: (0, 0))],
        out_specs=pl.BlockSpec((Q_BLOCK, d_b), lambda b, i: (b * nq + i, 0)),
        out_shape=jax.ShapeDtypeStruct((n, d_b), F32),
        compiler_params=_cparams(("arbitrary", "arbitrary")),
        name="nsa_prompt",
    )(q, gates, kc, vc, slc_bf, win_bf, cover, expand)


def _sample_queries(q_row):
    sub = lax.broadcasted_iota(jnp.int32, (8, LANES), 0)
    out = jnp.zeros((8, LANES), F32)
    for g in range(B_KV_HEADS):
        for r, part in enumerate(_group_queries(q_row, g)):
            out = jnp.where(sub == GQA_R * g + r, jnp.broadcast_to(part, (8, LANES)), out)
    return out


def _s_cmp_kernel(pt_ref, q_ref, *refs, n_pages, qpos):
    k_pages = refs[:n_pages]
    v_pages = refs[n_pages:2 * n_pages]
    wbig_ref, pe_ref, cover_ref, ocmp_ref, imp_ref = refs[2 * n_pages:]
    per_page = k_pages[0].shape[0] // CMP_STRIDE
    n_piece = n_pages * per_page

    def loader(pages):
        return lambda l: jnp.concatenate([pg[pl.ds(l, per_page, stride=CMP_STRIDE), :] for pg in pages], axis=0)

    kc = _compress_pieces(loader(k_pages), wbig_ref, pe_ref, 0, n_piece).astype(BF)
    vc = _compress_pieces(loader(v_pages), wbig_ref, pe_ref, 1, n_piece).astype(BF)
    qa = _sample_queries(q_ref[...])
    c_last = lax.broadcasted_iota(jnp.int32, (1, n_piece), 1) * CMP_STRIDE + (CMP_BLOCK - 1)
    cv = c_last <= qpos
    s = jnp.where(cv, _dot_nt(qa.astype(BF), kc), NEG_INF)
    e = jnp.where(cv, jnp.exp(s - jnp.max(s, axis=-1, keepdims=True)), 0.0)
    p = e / jnp.maximum(jnp.sum(e, axis=-1, keepdims=True), 1e-30)
    ocmp_ref[...] = _dot(p.astype(BF), vc)
    sub = lax.broadcasted_iota(jnp.int32, (8, n_piece), 0)
    ps0 = jnp.sum(jnp.where(sub < GQA_R, p, 0.0), axis=0, keepdims=True)
    ps1 = jnp.sum(jnp.where(sub >= GQA_R, p, 0.0), axis=0, keepdims=True)
    p_sum = jnp.where(sub == 0, ps0, jnp.where(sub == 1, ps1, 0.0))
    imp_ref[...] = _split_dot(p_sum, cover_ref[...])


def _s_cmp(page_table, q3, cache_l, layer, wbig, pebig, cover, qpos):
    db = q3.shape[0]
    n_pages = page_table.shape[1]
    page = cache_l.shape[2]
    n_piece = n_pages * page // CMP_STRIDE
    page_specs = [pl.BlockSpec((None, None, page, LANES),
                               functools.partial(lambda b, pt, j, kv: (layer, pt[b, j], 0, kv), j=j, kv=kv))
                  for kv in range(2) for j in range(n_pages)]
    gs = pltpu.PrefetchScalarGridSpec(
        num_scalar_prefetch=1,
        grid=(db,),
        in_specs=[pl.BlockSpec((None, 1, q3.shape[2]), lambda b, pt: (b, 0, 0))] + page_specs + [
            pl.BlockSpec(wbig.shape, lambda b, pt: (0, 0, 0, 0)),
            pl.BlockSpec(pebig.shape, lambda b, pt: (0, 0, 0)),
            pl.BlockSpec(cover.shape, lambda b, pt: (0, 0))],
        out_specs=[pl.BlockSpec((None, 8, LANES), lambda b, pt: (b, 0, 0)),
                   pl.BlockSpec((None, 8, LANES), lambda b, pt: (b, 0, 0))],
    )
    assert cover.shape[0] == n_piece
    return pl.pallas_call(
        functools.partial(_s_cmp_kernel, n_pages=n_pages, qpos=qpos),
        grid_spec=gs,
        out_shape=[jax.ShapeDtypeStruct((db, 8, LANES), F32), jax.ShapeDtypeStruct((db, 8, LANES), F32)],
        compiler_params=_cparams(("arbitrary",)),
        name="nsa_sample_cmp",
    )(page_table, q3, *([cache_l] * (2 * n_pages)), wbig, pebig, cover)


def _s_select_kernel(imp_ref, e_ref, o_ref, *, qpos):
    imp = imp_ref[...]
    pos = jnp.full((imp.shape[0], 1), qpos, jnp.int32)
    imp, blk_f, valid = _block_scores(imp, pos)
    sel = _top_select(imp, blk_f)
    sel_b = jnp.where(valid, sel, 0.0).astype(BF)
    o_ref[...] = _dot(sel_b, e_ref[...])


def _s_select(imp2, expand, qpos):
    m = imp2.shape[0]
    past = expand.shape[1]
    return pl.pallas_call(
        functools.partial(_s_select_kernel, qpos=qpos),
        grid=(1,),
        in_specs=[pl.BlockSpec(imp2.shape, lambda i: (0, 0)), pl.BlockSpec(expand.shape, lambda i: (0, 0))],
        out_specs=pl.BlockSpec((m, past), lambda i: (0, 0)),
        out_shape=jax.ShapeDtypeStruct((m, past), F32),
        compiler_params=_cparams(("arbitrary",)),
        name="nsa_sample_select",
    )(imp2, expand)


def _s_att_kernel(pt_ref, q_ref, gate_ref, mask_ref, ocmp_ref, slcn_ref, winn_ref, win_ref, *refs,
                  n_pages, qpos, past):
    pages = refs[:n_pages]
    o_ref = refs[n_pages]
    page = pages[0].shape[1]
    qa = _sample_queries(q_ref[...])
    qb = qa.astype(BF)
    sub = lax.broadcasted_iota(jnp.int32, (8, 1), 0)
    mrow = jnp.where(sub < GQA_R, mask_ref[0:1, :], mask_ref[1:2, :])
    bias = jnp.where(mrow > 0.5, 0.0, NEG_INF)
    sc = jnp.concatenate([_dot(qb, pg[0:LANES, :].astype(BF)) for pg in pages], axis=1) + bias
    kn = slcn_ref[:, 0:LANES]
    vn = slcn_ref[:, LANES:2 * LANES]
    s_new = jnp.sum(qa * kn, axis=-1, keepdims=True)
    m = jnp.maximum(jnp.max(sc, axis=-1, keepdims=True), s_new)
    e = jnp.exp(sc - m)
    e_new = jnp.exp(s_new - m)
    acc = e_new * vn
    for j, pg in enumerate(pages):
        acc = acc + _dot_nt(e[:, j * page:(j + 1) * page].astype(BF), pg[LANES:2 * LANES, :].astype(BF))
    o_slc = acc / (jnp.sum(e, axis=-1, keepdims=True) + e_new)
    wb = win_ref.shape[1]
    wpos = (past - wb) + lax.broadcasted_iota(jnp.int32, (1, wb), 1)
    wvalid = (wpos <= qpos) & (wpos > qpos - WINDOW) & (wpos >= 0)
    sw = _dot(qb, win_ref[0:LANES, :].astype(BF)) + jnp.where(wvalid, 0.0, NEG_INF)
    kwn = winn_ref[:, 0:LANES]
    vwn = winn_ref[:, LANES:2 * LANES]
    sw_new = jnp.sum(qa * kwn, axis=-1, keepdims=True)
    mw = jnp.maximum(jnp.max(sw, axis=-1, keepdims=True), sw_new)
    ew = jnp.exp(sw - mw)
    ew_new = jnp.exp(sw_new - mw)
    o_win = (_dot_nt(ew.astype(BF), win_ref[LANES:2 * LANES, :].astype(BF)) + ew_new * vwn) / (
        jnp.sum(ew, axis=-1, keepdims=True) + ew_new)
    lane = lax.broadcasted_iota(jnp.int32, (8, LANES), 1)
    gt = jnp.broadcast_to(gate_ref[...], (8, LANES))
    gsel = [jnp.sum(jnp.where(lane == 3 * sub + br, gt, 0.0), axis=-1, keepdims=True) for br in range(N_BRANCH)]
    res = gsel[0] * ocmp_ref[...] + gsel[1] * o_slc + gsel[2] * o_win
    o_ref[...] = _merge_heads([res[h:h + 1, :] for h in range(8)], 1)


def _s_att(page_table, q3, gates3, mask3, ocmp, slc_new3, win_new3, win_l, cache_l, layer, qpos, past):
    db = q3.shape[0]
    n_pages = page_table.shape[1]
    page = cache_l.shape[3]
    wb = win_l.shape[3]
    page_specs = [pl.BlockSpec((None, None, 2 * LANES, page),
                               functools.partial(lambda b, pt, j: (layer, pt[b, j], 0, 0), j=j))
                  for j in range(n_pages)]
    per_b = lambda r, w: pl.BlockSpec((None, r, w), lambda b, pt: (b, 0, 0))
    gs = pltpu.PrefetchScalarGridSpec(
        num_scalar_prefetch=1,
        grid=(db,),
        in_specs=[per_b(1, q3.shape[2]), per_b(1, LANES), per_b(2, past), per_b(8, LANES),
                  per_b(1, 2 * LANES), per_b(1, 2 * LANES),
                  pl.BlockSpec((None, None, 2 * LANES, wb), lambda b, pt: (layer, b, 0, 0))] + page_specs,
        out_specs=per_b(1, q3.shape[2]),
    )
    return pl.pallas_call(
        functools.partial(_s_att_kernel, n_pages=n_pages, qpos=qpos, past=past),
        grid_spec=gs,
        out_shape=jax.ShapeDtypeStruct(q3.shape, F32),
        compiler_params=_cparams(("arbitrary",)),
        name="nsa_sample_att",
    )(page_table, q3, gates3, mask3, ocmp, slc_new3, win_new3, win_l, *([cache_l] * n_pages))


def _untile(ref, tm):
    return jnp.concatenate([ref[pl.ds(c, tm, stride=TOKEN_TILE_ROWS), :] for c in range(TOKEN_TILE_ROWS)], axis=1)


def _store_tiled(ref, val):
    tm = val.shape[0]
    for c in range(TOKEN_TILE_ROWS):
        ref[pl.ds(c, tm, stride=TOKEN_TILE_ROWS), :] = val[:, c * LANES:(c + 1) * LANES]


def _post_kernel(a_ref, o_ref, x_ref, g1_ref, og_ref, wout_ref, bd_ref, ng_ref, sh2_ref, sc2_ref, y_ref, h2_ref,
                 *, tiled_h2):
    bd = bd_ref[...]
    d_a = a_ref.shape[1]

    def head_norm(t, g):
        return t * lax.rsqrt(_split_dot(t * t, bd) + EPS) * g

    a = head_norm(a_ref[...], og_ref[:, :d_a])
    o = head_norm(o_ref[...], og_ref[:, d_a:])
    mix = _dot(a.astype(BF), wout_ref[:d_a, :]) + _dot(o.astype(BF), wout_ref[d_a:, :])
    y = x_ref[...] + g1_ref[...] * mix
    y_ref[...] = y
    ms = jnp.mean(y * y, axis=-1, keepdims=True)
    h2 = y * lax.rsqrt(ms + EPS) * ng_ref[...]
    h2 = h2 * (1.0 + sc2_ref[...]) + sh2_ref[...]
    if tiled_h2:
        _store_tiled(h2_ref, h2)
    else:
        h2_ref[...] = h2


def _mod_spec(per_batch, tm, d, tokens_per_batch):
    if per_batch:
        tpb = tokens_per_batch // tm
        return pl.BlockSpec((None, 1, d), lambda i: (i // tpb, 0, 0))
    return pl.BlockSpec((tm, d), lambda i: (i, 0))


def _post_mix(a, o, x, gate1, out_g, wout_bf, bd, ffn_g, shift2, scale2, *, per_batch, tokens_per_batch, tiled_h2):
    n, d = x.shape
    tm = 512 if per_batch else n
    full = lambda shp: pl.BlockSpec(shp, lambda i: (0,) * len(shp))
    row = lambda w: pl.BlockSpec((tm, w), lambda i: (i, 0))
    ms = _mod_spec(per_batch, tm, d, tokens_per_batch)
    if tiled_h2:
        assert d == TOKEN_TILE_ROWS * LANES
        h2_spec = pl.BlockSpec((tm * TOKEN_TILE_ROWS, LANES), lambda i: (i, 0))
        h2_shape = jax.ShapeDtypeStruct((n * TOKEN_TILE_ROWS, LANES), F32)
    else:
        h2_spec, h2_shape = row(d), jax.ShapeDtypeStruct((n, d), F32)
    return pl.pallas_call(
        functools.partial(_post_kernel, tiled_h2=tiled_h2),
        grid=(n // tm,),
        in_specs=[row(a.shape[1]), row(o.shape[1]), row(d), ms, full((1, d)), full(wout_bf.shape), full(bd.shape),
                  full((1, d)), ms, ms],
        out_specs=[row(d), h2_spec],
        out_shape=[jax.ShapeDtypeStruct((n, d), F32), h2_shape],
        compiler_params=_cparams(("arbitrary",)),
        name="post_mix",
    )(a, o, x, gate1, out_g, wout_bf, bd, ffn_g, shift2, scale2)


def _ffn_kernel(h_ref, y_ref, g2_ref, wg_ref, wu_ref, wd_ref, o_ref, *, f_tile):
    hb = h_ref[...].astype(BF)
    d_ff = wg_ref.shape[1]
    acc = jnp.zeros(o_ref.shape, F32)
    for f in range(d_ff // f_tile):
        fs = slice(f * f_tile, (f + 1) * f_tile)
        gp = _dot(hb, wg_ref[:, fs])
        up = _dot(hb, wu_ref[:, fs])
        act = gp * jax.nn.sigmoid(gp) * up
        acc = acc + _dot(act.astype(BF), wd_ref[fs, :])
    o_ref[...] = y_ref[...] + g2_ref[...] * acc


def _ff_tile(d_ff):
    for cand in (1408, 1024, 512, 256, 128):
        if d_ff % cand == 0:
            return cand
    return d_ff


def _ffn_dense(h2, y, gate2, wg, wu, wd, *, per_batch, tokens_per_batch):
    n, d = y.shape
    tm = 512 if per_batch else n
    full = lambda shp: pl.BlockSpec(shp, lambda i: (0,) * len(shp))
    row = lambda w: pl.BlockSpec((tm, w), lambda i: (i, 0))
    return pl.pallas_call(
        functools.partial(_ffn_kernel, f_tile=_ff_tile(wg.shape[1])),
        grid=(n // tm,),
        in_specs=[row(d), row(d), _mod_spec(per_batch, tm, d, tokens_per_batch), full(wg.shape), full(wu.shape),
                  full(wd.shape)],
        out_specs=row(d),
        out_shape=jax.ShapeDtypeStruct((n, d), F32),
        compiler_params=_cparams(("arbitrary",)),
        name="ffn_dense",
    )(h2, y, gate2, wg, wu, wd)


def _router_kernel(h_ref, rwh_ref, rwl_ref, rb_ref, o_ref, *, n_experts):
    h = _untile(h_ref, o_ref.shape[0])
    hi = h.astype(BF)
    lo = (h - hi.astype(F32)).astype(BF)
    logits = _dot(hi, rwh_ref[...]) + _dot(lo, rwh_ref[...]) + _dot(hi, rwl_ref[...]) + rb_ref[...]
    lane = lax.broadcasted_iota(jnp.int32, logits.shape, 1)
    lane_f = lane.astype(F32)
    l1 = jnp.where(lane < n_experts, logits, -jnp.inf)
    m1 = jnp.max(l1, axis=-1, keepdims=True)
    i1 = jnp.min(jnp.where(l1 == m1, lane_f, float(LANES)), axis=-1, keepdims=True)
    l2 = jnp.where(lane_f == i1, -jnp.inf, l1)
    m2 = jnp.max(l2, axis=-1, keepdims=True)
    i2 = jnp.min(jnp.where(l2 == m2, lane_f, float(LANES)), axis=-1, keepdims=True)
    e2 = jnp.exp(m2 - m1)
    den = 1.0 + e2
    o_ref[...] = jnp.where(lane == 0, i1, jnp.where(lane == 1, i2, jnp.where(lane == 2, 1.0 / den,
                           jnp.where(lane == 3, e2 / den, 0.0))))


def _router(h2t, rw_hi, rw_lo, rb, n_experts):
    n = h2t.shape[0] // TOKEN_TILE_ROWS
    tm = 512 if n % 512 == 0 else n
    full = lambda shp: pl.BlockSpec(shp, lambda i: (0,) * len(shp))
    return pl.pallas_call(
        functools.partial(_router_kernel, n_experts=n_experts),
        grid=(n // tm,),
        in_specs=[pl.BlockSpec((tm * TOKEN_TILE_ROWS, LANES), lambda i: (i, 0)), full(rw_hi.shape),
                  full(rw_lo.shape), full(rb.shape)],
        out_specs=pl.BlockSpec((tm, LANES), lambda i: (i, 0)),
        out_shape=jax.ShapeDtypeStruct((n, LANES), F32),
        compiler_params=_cparams(("arbitrary",)),
        name="moe_router",
    )(h2t, rw_hi, rw_lo, rb)


def _gather_kernel(idx_ref, src_ref, out_ref, sem, *, rows_per_step):
    base = pl.program_id(0) * rows_per_step

    def row_copy(j):
        return pltpu.make_async_copy(src_ref.at[idx_ref[base + j]], out_ref.at[j], sem)

    def start(j, c):
        row_copy(j).start()
        return c

    def wait(j, c):
        row_copy(j).wait()
        return c

    lax.fori_loop(0, rows_per_step, start, 0, unroll=8)
    lax.fori_loop(0, rows_per_step, wait, 0, unroll=8)


def _gather_rows(src_t, idx, rows_per_step=512):
    m = idx.shape[0]
    while m % rows_per_step:
        rows_per_step //= 2
    src3 = src_t.reshape(-1, TOKEN_TILE_ROWS, LANES)
    gs = pltpu.PrefetchScalarGridSpec(
        num_scalar_prefetch=1,
        grid=(m // rows_per_step,),
        in_specs=[pl.BlockSpec(memory_space=pl.ANY)],
        out_specs=pl.BlockSpec((rows_per_step, TOKEN_TILE_ROWS, LANES), lambda i, idx_ref: (i, 0, 0)),
        scratch_shapes=[pltpu.SemaphoreType.DMA(())],
    )
    out = pl.pallas_call(
        functools.partial(_gather_kernel, rows_per_step=rows_per_step),
        grid_spec=gs,
        out_shape=jax.ShapeDtypeStruct((m, TOKEN_TILE_ROWS, LANES), src_t.dtype),
        compiler_params=pltpu.CompilerParams(dimension_semantics=("arbitrary",), vmem_limit_bytes=VMEM_LIMIT),
        name="gather_rows",
    )(idx, src3)
    return out.reshape(m * TOKEN_TILE_ROWS, LANES)


def _moe_block_kernel(be_ref, nu_ref, x_ref, wg_ref, wu_ref, wd_ref, o_ref, *, f_tile):
    i = pl.program_id(0)

    @pl.when(i < nu_ref[0])
    def _():
        hb = _untile(x_ref, MOE_BLOCK).astype(BF)
        d_ff = wg_ref.shape[1]
        acc = jnp.zeros((MOE_BLOCK, wd_ref.shape[1]), F32)
        for f in range(d_ff // f_tile):
            fs = slice(f * f_tile, (f + 1) * f_tile)
            gp = _dot(hb, wg_ref[:, fs])
            up = _dot(hb, wu_ref[:, fs])
            act = gp * jax.nn.sigmoid(gp) * up
            acc = acc + _dot(act.astype(BF), wd_ref[fs, :])
        _store_tiled(o_ref, acc)

    @pl.when(i >= nu_ref[0])
    def _():
        o_ref[...] = jnp.zeros(o_ref.shape, F32)


def _moe_blocks(blk_e, n_used, xb_t, wg, wu, wd):
    nb = xb_t.shape[0] // (MOE_BLOCK * TOKEN_TILE_ROWS)
    d, d_ff = wg.shape[1], wg.shape[2]
    blk = pl.BlockSpec((MOE_BLOCK * TOKEN_TILE_ROWS, LANES), lambda i, be, nu: (i, 0))
    gs = pltpu.PrefetchScalarGridSpec(
        num_scalar_prefetch=2,
        grid=(nb,),
        in_specs=[blk,
                  pl.BlockSpec((None, d, d_ff), lambda i, be, nu: (be[i], 0, 0)),
                  pl.BlockSpec((None, d, d_ff), lambda i, be, nu: (be[i], 0, 0)),
                  pl.BlockSpec((None, d_ff, d), lambda i, be, nu: (be[i], 0, 0))],
        out_specs=blk,
    )
    return pl.pallas_call(
        functools.partial(_moe_block_kernel, f_tile=_ff_tile(d_ff)),
        grid_spec=gs,
        out_shape=jax.ShapeDtypeStruct(xb_t.shape, F32),
        compiler_params=_cparams(("arbitrary",)),
        name="moe_blocks",
    )(blk_e, n_used, xb_t, wg, wu, wd)


def _combine_kernel(ya0_ref, ya1_ref, r_ref, y_ref, g2_ref, o_ref):
    tm = o_ref.shape[0]
    w1 = r_ref[:, 2:3]
    w2 = r_ref[:, 3:4]
    o_ref[...] = y_ref[...] + g2_ref[...] * (w1 * _untile(ya0_ref, tm) + w2 * _untile(ya1_ref, tm))


def _moe_combine(ya_t, route, y, gate2, *, per_batch, tokens_per_batch):
    n, d = y.shape
    tm = 512 if per_batch else n
    nt = n // tm
    row = lambda w: pl.BlockSpec((tm, w), lambda i: (i, 0))
    return pl.pallas_call(
        _combine_kernel,
        grid=(nt,),
        in_specs=[pl.BlockSpec((tm * TOKEN_TILE_ROWS, LANES), lambda i: (i, 0)),
                  pl.BlockSpec((tm * TOKEN_TILE_ROWS, LANES), lambda i: (i + nt, 0)), row(LANES), row(d),
                  _mod_spec(per_batch, tm, d, tokens_per_batch)],
        out_specs=row(d),
        out_shape=jax.ShapeDtypeStruct((n, d), F32),
        compiler_params=_cparams(("arbitrary",)),
        name="moe_combine",
    )(ya_t, ya_t, route, y, gate2)


def _moe_ffn(h2t, y, gate2, rw_hi, rw_lo, rb, wg, wu, wd, *, per_batch, tokens_per_batch):
    n, d = y.shape
    n_experts = wg.shape[0]
    route = _router(h2t, rw_hi, rw_lo, rb, n_experts)
    flat_e = route[:, 0:TOP_K].astype(jnp.int32).reshape(-1)
    onehot = (flat_e[:, None] == jnp.arange(n_experts, dtype=jnp.int32)[None, :]).astype(jnp.int32)
    csum = jnp.cumsum(onehot, axis=0)
    rank = jnp.sum(onehot * csum, axis=1) - 1
    counts = csum[-1]
    padded = (counts + MOE_BLOCK - 1) // MOE_BLOCK * MOE_BLOCK
    pad_end = jnp.cumsum(padded)
    pad_start = pad_end - padded
    slot = (pad_start[flat_e] + rank).astype(jnp.int32)
    nb = -(-(n * TOP_K) // MOE_BLOCK) + n_experts
    src = jnp.zeros((nb * MOE_BLOCK,), jnp.int32).at[slot].set(jnp.arange(n * TOP_K, dtype=jnp.int32) // TOP_K)
    blk_e = jnp.minimum(jnp.sum(pad_end[None, :] <= (jnp.arange(nb) * MOE_BLOCK)[:, None], axis=-1),
                        n_experts - 1).astype(jnp.int32)
    n_used = (pad_end[-1:] // MOE_BLOCK).astype(jnp.int32)
    xb = _gather_rows(h2t, src)
    yb = _moe_blocks(blk_e, n_used, xb, wg, wu, wd)
    ya = _gather_rows(yb, jnp.concatenate([slot[0::TOP_K], slot[1::TOP_K]]))
    return _moe_combine(ya, route, y, gate2, per_batch=per_batch, tokens_per_batch=tokens_per_batch)


def _rope_tables(pos):
    half = HEAD_DIM // 2
    inv = ROPE_THETA ** (-jnp.arange(half, dtype=F32) / half)
    ang = pos.astype(F32)[:, None] * inv[None, :]
    cos, sin = jnp.cos(ang), jnp.sin(ang)
    cos_l = jnp.concatenate([cos, cos] * (LANES // HEAD_DIM), axis=1)
    sin_l = jnp.concatenate([-sin, sin] * (LANES // HEAD_DIM), axis=1)
    return cos_l, sin_l


def _cover_matrix(n_rows, n_cmp, n_slc):
    cs = np.arange(n_cmp) * CMP_STRIDE
    ss = np.arange(n_slc) * SLC_BLOCK
    cov = np.clip(np.minimum(cs[:, None] + CMP_BLOCK, ss[None, :] + SLC_BLOCK)
                  - np.maximum(cs[:, None], ss[None, :]), 0, None).astype(np.float32) / CMP_BLOCK
    out = np.zeros((n_rows, LANES), np.float32)
    out[:n_cmp, :n_slc] = cov
    return jnp.asarray(out, BF)


def _expand_matrix(n_keys):
    blk = np.arange(LANES)[:, None]
    key = np.arange(n_keys)[None, :] // SLC_BLOCK
    return jnp.asarray((blk == key).astype(np.float32), BF)


def _block_diag_mean(width):
    idx = np.arange(width) // HEAD_DIM
    return jnp.asarray((idx[:, None] == idx[None, :]).astype(np.float32) / HEAD_DIM, BF)


def _compress_weights(cmp_w_l, cmp_pe_l):
    z = jnp.zeros_like(cmp_w_l)
    wbig = jnp.concatenate([jnp.concatenate([cmp_w_l, z], axis=3), jnp.concatenate([z, cmp_w_l], axis=3)],
                           axis=2).astype(BF)
    pebig = jnp.concatenate([cmp_pe_l, cmp_pe_l], axis=2)
    return wbig, pebig


def kernel(x_prompt, x_sample, cache_cmp, cache_slc, state_win, page_table, c_prompt, c_sample, norm_mix_g, norm_ffn_g, w_ada, b_ada, w_in, w_spatial, b_spatial, q_norm_g, k_norm_g, cmp_pe, cmp_w, out_norm_g, w_out, ffn_w_gate, ffn_w_up, ffn_w_down, router_w, router_b, moe_w_gate, moe_w_up, moe_w_down):
    batch, t, d = x_prompt.shape
    db, tn, _ = x_sample.shape
    depth = w_ada.shape[0]
    n_pool, page = cache_cmp.shape[1], cache_cmp.shape[2]
    n_pages = page_table.shape[1]
    past = n_pages * page
    wb = state_win.shape[2]
    d_a = d // 2
    a_groups = d_a // HEAD_DIM
    assert tn == 1 and past % CHUNK == 0 and t % KEY_TILE == 0 and d_a % LANES == 0
    assert d - d_a == B_KV_HEADS * GQA_R * HEAD_DIM and B_KV_HEADS * HEAD_DIM == LANES
    n_p = batch * t
    kvw = 2 * LANES
    p_main = 2 * d_a + (d - d_a) + N_BRANCH * kvw
    n_gate = w_in.shape[2] - p_main
    qpos_s = past

    m_rows = batch + db
    m_pad = -(-m_rows // 8) * 8
    c_all = jnp.concatenate([c_prompt, c_sample, jnp.zeros((m_pad - m_rows, d), F32)], axis=0)
    mods = _adaln(c_all, w_ada, b_ada)

    bd = _block_diag_mean(d_a)
    cos_p, sin_p = _rope_tables(jnp.arange(t, dtype=jnp.int32))
    cos_s, sin_s = _rope_tables(jnp.full((1,), qpos_s, jnp.int32))
    np_p = t // CMP_STRIDE
    cover_p = _cover_matrix(np_p, np_p - 1, t // SLC_BLOCK)
    expand_p = _expand_matrix(t)
    np_s = past // CMP_STRIDE
    ns_s = -(-(past + tn) // SLC_BLOCK)
    cover_s = _cover_matrix(np_s, np_s - 1, ns_s)
    expand_s = _expand_matrix(past)
    tril = jnp.tril(jnp.ones((CHUNK, CHUNK), dtype=bool))

    cache_cmp4 = cache_cmp.reshape(depth, n_pool, page, kvw)
    cache_slc4 = jnp.transpose(cache_slc, (0, 1, 3, 4, 5, 2)).reshape(depth, n_pool, kvw, page)
    state_win4 = jnp.transpose(state_win, (0, 1, 3, 4, 5, 2)).reshape(depth, db, kvw, wb)

    yp = x_prompt.reshape(n_p, d)
    ys = x_sample.reshape(db, d)
    outs = {k: [] for k in ("cmp_p", "cmp_s", "slc_p", "slc_s", "win_p", "win_s", "chv_s")}
    for l in range(depth):
        mod_p = [mods[l, :batch, k * d:(k + 1) * d].reshape(batch, 1, d) for k in range(6)]
        mod_s = [mods[l, batch:batch + db, k * d:(k + 1) * d] for k in range(6)]
        w_in_bf = jnp.concatenate([w_in[l], jnp.zeros((d, LANES - n_gate), F32)], axis=1).astype(BF)
        qg = jnp.tile(q_norm_g[l], (d - d_a) // HEAD_DIM)[None, :]
        kg = jnp.tile(k_norm_g[l], (1, LANES // HEAD_DIM))
        ws_p = jnp.where(tril, w_spatial[l], 0.0).astype(BF)
        bs_p = jnp.repeat(b_spatial[l].T, HEAD_DIM, axis=1)
        ws_s = jnp.repeat(w_spatial[l][:, 0, 0], HEAD_DIM)[None, :]
        bs_s = jnp.repeat(b_spatial[l][:, 0], HEAD_DIM)[None, :]
        wbig, pebig = _compress_weights(cmp_w[l], cmp_pe[l])
        out_g = out_norm_g[l].reshape(1, d)
        wout_bf = w_out[l].astype(BF)
        ng1 = norm_mix_g[l][None, :]
        ng2 = norm_ffn_g[l][None, :]

        a, q, cmp_r, slc_r, win_r, slc_bf, win_bf, gates = _pre_mix(
            yp, ng1, mod_p[0], mod_p[1], w_in_bf, cos_p, sin_p, bd, qg, kg, ws_p, bs_p,
            chunked=True, tokens_per_batch=t)
        kc, vc = _compress_prompt(cmp_r, wbig, pebig, batch)
        o = _nsa_prompt(q, gates, kc, vc, slc_bf, win_bf, cover_p, expand_p, batch)
        yp, h2p = _post_mix(a, o, yp, mod_p[2], out_g, wout_bf, bd, ng2, mod_p[3], mod_p[4],
                            per_batch=True, tokens_per_batch=t, tiled_h2=(l % 2 == 1))
        outs["cmp_p"].append(cmp_r.reshape(batch, t, 2, B_KV_HEADS, HEAD_DIM))
        outs["slc_p"].append(slc_r.reshape(batch, t, 2, B_KV_HEADS, HEAD_DIM))
        wbp = min(WINDOW, t)
        outs["win_p"].append(win_r.reshape(batch, t, 2, B_KV_HEADS, HEAD_DIM)[:, t - wbp:])

        a_s, q_s, cmp_s, slc_s, win_s, gates_s, v_s = _pre_mix(
            ys, ng1, mod_s[0], mod_s[1], w_in_bf, cos_s, sin_s, bd, qg, kg, ws_s, bs_s,
            chunked=False, tokens_per_batch=1)
        q3 = q_s.reshape(db, 1, d - d_a)
        ocmp, imp = _s_cmp(page_table, q3, cache_cmp4, l, wbig, pebig, cover_s, qpos_s)
        mask = _s_select(imp[:, 0:B_KV_HEADS, :].reshape(db * B_KV_HEADS, LANES), expand_s, qpos_s)
        o_s = _s_att(page_table, q3, gates_s.reshape(db, 1, LANES), mask.reshape(db, B_KV_HEADS, past), ocmp,
                     slc_s.reshape(db, 1, kvw), win_s.reshape(db, 1, kvw), state_win4, cache_slc4, l,
                     qpos_s, past)
        ys, h2s = _post_mix(a_s, o_s.reshape(db, d - d_a), ys, mod_s[2], out_g, wout_bf, bd, ng2, mod_s[3],
                            mod_s[4], per_batch=False, tokens_per_batch=1, tiled_h2=(l % 2 == 1))
        outs["cmp_s"].append(cmp_s.reshape(db, tn, 2, B_KV_HEADS, HEAD_DIM))
        outs["slc_s"].append(slc_s.reshape(db, tn, 2, B_KV_HEADS, HEAD_DIM))
        win_all = jnp.concatenate([state_win[l], win_s.reshape(db, tn, 2, B_KV_HEADS, HEAD_DIM)], axis=1)
        outs["win_s"].append(win_all[:, tn:])
        outs["chv_s"].append(v_s.reshape(db, tn, d_a))

        i = l // 2
        if l % 2 == 0:
            wg, wu, wd = ffn_w_gate[i].astype(BF), ffn_w_up[i].astype(BF), ffn_w_down[i].astype(BF)
            yp = _ffn_dense(h2p, yp, mod_p[5], wg, wu, wd, per_batch=True, tokens_per_batch=t)
            ys = _ffn_dense(h2s, ys, mod_s[5], wg, wu, wd, per_batch=False, tokens_per_batch=1)
        else:
            wg, wu, wd = moe_w_gate[i].astype(BF), moe_w_up[i].astype(BF), moe_w_down[i].astype(BF)
            n_exp = router_w.shape[2]
            rw = jnp.concatenate([router_w[i], jnp.zeros((d, LANES - n_exp), F32)], axis=1)
            rw_hi = rw.astype(BF)
            rw_lo = (rw - rw_hi.astype(F32)).astype(BF)
            rb = jnp.concatenate([router_b[i], jnp.zeros((LANES - n_exp,), F32)])[None, :]
            yp = _moe_ffn(h2p, yp, mod_p[5], rw_hi, rw_lo, rb, wg, wu, wd, per_batch=True, tokens_per_batch=t)
            ys = _moe_ffn(h2s, ys, mod_s[5], rw_hi, rw_lo, rb, wg, wu, wd, per_batch=False, tokens_per_batch=1)

    st = lambda k: jnp.stack(outs[k])
    return (yp.reshape(batch, t, d), ys.reshape(db, tn, d), st("cmp_p"), st("cmp_s"), st("slc_p"), st("slc_s"),
            st("win_p"), st("win_s"), st("chv_s"))
```

```python
import functools

import numpy as np
import jax
import jax.numpy as jnp
from jax import lax
from jax.experimental import pallas as pl
from jax.experimental.pallas import tpu as pltpu

F32 = jnp.float32
BF = jnp.bfloat16

HEAD_DIM = 64
B_KV_HEADS = 2
GQA_R = 4
N_BRANCH = 3
CHUNK = 128
CMP_BLOCK = 32
CMP_STRIDE = 16
SLC_BLOCK = 64
N_SELECT = 16
WINDOW = 512
Q_BLOCK = 128
ROPE_THETA = 10000.0
TOP_K = 2
MOE_BLOCK = 128
NEG_INF = -1e30
FORCE_SCORE = 1e30
EPS = 1e-6
LOG2_E = 1.4426950408889634

LANES = 128
TOKEN_TILE_ROWS = 8
SAMPLES_PER_STEP = 2
KEY_TILE = 512
VMEM_LIMIT = 56 * 1024 * 1024


def _cparams(sem):
    return pltpu.CompilerParams(dimension_semantics=sem, vmem_limit_bytes=VMEM_LIMIT)


def _dot(a, b):
    return jnp.dot(a, b, preferred_element_type=F32)


def _dot_nt(a, b):
    return lax.dot_general(a, b, (((1,), (1,)), ((), ())), preferred_element_type=F32)


def _split_dot(x, w):
    hi = x.astype(BF)
    lo = (x - hi.astype(F32)).astype(BF)
    return _dot(hi, w) + _dot(lo, w)


def _ada_kernel(c_ref, w_ref, b_ref, o_ref):
    c = c_ref[...]
    s = c * jax.nn.sigmoid(c)
    o_ref[0] = _dot(s.astype(BF), w_ref[0].astype(BF)) + b_ref[0]


def _adaln(c_all, w_ada, b_ada):
    depth, d, n6 = w_ada.shape
    m = c_all.shape[0]
    tn = 1536
    return pl.pallas_call(
        _ada_kernel,
        grid=(depth, n6 // tn),
        in_specs=[pl.BlockSpec((m, d), lambda l, j: (0, 0)),
                  pl.BlockSpec((1, d, tn), lambda l, j: (l, 0, j)),
                  pl.BlockSpec((1, 1, tn), lambda l, j: (l, 0, j))],
        out_specs=pl.BlockSpec((1, m, tn), lambda l, j: (l, 0, j)),
        out_shape=jax.ShapeDtypeStruct((depth, m, n6), F32),
        compiler_params=_cparams(("arbitrary", "arbitrary")),
        name="adaln",
    )(c_all, w_ada, b_ada.reshape(depth, 1, n6))


def _pre_kernel(x_ref, g_ref, sh_ref, sc_ref, w_ref, cos_ref, sin_ref, bd_ref, qg_ref, kg_ref,
                ws_ref, bs_ref, *out_refs, chunked, d_a, d_b):
    if chunked:
        a_ref, q_ref, cmp_ref, slc_ref, win_ref, slcb_ref, winb_ref, gate_ref = out_refs
    else:
        a_ref, q_ref, cmp_ref, slc_ref, win_ref, gate_ref, v_ref = out_refs
    tm = x_ref.shape[0]
    x = x_ref[...]
    ms = jnp.mean(x * x, axis=-1, keepdims=True)
    h = x * lax.rsqrt(ms + EPS) * g_ref[...]
    h = h * (1.0 + sc_ref[...]) + sh_ref[...]
    z = _dot(h.astype(BF), w_ref[...])
    uv = jax.nn.gelu(z[:, :2 * d_a])
    u = uv[:, :d_a]
    v = uv[:, d_a:]
    bd = bd_ref[...]
    mu = _split_dot(v, bd)
    vc = v - mu
    vn = vc * lax.rsqrt(_split_dot(vc * vc, bd) + EPS)

    cos = cos_ref[...]
    sin = sin_ref[...]
    lane = lax.broadcasted_iota(jnp.int32, (1, LANES), 1)
    first_half = (lane % HEAD_DIM) < (HEAD_DIM // 2)

    def rope(t):
        swapped = jnp.where(first_half, pltpu.roll(t, LANES - HEAD_DIM // 2, 1),
                            pltpu.roll(t, HEAD_DIM // 2, 1))
        return t * cos + swapped * sin

    o = 2 * d_a
    zq = z[:, o:o + d_b]
    qn = zq * lax.rsqrt(_split_dot(zq * zq, bd) + EPS) * qg_ref[...]
    for c in range(d_b // LANES):
        q_ref[:, c * LANES:(c + 1) * LANES] = rope(qn[:, c * LANES:(c + 1) * LANES]) * (HEAD_DIM ** -0.5)
    o += d_b
    bd1 = bd_ref[0:LANES, 0:LANES]
    row_refs = (cmp_ref, slc_ref, win_ref)
    bf_refs = (None, slcb_ref, winb_ref) if chunked else (None, None, None)
    for br in range(N_BRANCH):
        kp = z[:, o:o + LANES]
        vp = z[:, o + LANES:o + 2 * LANES]
        kn = kp * lax.rsqrt(_split_dot(kp * kp, bd1) + EPS) * kg_ref[br:br + 1, :]
        kr = rope(kn)
        row_refs[br][:, 0:LANES] = kr
        row_refs[br][:, LANES:2 * LANES] = vp
        if bf_refs[br] is not None:
            bf_refs[br][:, 0:LANES] = kr.astype(BF)
            bf_refs[br][:, LANES:2 * LANES] = vp.astype(BF)
        o += 2 * LANES
    gate_ref[...] = jax.nn.sigmoid(z[:, o:o + LANES])

    if chunked:
        low = lane < HEAD_DIM
        for c in range(tm // CHUNK):
            rs = slice(c * CHUNK, (c + 1) * CHUNK)
            for j in range(d_a // LANES):
                cs = slice(j * LANES, (j + 1) * LANES)
                vpair = vn[rs, cs].astype(BF)
                m0 = _dot(ws_ref[2 * j], vpair)
                m1 = _dot(ws_ref[2 * j + 1], vpair)
                mixed = jnp.where(low, m0, m1) + bs_ref[:, cs]
                a_ref[rs, cs] = u[rs, cs] * mixed
    else:
        a_ref[...] = u * (vn * ws_ref[...] + bs_ref[...])
        v_ref[...] = vn


def _pre_mix(x, norm_g, shift, scale, w_in_bf, cos, sin, bd, qg, kg, ws, bs, *, chunked, tokens_per_batch):
    n, d = x.shape
    d_a = d_b = d // 2
    tm = 256 if chunked else n
    nt = n // tm
    pw = w_in_bf.shape[1]
    full = lambda shp: pl.BlockSpec(shp, lambda i: (0,) * len(shp))
    row = lambda w: pl.BlockSpec((tm, w), lambda i: (i, 0))
    if chunked:
        tpb = tokens_per_batch // tm
        mod_spec = pl.BlockSpec((None, 1, d), lambda i: (i // tpb, 0, 0))
        rot_spec = pl.BlockSpec((tm, LANES), lambda i: (i % tpb, 0))
        ws_spec, bs_spec = full(ws.shape), full(bs.shape)
        out_shape = [jax.ShapeDtypeStruct((n, d_a), F32), jax.ShapeDtypeStruct((n, d_b), F32),
                     jax.ShapeDtypeStruct((n, 2 * LANES), F32), jax.ShapeDtypeStruct((n, 2 * LANES), F32),
                     jax.ShapeDtypeStruct((n, 2 * LANES), F32), jax.ShapeDtypeStruct((n, 2 * LANES), BF),
                     jax.ShapeDtypeStruct((n, 2 * LANES), BF), jax.ShapeDtypeStruct((n, LANES), F32)]
        out_specs = [row(d_a), row(d_b), row(2 * LANES), row(2 * LANES), row(2 * LANES), row(2 * LANES),
                     row(2 * LANES), row(LANES)]
    else:
        mod_spec = row(d)
        rot_spec = full((1, LANES))
        ws_spec, bs_spec = full(ws.shape), full(bs.shape)
        out_shape = [jax.ShapeDtypeStruct((n, d_a), F32), jax.ShapeDtypeStruct((n, d_b), F32),
                     jax.ShapeDtypeStruct((n, 2 * LANES), F32), jax.ShapeDtypeStruct((n, 2 * LANES), F32),
                     jax.ShapeDtypeStruct((n, 2 * LANES), F32), jax.ShapeDtypeStruct((n, LANES), F32),
                     jax.ShapeDtypeStruct((n, d_a), F32)]
        out_specs = [row(d_a), row(d_b), row(2 * LANES), row(2 * LANES), row(2 * LANES), row(LANES), row(d_a)]
    return pl.pallas_call(
        functools.partial(_pre_kernel, chunked=chunked, d_a=d_a, d_b=d_b),
        grid=(nt,),
        in_specs=[row(d), full((1, d)), mod_spec, mod_spec, full((d, pw)), rot_spec, rot_spec,
                  full(bd.shape), full(qg.shape), full(kg.shape), ws_spec, bs_spec],
        out_specs=out_specs,
        out_shape=out_shape,
        compiler_params=_cparams(("arbitrary",)),
        name="pre_mix_prompt" if chunked else "pre_mix_sample",
    )(x, norm_g, shift, scale, w_in_bf, cos, sin, bd, qg, kg, ws, bs)


def _compress_pieces(load_piece_rows, w_ref, pe_ref, kv, n_piece, seg=None):
    assert CMP_BLOCK == 2 * CMP_STRIDE
    seg = n_piece if seg is None else seg
    acc_a = jnp.zeros((n_piece, LANES), F32)
    acc_b = jnp.zeros((n_piece, LANES), F32)
    for l in range(CMP_STRIDE):
        xl = load_piece_rows(l)
        l2 = CMP_STRIDE + l
        acc_a = acc_a + _dot((xl + pe_ref[kv, l:l + 1, :]).astype(BF), w_ref[kv, l])
        acc_b = acc_b + _dot((xl + pe_ref[kv, l2:l2 + 1, :]).astype(BF), w_ref[kv, l2])
    shifted = pltpu.roll(acc_b, n_piece - 1, 0)
    rows = lax.broadcasted_iota(jnp.int32, (n_piece, 1), 0)
    return jnp.where(rows % seg < seg - 1, acc_a + shifted, 0.0)


def _compress_kernel(xk_ref, xv_ref, w_ref, pe_ref, kc_ref, vc_ref):
    n_piece = kc_ref.shape[0]
    for kv, (x_ref, o_ref) in enumerate(((xk_ref, kc_ref), (xv_ref, vc_ref))):
        out = _compress_pieces(lambda l: x_ref[pl.ds(l, n_piece, stride=CMP_STRIDE), :], w_ref, pe_ref, kv, n_piece)
        o_ref[...] = out.astype(BF)


def _compress_prompt(cmp_rows, wbig, pebig, batch):
    n = cmp_rows.shape[0]
    t = n // batch
    n_piece = t // CMP_STRIDE
    out = jax.ShapeDtypeStruct((batch * n_piece, LANES), BF)
    return pl.pallas_call(
        _compress_kernel,
        grid=(batch,),
        in_specs=[pl.BlockSpec((t, LANES), lambda b: (b, 0)),
                  pl.BlockSpec((t, LANES), lambda b: (b, 1)),
                  pl.BlockSpec(wbig.shape, lambda b: (0, 0, 0, 0)),
                  pl.BlockSpec(pebig.shape, lambda b: (0, 0, 0))],
        out_specs=[pl.BlockSpec((n_piece, LANES), lambda b: (b, 0))] * 2,
        out_shape=[out, out],
        compiler_params=_cparams(("arbitrary",)),
        name="compress_prompt",
    )(cmp_rows, cmp_rows, wbig, pebig)


def _group_queries(q, g):
    m = q.shape[0]
    low = lax.broadcasted_iota(jnp.int32, (m, LANES), 1) < HEAD_DIM
    keep = low if g == 0 else jnp.logical_not(low)
    parts = []
    for r in range(GQA_R):
        h = GQA_R * g + r
        t = q[:, (h // 2) * LANES:(h // 2 + 1) * LANES]
        if h % 2 != g:
            t = pltpu.roll(t, HEAD_DIM, 1)
        parts.append(jnp.where(keep, t, 0.0))
    return parts


def _top_select(imp, blk_f):
    sel = jnp.zeros(imp.shape, F32)
    work = imp
    for _ in range(N_SELECT):
        mx = jnp.max(work, axis=-1, keepdims=True)
        first = jnp.min(jnp.where(work == mx, blk_f, float(LANES)), axis=-1, keepdims=True)
        hit = blk_f == first
        sel = jnp.where(hit, 1.0, sel)
        work = jnp.where(hit, -jnp.inf, work)
    return sel


def _block_scores(imp, qpos):
    blk = lax.broadcasted_iota(jnp.int32, imp.shape, 1)
    cur = qpos // SLC_BLOCK
    forced = (blk == 0) | (blk == cur) | (blk == cur - 1)
    valid = blk <= cur
    imp = jnp.where(forced, FORCE_SCORE, imp)
    imp = jnp.where(valid, imp, NEG_INF)
    return imp, blk.astype(F32), valid


def _merge_heads(res, m):
    low = lax.broadcasted_iota(jnp.int32, (m, LANES), 1) < HEAD_DIM
    cols = []
    for c in range(4):
        a, b = res[2 * c], res[2 * c + 1]
        if c // 2 == 0:
            cols.append(jnp.where(low, a, pltpu.roll(b, HEAD_DIM, 1)))
        else:
            cols.append(jnp.where(low, pltpu.roll(a, HEAD_DIM, 1), b))
    return jnp.concatenate(cols, axis=1)


def _nsa_prompt_kernel(q_ref, gate_ref, kc_ref, vc_ref, slc_ref, win_ref, cover_ref, e_ref, o_ref):
    i = pl.program_id(1)
    s0 = i * Q_BLOCK
    n_cmp = kc_ref.shape[0]
    rows4 = GQA_R * Q_BLOCK
    q = q_ref[...] * LOG2_E
    gates = gate_ref[...]
    tpos = s0 + lax.broadcasted_iota(jnp.int32, (Q_BLOCK, 1), 0)
    tpos4 = jnp.concatenate([tpos] * GQA_R, axis=0)
    tile4 = lambda x: jnp.concatenate([x] * GQA_R, axis=0)
    groups = range(B_KV_HEADS)
    qg = [jnp.concatenate(_group_queries(q, g), axis=0).astype(BF) for g in groups]

    kc = kc_ref[...]
    vc = vc_ref[...]
    c_last = lax.broadcasted_iota(jnp.int32, (1, n_cmp), 1) * CMP_STRIDE + (CMP_BLOCK - 1)
    c_bias = jnp.where(c_last <= tpos4, 0.0, NEG_INF)
    has_cmp = jnp.where(tpos4 >= CMP_BLOCK - 1, 1.0, 0.0)
    o_cmp, p_sums = [], []
    for g in groups:
        s = _dot_nt(qg[g], kc) + c_bias
        e = jnp.exp2(s - jnp.max(s, axis=-1, keepdims=True))
        p = e * (has_cmp / jnp.sum(e, axis=-1, keepdims=True))
        o_cmp.append(_dot(p.astype(BF), vc))
        p_sum = p[0:Q_BLOCK]
        for r in range(1, GQA_R):
            p_sum = p_sum + p[r * Q_BLOCK:(r + 1) * Q_BLOCK]
        p_sums.append(p_sum)

    imp = _split_dot(jnp.concatenate(p_sums, axis=0), cover_ref[...])
    imp, blk_f, valid = _block_scores(imp, jnp.concatenate([tpos] * B_KV_HEADS, axis=0))
    sel_all = jnp.where(valid, _top_select(imp, blk_f), 0.0).astype(BF)
    sel_b = [sel_all[g * Q_BLOCK:(g + 1) * Q_BLOCK] for g in groups]

    w_start = pl.multiple_of(jnp.maximum(s0 - WINDOW, 0), Q_BLOCK)
    w_len = WINDOW + Q_BLOCK
    wpos = w_start + lax.broadcasted_iota(jnp.int32, (1, w_len), 1)
    w_bias4 = tile4(jnp.where((wpos <= tpos) & (wpos > tpos - WINDOW), 0.0, NEG_INF))
    kw = win_ref[pl.ds(w_start, w_len), 0:LANES]
    vw = win_ref[pl.ds(w_start, w_len), LANES:2 * LANES]
    low_w = lax.broadcasted_iota(jnp.int32, (1, LANES), 1) < HEAD_DIM
    keep = [low_w, jnp.logical_not(low_w)]
    with_ones = lambda v, g: jnp.where(keep[g], v, jnp.ones_like(v))
    den_lane = [HEAD_DIM, 0]

    def normalise(acc, g):
        return acc / acc[:, den_lane[g]:den_lane[g] + 1]

    o_win = []
    for g in groups:
        sw = _dot_nt(qg[g], kw) + w_bias4
        ew = jnp.exp2(sw - jnp.max(sw, axis=-1, keepdims=True))
        o_win.append(_dot(ew.astype(BF), vw) / jnp.sum(ew, axis=-1, keepdims=True))

    def body(kt, carry):
        k0 = pl.multiple_of(kt * KEY_TILE, KEY_TILE)
        kk = slc_ref[pl.ds(k0, KEY_TILE), 0:LANES]
        vv = slc_ref[pl.ds(k0, KEY_TILE), LANES:2 * LANES]
        et = e_ref[:, pl.ds(k0, KEY_TILE)]
        kpos = k0 + lax.broadcasted_iota(jnp.int32, (1, KEY_TILE), 1)
        causal = kpos <= tpos
        out = []
        for g in groups:
            m, acc = carry[g]
            bias = jnp.where((_dot(sel_b[g], et) > 0.5) & causal, 0.0, NEG_INF)
            sc = _dot_nt(qg[g], kk) + tile4(bias)
            m_new = jnp.maximum(m, jnp.max(sc, axis=-1, keepdims=True))
            pe = jnp.exp2(sc - m_new)
            acc = jnp.exp2(m - m_new) * acc + _dot(pe.astype(BF), with_ones(vv, g))
            out.append((m_new, acc))
        return tuple(out)

    init = tuple((jnp.full((rows4, 1), NEG_INF, F32), jnp.zeros((rows4, LANES), F32)) for _ in groups)
    n_tiles = (s0 + Q_BLOCK + KEY_TILE - 1) // KEY_TILE
    fin = lax.fori_loop(0, n_tiles, body, init)

    res = []
    for g in groups:
        o_slc = normalise(fin[g][1], g)
        for r in range(GQA_R):
            h = GQA_R * g + r
            rs = slice(r * Q_BLOCK, (r + 1) * Q_BLOCK)
            res.append(gates[:, 3 * h:3 * h + 1] * o_cmp[g][rs] + gates[:, 3 * h + 1:3 * h + 2] * o_slc[rs]
                       + gates[:, 3 * h + 2:3 * h + 3] * o_win[g][rs])
    o_ref[...] = _merge_heads(res, Q_BLOCK)


def _nsa_prompt(q, gates, kc, vc, slc_bf, win_bf, cover, expand, batch):
    n, d_b = q.shape
    t = n // batch
    nq = t // Q_BLOCK
    n_cmp = kc.shape[0] // batch
    return pl.pallas_call(
        _nsa_prompt_kernel,
        grid=(batch, nq),
        in_specs=[pl.BlockSpec((Q_BLOCK, d_b), lambda b, i: (b * nq + i, 0)),
                  pl.BlockSpec((Q_BLOCK, LANES), lambda b, i: (b * nq + i, 0)),
                  pl.BlockSpec((n_cmp, LANES), lambda b, i: (b, 0)),
                  pl.BlockSpec((n_cmp, LANES), lambda b, i: (b, 0)),
                  pl.BlockSpec((t, 2 * LANES), lambda b, i: (b, 0)),
                  pl.BlockSpec((t, 2 * LANES), lambda b, i: (b, 0)),
                  pl.BlockSpec(cover.shape, lambda b, i: (0, 0)),
                  pl.BlockSpec(expand.shape, lambda b, i: (0, 0))],
        out_specs=pl.BlockSpec((Q_BLOCK, d_b), lambda b, i: (b * nq + i, 0)),
        out_shape=jax.ShapeDtypeStruct((n, d_b), F32),
        compiler_params=_cparams(("arbitrary", "arbitrary")),
        name="nsa_prompt",
    )(q, gates, kc, vc, slc_bf, win_bf, cover, expand)


def _sample_queries(q_row):
    sub = lax.broadcasted_iota(jnp.int32, (8, LANES), 0)
    out = jnp.zeros((8, LANES), F32)
    for g in range(B_KV_HEADS):
        for r, part in enumerate(_group_queries(q_row, g)):
            out = jnp.where(sub == GQA_R * g + r, jnp.broadcast_to(part, (8, LANES)), out)
    return out


def _s_cmp_kernel(pt_ref, q_ref, *refs, n_pages, n_samp, qpos):
    n_pg = n_samp * n_pages
    pages = refs[:n_pg]
    perm_ref, wbig_ref, pe_ref, cover_ref, ocmp_ref, imp_ref = refs[n_pg:]
    per_page = pages[0].shape[1] // CMP_STRIDE
    n_piece = n_pages * per_page

    def loader(kv):
        xp = [_dot_nt(perm_ref[...], pg[kv * LANES:(kv + 1) * LANES, :].astype(BF)) for pg in pages]
        return lambda l: jnp.concatenate([x[l * per_page:(l + 1) * per_page, :] for x in xp], axis=0)

    kc_all = _compress_pieces(loader(0), wbig_ref, pe_ref, 0, n_samp * n_piece, n_piece).astype(BF)
    vc_all = _compress_pieces(loader(1), wbig_ref, pe_ref, 1, n_samp * n_piece, n_piece).astype(BF)
    c_last = lax.broadcasted_iota(jnp.int32, (1, n_piece), 1) * CMP_STRIDE + (CMP_BLOCK - 1)
    cv = c_last <= qpos
    sub = lax.broadcasted_iota(jnp.int32, (8, n_piece), 0)
    for i in range(n_samp):
        kc = kc_all[i * n_piece:(i + 1) * n_piece]
        vc = vc_all[i * n_piece:(i + 1) * n_piece]
        qa = _sample_queries(q_ref[i])
        s = jnp.where(cv, _dot_nt(qa.astype(BF), kc), NEG_INF)
        e = jnp.where(cv, jnp.exp(s - jnp.max(s, axis=-1, keepdims=True)), 0.0)
        p = e / jnp.maximum(jnp.sum(e, axis=-1, keepdims=True), 1e-30)
        ocmp_ref[i] = _dot(p.astype(BF), vc)
        ps0 = jnp.sum(jnp.where(sub < GQA_R, p, 0.0), axis=0, keepdims=True)
        ps1 = jnp.sum(jnp.where(sub >= GQA_R, p, 0.0), axis=0, keepdims=True)
        p_sum = jnp.where(sub == 0, ps0, jnp.where(sub == 1, ps1, 0.0))
        imp_ref[i] = _split_dot(p_sum, cover_ref[...])


def _samples_per_step(db):
    return SAMPLES_PER_STEP if db % SAMPLES_PER_STEP == 0 else 1


def _page_specs(n_samp, n_pages, layer, rows, page, kvs):
    def spec(kv, i, j):
        if rows == LANES:
            return pl.BlockSpec((None, None, LANES, page), lambda b, pt: (layer, pt[b * n_samp + i, j], kv, 0))
        return pl.BlockSpec((None, None, rows, page), lambda b, pt: (layer, pt[b * n_samp + i, j], 0, 0))
    return [spec(kv, i, j) for kv in kvs for i in range(n_samp) for j in range(n_pages)]


def _s_cmp(page_table, q3, cache_l, layer, perm, wbig, pebig, cover, qpos):
    db = q3.shape[0]
    n_pages = page_table.shape[1]
    page = cache_l.shape[3]
    n_piece = n_pages * page // CMP_STRIDE
    n_samp = _samples_per_step(db)
    page_specs = _page_specs(n_samp, n_pages, layer, 2 * LANES, page, (0,))
    per_b = lambda r, w: pl.BlockSpec((n_samp, r, w), lambda b, pt: (b, 0, 0))
    gs = pltpu.PrefetchScalarGridSpec(
        num_scalar_prefetch=1,
        grid=(db // n_samp,),
        in_specs=[per_b(1, q3.shape[2])] + page_specs + [
            pl.BlockSpec(perm.shape, lambda b, pt: (0, 0)),
            pl.BlockSpec(wbig.shape, lambda b, pt: (0, 0, 0, 0)),
            pl.BlockSpec(pebig.shape, lambda b, pt: (0, 0, 0)),
            pl.BlockSpec(cover.shape, lambda b, pt: (0, 0))],
        out_specs=[per_b(8, LANES), per_b(8, LANES)],
    )
    assert cover.shape[0] == n_piece
    return pl.pallas_call(
        functools.partial(_s_cmp_kernel, n_pages=n_pages, n_samp=n_samp, qpos=qpos),
        grid_spec=gs,
        out_shape=[jax.ShapeDtypeStruct((db, 8, LANES), F32), jax.ShapeDtypeStruct((db, 8, LANES), F32)],
        compiler_params=_cparams(("arbitrary",)),
        name="nsa_sample_cmp",
    )(page_table, q3, *([cache_l] * (n_samp * n_pages)), perm, wbig, pebig, cover)


def _s_select_kernel(imp_ref, e_ref, o_ref, *, qpos):
    imp = imp_ref[...]
    pos = jnp.full((imp.shape[0], 1), qpos, jnp.int32)
    imp, blk_f, valid = _block_scores(imp, pos)
    sel = _top_select(imp, blk_f)
    sel_b = jnp.where(valid, sel, 0.0).astype(BF)
    o_ref[...] = _dot(sel_b, e_ref[...])


def _s_select(imp2, expand, qpos):
    m = imp2.shape[0]
    past = expand.shape[1]
    return pl.pallas_call(
        functools.partial(_s_select_kernel, qpos=qpos),
        grid=(1,),
        in_specs=[pl.BlockSpec(imp2.shape, lambda i: (0, 0)), pl.BlockSpec(expand.shape, lambda i: (0, 0))],
        out_specs=pl.BlockSpec((m, past), lambda i: (0, 0)),
        out_shape=jax.ShapeDtypeStruct((m, past), F32),
        compiler_params=_cparams(("arbitrary",)),
        name="nsa_sample_select",
    )(imp2, expand)


def _s_att_kernel(pt_ref, q_ref, gate_ref, mask_ref, ocmp_ref, slcn_ref, winn_ref, win_ref, *refs,
                  n_pages, n_samp, qpos, past):
    for i in range(n_samp):
        _s_att_one(q_ref.at[i], gate_ref.at[i], mask_ref.at[i], ocmp_ref.at[i], slcn_ref.at[i], winn_ref.at[i],
                   win_ref.at[i], refs[i * n_pages:(i + 1) * n_pages], refs[n_samp * n_pages].at[i],
                   qpos=qpos, past=past)


def _s_att_one(q_ref, gate_ref, mask_ref, ocmp_ref, slcn_ref, winn_ref, win_ref, pages, o_ref, *, qpos, past):
    page = pages[0].shape[1]
    qa = _sample_queries(q_ref[...])
    qb = qa.astype(BF)
    sub = lax.broadcasted_iota(jnp.int32, (8, 1), 0)
    mrow = jnp.where(sub < GQA_R, mask_ref[0:1, :], mask_ref[1:2, :])
    bias = jnp.where(mrow > 0.5, 0.0, NEG_INF)
    sc = jnp.concatenate([_dot(qb, pg[0:LANES, :].astype(BF)) for pg in pages], axis=1) + bias
    kn = slcn_ref[:, 0:LANES]
    vn = slcn_ref[:, LANES:2 * LANES]
    s_new = jnp.sum(qa * kn, axis=-1, keepdims=True)
    m = jnp.maximum(jnp.max(sc, axis=-1, keepdims=True), s_new)
    e = jnp.exp(sc - m)
    e_new = jnp.exp(s_new - m)
    acc = e_new * vn
    for j, pg in enumerate(pages):
        acc = acc + _dot_nt(e[:, j * page:(j + 1) * page].astype(BF), pg[LANES:2 * LANES, :].astype(BF))
    o_slc = acc / (jnp.sum(e, axis=-1, keepdims=True) + e_new)
    wb = win_ref.shape[1]
    wpos = (past - wb) + lax.broadcasted_iota(jnp.int32, (1, wb), 1)
    wvalid = (wpos <= qpos) & (wpos > qpos - WINDOW) & (wpos >= 0)
    sw = _dot(qb, win_ref[0:LANES, :].astype(BF)) + jnp.where(wvalid, 0.0, NEG_INF)
    kwn = winn_ref[:, 0:LANES]
    vwn = winn_ref[:, LANES:2 * LANES]
    sw_new = jnp.sum(qa * kwn, axis=-1, keepdims=True)
    mw = jnp.maximum(jnp.max(sw, axis=-1, keepdims=True), sw_new)
    ew = jnp.exp(sw - mw)
    ew_new = jnp.exp(sw_new - mw)
    o_win = (_dot_nt(ew.astype(BF), win_ref[LANES:2 * LANES, :].astype(BF)) + ew_new * vwn) / (
        jnp.sum(ew, axis=-1, keepdims=True) + ew_new)
    lane = lax.broadcasted_iota(jnp.int32, (8, LANES), 1)
    gt = jnp.broadcast_to(gate_ref[...], (8, LANES))
    gsel = [jnp.sum(jnp.where(lane == 3 * sub + br, gt, 0.0), axis=-1, keepdims=True) for br in range(N_BRANCH)]
    res = gsel[0] * ocmp_ref[...] + gsel[1] * o_slc + gsel[2] * o_win
    o_ref[...] = _merge_heads([res[h:h + 1, :] for h in range(8)], 1)


def _s_att(page_table, q3, gates3, mask3, ocmp, slc_new3, win_new3, win_l, cache_l, layer, qpos, past):
    db = q3.shape[0]
    n_pages = page_table.shape[1]
    page = cache_l.shape[3]
    wb = win_l.shape[3]
    n_samp = _samples_per_step(db)
    page_specs = _page_specs(n_samp, n_pages, layer, 2 * LANES, page, (0,))
    per_b = lambda r, w: pl.BlockSpec((n_samp, r, w), lambda b, pt: (b, 0, 0))
    gs = pltpu.PrefetchScalarGridSpec(
        num_scalar_prefetch=1,
        grid=(db // n_samp,),
        in_specs=[per_b(1, q3.shape[2]), per_b(1, LANES), per_b(2, past), per_b(8, LANES),
                  per_b(1, 2 * LANES), per_b(1, 2 * LANES),
                  pl.BlockSpec((None, n_samp, 2 * LANES, wb), lambda b, pt: (layer, b, 0, 0))] + page_specs,
        out_specs=per_b(1, q3.shape[2]),
    )
    return pl.pallas_call(
        functools.partial(_s_att_kernel, n_pages=n_pages, n_samp=n_samp, qpos=qpos, past=past),
        grid_spec=gs,
        out_shape=jax.ShapeDtypeStruct(q3.shape, F32),
        compiler_params=_cparams(("arbitrary",)),
        name="nsa_sample_att",
    )(page_table, q3, gates3, mask3, ocmp, slc_new3, win_new3, win_l, *([cache_l] * (n_samp * n_pages)))


def _untile(ref, tm):
    return jnp.concatenate([ref[pl.ds(c, tm, stride=TOKEN_TILE_ROWS), :] for c in range(TOKEN_TILE_ROWS)], axis=1)


def _store_tiled(ref, val):
    tm = val.shape[0]
    for c in range(TOKEN_TILE_ROWS):
        ref[pl.ds(c, tm, stride=TOKEN_TILE_ROWS), :] = val[:, c * LANES:(c + 1) * LANES]


def _post_kernel(a_ref, o_ref, x_ref, g1_ref, og_ref, wout_ref, bd_ref, ng_ref, sh2_ref, sc2_ref, y_ref, h2_ref,
                 *, tiled_h2):
    bd = bd_ref[...]
    d_a = a_ref.shape[1]

    def head_norm(t, g):
        return t * lax.rsqrt(_split_dot(t * t, bd) + EPS) * g

    a = head_norm(a_ref[...], og_ref[:, :d_a])
    o = head_norm(o_ref[...], og_ref[:, d_a:])
    mix = _dot(a.astype(BF), wout_ref[:d_a, :]) + _dot(o.astype(BF), wout_ref[d_a:, :])
    y = x_ref[...] + g1_ref[...] * mix
    y_ref[...] = y
    ms = jnp.mean(y * y, axis=-1, keepdims=True)
    h2 = y * lax.rsqrt(ms + EPS) * ng_ref[...]
    h2 = h2 * (1.0 + sc2_ref[...]) + sh2_ref[...]
    if tiled_h2:
        _store_tiled(h2_ref, h2)
    else:
        h2_ref[...] = h2


def _mod_spec(per_batch, tm, d, tokens_per_batch):
    if per_batch:
        tpb = tokens_per_batch // tm
        return pl.BlockSpec((None, 1, d), lambda i: (i // tpb, 0, 0))
    return pl.BlockSpec((tm, d), lambda i: (i, 0))


def _post_mix(a, o, x, gate1, out_g, wout_bf, bd, ffn_g, shift2, scale2, *, per_batch, tokens_per_batch, tiled_h2):
    n, d = x.shape
    tm = 512 if per_batch else n
    full = lambda shp: pl.BlockSpec(shp, lambda i: (0,) * len(shp))
    row = lambda w: pl.BlockSpec((tm, w), lambda i: (i, 0))
    ms = _mod_spec(per_batch, tm, d, tokens_per_batch)
    if tiled_h2:
        assert d == TOKEN_TILE_ROWS * LANES
        h2_spec = pl.BlockSpec((tm * TOKEN_TILE_ROWS, LANES), lambda i: (i, 0))
        h2_shape = jax.ShapeDtypeStruct((n * TOKEN_TILE_ROWS, LANES), F32)
    else:
        h2_spec, h2_shape = row(d), jax.ShapeDtypeStruct((n, d), F32)
    return pl.pallas_call(
        functools.partial(_post_kernel, tiled_h2=tiled_h2),
        grid=(n // tm,),
        in_specs=[row(a.shape[1]), row(o.shape[1]), row(d), ms, full((1, d)), full(wout_bf.shape), full(bd.shape),
                  full((1, d)), ms, ms],
        out_specs=[row(d), h2_spec],
        out_shape=[jax.ShapeDtypeStruct((n, d), F32), h2_shape],
        compiler_params=_cparams(("arbitrary",)),
        name="post_mix",
    )(a, o, x, gate1, out_g, wout_bf, bd, ffn_g, shift2, scale2)


def _ffn_kernel(h_ref, y_ref, g2_ref, wg_ref, wu_ref, wd_ref, o_ref, *, f_tile):
    hb = h_ref[...].astype(BF)
    d_ff = wg_ref.shape[1]
    acc = jnp.zeros(o_ref.shape, F32)
    for f in range(d_ff // f_tile):
        fs = slice(f * f_tile, (f + 1) * f_tile)
        gp = _dot(hb, wg_ref[:, fs])
        up = _dot(hb, wu_ref[:, fs])
        act = gp * jax.nn.sigmoid(gp) * up
        acc = acc + _dot(act.astype(BF), wd_ref[fs, :])
    o_ref[...] = y_ref[...] + g2_ref[...] * acc


def _ff_tile(d_ff):
    for cand in (1408, 1024, 512, 256, 128):
        if d_ff % cand == 0:
            return cand
    return d_ff


def _ffn_dense(h2, y, gate2, wg, wu, wd, *, per_batch, tokens_per_batch):
    n, d = y.shape
    tm = 512 if per_batch else n
    full = lambda shp: pl.BlockSpec(shp, lambda i: (0,) * len(shp))
    row = lambda w: pl.BlockSpec((tm, w), lambda i: (i, 0))
    return pl.pallas_call(
        functools.partial(_ffn_kernel, f_tile=_ff_tile(wg.shape[1])),
        grid=(n // tm,),
        in_specs=[row(d), row(d), _mod_spec(per_batch, tm, d, tokens_per_batch), full(wg.shape), full(wu.shape),
                  full(wd.shape)],
        out_specs=row(d),
        out_shape=jax.ShapeDtypeStruct((n, d), F32),
        compiler_params=_cparams(("arbitrary",)),
        name="ffn_dense",
    )(h2, y, gate2, wg, wu, wd)


def _router_kernel(h_ref, rwh_ref, rwl_ref, rb_ref, o_ref, *, n_experts):
    h = _untile(h_ref, o_ref.shape[0])
    hi = h.astype(BF)
    lo = (h - hi.astype(F32)).astype(BF)
    logits = _dot(hi, rwh_ref[...]) + _dot(lo, rwh_ref[...]) + _dot(hi, rwl_ref[...]) + rb_ref[...]
    lane = lax.broadcasted_iota(jnp.int32, logits.shape, 1)
    lane_f = lane.astype(F32)
    l1 = jnp.where(lane < n_experts, logits, -jnp.inf)
    m1 = jnp.max(l1, axis=-1, keepdims=True)
    i1 = jnp.min(jnp.where(l1 == m1, lane_f, float(LANES)), axis=-1, keepdims=True)
    l2 = jnp.where(lane_f == i1, -jnp.inf, l1)
    m2 = jnp.max(l2, axis=-1, keepdims=True)
    i2 = jnp.min(jnp.where(l2 == m2, lane_f, float(LANES)), axis=-1, keepdims=True)
    e2 = jnp.exp(m2 - m1)
    den = 1.0 + e2
    o_ref[...] = jnp.where(lane == 0, i1, jnp.where(lane == 1, i2, jnp.where(lane == 2, 1.0 / den,
                           jnp.where(lane == 3, e2 / den, 0.0))))


def _router(h2t, rw_hi, rw_lo, rb, n_experts):
    n = h2t.shape[0] // TOKEN_TILE_ROWS
    tm = 512 if n % 512 == 0 else n
    full = lambda shp: pl.BlockSpec(shp, lambda i: (0,) * len(shp))
    return pl.pallas_call(
        functools.partial(_router_kernel, n_experts=n_experts),
        grid=(n // tm,),
        in_specs=[pl.BlockSpec((tm * TOKEN_TILE_ROWS, LANES), lambda i: (i, 0)), full(rw_hi.shape),
                  full(rw_lo.shape), full(rb.shape)],
        out_specs=pl.BlockSpec((tm, LANES), lambda i: (i, 0)),
        out_shape=jax.ShapeDtypeStruct((n, LANES), F32),
        compiler_params=_cparams(("arbitrary",)),
        name="moe_router",
    )(h2t, rw_hi, rw_lo, rb)


def _gather_kernel(idx_ref, src_ref, out_ref, sem, *, rows_per_step):
    base = pl.program_id(0) * rows_per_step

    def row_copy(j):
        return pltpu.make_async_copy(src_ref.at[idx_ref[base + j]], out_ref.at[j], sem)

    def start(j, c):
        row_copy(j).start()
        return c

    def wait(j, c):
        row_copy(j).wait()
        return c

    lax.fori_loop(0, rows_per_step, start, 0, unroll=8)
    lax.fori_loop(0, rows_per_step, wait, 0, unroll=8)


def _gather_rows(src_t, idx, rows_per_step=512):
    m = idx.shape[0]
    while m % rows_per_step:
        rows_per_step //= 2
    src3 = src_t.reshape(-1, TOKEN_TILE_ROWS, LANES)
    gs = pltpu.PrefetchScalarGridSpec(
        num_scalar_prefetch=1,
        grid=(m // rows_per_step,),
        in_specs=[pl.BlockSpec(memory_space=pl.ANY)],
        out_specs=pl.BlockSpec((rows_per_step, TOKEN_TILE_ROWS, LANES), lambda i, idx_ref: (i, 0, 0)),
        scratch_shapes=[pltpu.SemaphoreType.DMA(())],
    )
    out = pl.pallas_call(
        functools.partial(_gather_kernel, rows_per_step=rows_per_step),
        grid_spec=gs,
        out_shape=jax.ShapeDtypeStruct((m, TOKEN_TILE_ROWS, LANES), src_t.dtype),
        compiler_params=pltpu.CompilerParams(dimension_semantics=("arbitrary",), vmem_limit_bytes=VMEM_LIMIT),
        name="gather_rows",
    )(idx, src3)
    return out.reshape(m * TOKEN_TILE_ROWS, LANES)


def _moe_block_kernel(be_ref, nu_ref, x_ref, wg_ref, wu_ref, wd_ref, o_ref, *, f_tile):
    i = pl.program_id(0)

    @pl.when(i < nu_ref[0])
    def _():
        hb = _untile(x_ref, MOE_BLOCK).astype(BF)
        d_ff = wg_ref.shape[1]
        acc = jnp.zeros((MOE_BLOCK, wd_ref.shape[1]), F32)
        for f in range(d_ff // f_tile):
            fs = slice(f * f_tile, (f + 1) * f_tile)
            gp = _dot(hb, wg_ref[:, fs])
            up = _dot(hb, wu_ref[:, fs])
            act = gp * jax.nn.sigmoid(gp) * up
            acc = acc + _dot(act.astype(BF), wd_ref[fs, :])
        _store_tiled(o_ref, acc)

    @pl.when(i >= nu_ref[0])
    def _():
        o_ref[...] = jnp.zeros(o_ref.shape, F32)


def _moe_blocks(blk_e, n_used, xb_t, wg, wu, wd):
    nb = xb_t.shape[0] // (MOE_BLOCK * TOKEN_TILE_ROWS)
    d, d_ff = wg.shape[1], wg.shape[2]
    blk = pl.BlockSpec((MOE_BLOCK * TOKEN_TILE_ROWS, LANES), lambda i, be, nu: (i, 0))
    gs = pltpu.PrefetchScalarGridSpec(
        num_scalar_prefetch=2,
        grid=(nb,),
        in_specs=[blk,
                  pl.BlockSpec((None, d, d_ff), lambda i, be, nu: (be[i], 0, 0)),
                  pl.BlockSpec((None, d, d_ff), lambda i, be, nu: (be[i], 0, 0)),
                  pl.BlockSpec((None, d_ff, d), lambda i, be, nu: (be[i], 0, 0))],
        out_specs=blk,
    )
    return pl.pallas_call(
        functools.partial(_moe_block_kernel, f_tile=_ff_tile(d_ff)),
        grid_spec=gs,
        out_shape=jax.ShapeDtypeStruct(xb_t.shape, F32),
        compiler_params=_cparams(("arbitrary",)),
        name="moe_blocks",
    )(blk_e, n_used, xb_t, wg, wu, wd)


def _combine_kernel(ya0_ref, ya1_ref, r_ref, y_ref, g2_ref, o_ref):
    tm = o_ref.shape[0]
    w1 = r_ref[:, 2:3]
    w2 = r_ref[:, 3:4]
    o_ref[...] = y_ref[...] + g2_ref[...] * (w1 * _untile(ya0_ref, tm) + w2 * _untile(ya1_ref, tm))


def _moe_combine(ya_t, route, y, gate2, *, per_batch, tokens_per_batch):
    n, d = y.shape
    tm = 512 if per_batch else n
    nt = n // tm
    row = lambda w: pl.BlockSpec((tm, w), lambda i: (i, 0))
    return pl.pallas_call(
        _combine_kernel,
        grid=(nt,),
        in_specs=[pl.BlockSpec((tm * TOKEN_TILE_ROWS, LANES), lambda i: (i, 0)),
                  pl.BlockSpec((tm * TOKEN_TILE_ROWS, LANES), lambda i: (i + nt, 0)), row(LANES), row(d),
                  _mod_spec(per_batch, tm, d, tokens_per_batch)],
        out_specs=row(d),
        out_shape=jax.ShapeDtypeStruct((n, d), F32),
        compiler_params=_cparams(("arbitrary",)),
        name="moe_combine",
    )(ya_t, ya_t, route, y, gate2)


def _moe_ffn(h2t, y, gate2, rw_hi, rw_lo, rb, wg, wu, wd, *, per_batch, tokens_per_batch):
    n, d = y.shape
    n_experts = wg.shape[0]
    route = _router(h2t, rw_hi, rw_lo, rb, n_experts)
    flat_e = route[:, 0:TOP_K].astype(jnp.int32).reshape(-1)
    onehot = (flat_e[:, None] == jnp.arange(n_experts, dtype=jnp.int32)[None, :]).astype(jnp.int32)
    csum = jnp.cumsum(onehot, axis=0)
    rank = jnp.sum(onehot * csum, axis=1) - 1
    counts = csum[-1]
    padded = (counts + MOE_BLOCK - 1) // MOE_BLOCK * MOE_BLOCK
    pad_end = jnp.cumsum(padded)
    pad_start = pad_end - padded
    slot = (pad_start[flat_e] + rank).astype(jnp.int32)
    nb = -(-(n * TOP_K) // MOE_BLOCK) + n_experts
    src = jnp.zeros((nb * MOE_BLOCK,), jnp.int32).at[slot].set(jnp.arange(n * TOP_K, dtype=jnp.int32) // TOP_K)
    blk_e = jnp.minimum(jnp.sum(pad_end[None, :] <= (jnp.arange(nb) * MOE_BLOCK)[:, None], axis=-1),
                        n_experts - 1).astype(jnp.int32)
    n_used = (pad_end[-1:] // MOE_BLOCK).astype(jnp.int32)
    xb = _gather_rows(h2t, src)
    yb = _moe_blocks(blk_e, n_used, xb, wg, wu, wd)
    ya = _gather_rows(yb, jnp.concatenate([slot[0::TOP_K], slot[1::TOP_K]]))
    return _moe_combine(ya, route, y, gate2, per_batch=per_batch, tokens_per_batch=tokens_per_batch)


def _rope_tables(pos):
    half = HEAD_DIM // 2
    inv = ROPE_THETA ** (-jnp.arange(half, dtype=F32) / half)
    ang = pos.astype(F32)[:, None] * inv[None, :]
    cos, sin = jnp.cos(ang), jnp.sin(ang)
    cos_l = jnp.concatenate([cos, cos] * (LANES // HEAD_DIM), axis=1)
    sin_l = jnp.concatenate([-sin, sin] * (LANES // HEAD_DIM), axis=1)
    return cos_l, sin_l


def _cover_matrix(n_rows, n_cmp, n_slc):
    cs = np.arange(n_cmp) * CMP_STRIDE
    ss = np.arange(n_slc) * SLC_BLOCK
    cov = np.clip(np.minimum(cs[:, None] + CMP_BLOCK, ss[None, :] + SLC_BLOCK)
                  - np.maximum(cs[:, None], ss[None, :]), 0, None).astype(np.float32) / CMP_BLOCK
    out = np.zeros((n_rows, LANES), np.float32)
    out[:n_cmp, :n_slc] = cov
    return jnp.asarray(out, BF)


def _expand_matrix(n_keys):
    blk = np.arange(LANES)[:, None]
    key = np.arange(n_keys)[None, :] // SLC_BLOCK
    return jnp.asarray((blk == key).astype(np.float32), BF)


def _piece_row_permutation(page):
    per_page = page // CMP_STRIDE
    r = np.arange(page)
    src = (r % per_page) * CMP_STRIDE + r // per_page
    return jnp.asarray((src[:, None] == np.arange(page)[None, :]).astype(np.float32), BF)


def _block_diag_mean(width):
    idx = np.arange(width) // HEAD_DIM
    return jnp.asarray((idx[:, None] == idx[None, :]).astype(np.float32) / HEAD_DIM, BF)


def _compress_weights(cmp_w_l, cmp_pe_l):
    z = jnp.zeros_like(cmp_w_l)
    wbig = jnp.concatenate([jnp.concatenate([cmp_w_l, z], axis=3), jnp.concatenate([z, cmp_w_l], axis=3)],
                           axis=2).astype(BF)
    pebig = jnp.concatenate([cmp_pe_l, cmp_pe_l], axis=2)
    return wbig, pebig


def kernel(x_prompt, x_sample, cache_cmp, cache_slc, state_win, page_table, c_prompt, c_sample, norm_mix_g, norm_ffn_g, w_ada, b_ada, w_in, w_spatial, b_spatial, q_norm_g, k_norm_g, cmp_pe, cmp_w, out_norm_g, w_out, ffn_w_gate, ffn_w_up, ffn_w_down, router_w, router_b, moe_w_gate, moe_w_up, moe_w_down):
    batch, t, d = x_prompt.shape
    db, tn, _ = x_sample.shape
    depth = w_ada.shape[0]
    n_pool, page = cache_cmp.shape[1], cache_cmp.shape[2]
    n_pages = page_table.shape[1]
    past = n_pages * page
    wb = state_win.shape[2]
    d_a = d // 2
    a_groups = d_a // HEAD_DIM
    assert tn == 1 and past % CHUNK == 0 and t % KEY_TILE == 0 and d_a % LANES == 0
    assert d - d_a == B_KV_HEADS * GQA_R * HEAD_DIM and B_KV_HEADS * HEAD_DIM == LANES
    n_p = batch * t
    kvw = 2 * LANES
    p_main = 2 * d_a + (d - d_a) + N_BRANCH * kvw
    n_gate = w_in.shape[2] - p_main
    qpos_s = past

    m_rows = batch + db
    m_pad = -(-m_rows // 8) * 8
    c_all = jnp.concatenate([c_prompt, c_sample, jnp.zeros((m_pad - m_rows, d), F32)], axis=0)
    mods = _adaln(c_all, w_ada, b_ada)

    bd = _block_diag_mean(d_a)
    cos_p, sin_p = _rope_tables(jnp.arange(t, dtype=jnp.int32))
    cos_s, sin_s = _rope_tables(jnp.full((1,), qpos_s, jnp.int32))
    np_p = t // CMP_STRIDE
    cover_p = _cover_matrix(np_p, np_p - 1, t // SLC_BLOCK)
    expand_p = _expand_matrix(t)
    np_s = past // CMP_STRIDE
    ns_s = -(-(past + tn) // SLC_BLOCK)
    cover_s = _cover_matrix(np_s, np_s - 1, ns_s)
    expand_s = _expand_matrix(past)
    tril = jnp.tril(jnp.ones((CHUNK, CHUNK), dtype=bool))

    cache_cmp4 = jnp.transpose(cache_cmp, (0, 1, 3, 4, 5, 2)).reshape(depth, n_pool, kvw, page)
    perm_s = _piece_row_permutation(page)
    cache_slc4 = jnp.transpose(cache_slc, (0, 1, 3, 4, 5, 2)).reshape(depth, n_pool, kvw, page)
    state_win4 = jnp.transpose(state_win, (0, 1, 3, 4, 5, 2)).reshape(depth, db, kvw, wb)

    yp = x_prompt.reshape(n_p, d)
    ys = x_sample.reshape(db, d)
    outs = {k: [] for k in ("cmp_p", "cmp_s", "slc_p", "slc_s", "win_p", "win_s", "chv_s")}
    for l in range(depth):
        mod_p = [mods[l, :batch, k * d:(k + 1) * d].reshape(batch, 1, d) for k in range(6)]
        mod_s = [mods[l, batch:batch + db, k * d:(k + 1) * d] for k in range(6)]
        w_in_bf = jnp.concatenate([w_in[l], jnp.zeros((d, LANES - n_gate), F32)], axis=1).astype(BF)
        qg = jnp.tile(q_norm_g[l], (d - d_a) // HEAD_DIM)[None, :]
        kg = jnp.tile(k_norm_g[l], (1, LANES // HEAD_DIM))
        ws_p = jnp.where(tril, w_spatial[l], 0.0).astype(BF)
        bs_p = jnp.repeat(b_spatial[l].T, HEAD_DIM, axis=1)
        ws_s = jnp.repeat(w_spatial[l][:, 0, 0], HEAD_DIM)[None, :]
        bs_s = jnp.repeat(b_spatial[l][:, 0], HEAD_DIM)[None, :]
        wbig, pebig = _compress_weights(cmp_w[l], cmp_pe[l])
        out_g = out_norm_g[l].reshape(1, d)
        wout_bf = w_out[l].astype(BF)
        ng1 = norm_mix_g[l][None, :]
        ng2 = norm_ffn_g[l][None, :]

        a, q, cmp_r, slc_r, win_r, slc_bf, win_bf, gates = _pre_mix(
            yp, ng1, mod_p[0], mod_p[1], w_in_bf, cos_p, sin_p, bd, qg, kg, ws_p, bs_p,
            chunked=True, tokens_per_batch=t)
        kc, vc = _compress_prompt(cmp_r, wbig, pebig, batch)
        o = _nsa_prompt(q, gates, kc, vc, slc_bf, win_bf, cover_p, expand_p, batch)
        yp, h2p = _post_mix(a, o, yp, mod_p[2], out_g, wout_bf, bd, ng2, mod_p[3], mod_p[4],
                            per_batch=True, tokens_per_batch=t, tiled_h2=(l % 2 == 1))
        outs["cmp_p"].append(cmp_r.reshape(batch, t, 2, B_KV_HEADS, HEAD_DIM))
        outs["slc_p"].append(slc_r.reshape(batch, t, 2, B_KV_HEADS, HEAD_DIM))
        wbp = min(WINDOW, t)
        outs["win_p"].append(win_r.reshape(batch, t, 2, B_KV_HEADS, HEAD_DIM)[:, t - wbp:])

        a_s, q_s, cmp_s, slc_s, win_s, gates_s, v_s = _pre_mix(
            ys, ng1, mod_s[0], mod_s[1], w_in_bf, cos_s, sin_s, bd, qg, kg, ws_s, bs_s,
            chunked=False, tokens_per_batch=1)
        q3 = q_s.reshape(db, 1, d - d_a)
        ocmp, imp = _s_cmp(page_table, q3, cache_cmp4, l, perm_s, wbig, pebig, cover_s, qpos_s)
        mask = _s_select(imp[:, 0:B_KV_HEADS, :].reshape(db * B_KV_HEADS, LANES), expand_s, qpos_s)
        o_s = _s_att(page_table, q3, gates_s.reshape(db, 1, LANES), mask.reshape(db, B_KV_HEADS, past), ocmp,
                     slc_s.reshape(db, 1, kvw), win_s.reshape(db, 1, kvw), state_win4, cache_slc4, l,
                     qpos_s, past)
        ys, h2s = _post_mix(a_s, o_s.reshape(db, d - d_a), ys, mod_s[2], out_g, wout_bf, bd, ng2, mod_s[3],
                            mod_s[4], per_batch=False, tokens_per_batch=1, tiled_h2=(l % 2 == 1))
        outs["cmp_s"].append(cmp_s.reshape(db, tn, 2, B_KV_HEADS, HEAD_DIM))
        outs["slc_s"].append(slc_s.reshape(db, tn, 2, B_KV_HEADS, HEAD_DIM))
        win_all = jnp.concatenate([state_win[l], win_s.reshape(db, tn, 2, B_KV_HEADS, HEAD_DIM)], axis=1)
        outs["win_s"].append(win_all[:, tn:])
        outs["chv_s"].append(v_s.reshape(db, tn, d_a))

        i = l // 2
        if l % 2 == 0:
            wg, wu, wd = ffn_w_gate[i].astype(BF), ffn_w_up[i].astype(BF), ffn_w_down[i].astype(BF)
            yp = _ffn_dense(h2p, yp, mod_p[5], wg, wu, wd, per_batch=True, tokens_per_batch=t)
            ys = _ffn_dense(h2s, ys, mod_s[5], wg, wu, wd, per_batch=False, tokens_per_batch=1)
        else:
            wg, wu, wd = moe_w_gate[i].astype(BF), moe_w_up[i].astype(BF), moe_w_down[i].astype(BF)
            n_exp = router_w.shape[2]
            rw = jnp.concatenate([router_w[i], jnp.zeros((d, LANES - n_exp), F32)], axis=1)
            rw_hi = rw.astype(BF)
            rw_lo = (rw - rw_hi.astype(F32)).astype(BF)
            rb = jnp.concatenate([router_b[i], jnp.zeros((LANES - n_exp,), F32)])[None, :]
            yp = _moe_ffn(h2p, yp, mod_p[5], rw_hi, rw_lo, rb, wg, wu, wd, per_batch=True, tokens_per_batch=t)
            ys = _moe_ffn(h2s, ys, mod_s[5], rw_hi, rw_lo, rb, wg, wu, wd, per_batch=False, tokens_per_batch=1)

    st = lambda k: jnp.stack(outs[k])
    return (yp.reshape(batch, t, d), ys.reshape(db, tn, d), st("cmp_p"), st("cmp_s"), st("slc_p"), st("slc_s"),
            st("win_p"), st("win_s"), st("chv_s"))
```

```python
import functools

import numpy as np
import jax
import jax.numpy as jnp
from jax import lax
from jax.experimental import pallas as pl
from jax.experimental.pallas import tpu as pltpu

F32 = jnp.float32
BF = jnp.bfloat16

HEAD_DIM = 64
B_KV_HEADS = 2
GQA_R = 4
N_BRANCH = 3
CHUNK = 128
CMP_BLOCK = 32
CMP_STRIDE = 16
SLC_BLOCK = 64
N_SELECT = 16
WINDOW = 512
Q_BLOCK = 128
ROPE_THETA = 10000.0
TOP_K = 2
MOE_BLOCK = 128
NEG_INF = -1e30
FORCE_SCORE = 1e30
EPS = 1e-6
LOG2_E = 1.4426950408889634

LANES = 128
TOKEN_TILE_ROWS = 8
SAMPLES_PER_STEP = 2
KEY_TILE = 512
VMEM_LIMIT = 56 * 1024 * 1024


def _cparams(sem):
    return pltpu.CompilerParams(dimension_semantics=sem, vmem_limit_bytes=VMEM_LIMIT)


def _dot(a, b):
    return jnp.dot(a, b, preferred_element_type=F32)


def _dot_nt(a, b):
    return lax.dot_general(a, b, (((1,), (1,)), ((), ())), preferred_element_type=F32)


def _split_dot(x, w):
    hi = x.astype(BF)
    lo = (x - hi.astype(F32)).astype(BF)
    return _dot(hi, w) + _dot(lo, w)


def _ada_kernel(c_ref, w_ref, b_ref, o_ref):
    c = c_ref[...]
    s = c * jax.nn.sigmoid(c)
    o_ref[0] = _dot(s.astype(BF), w_ref[0].astype(BF)) + b_ref[0]


def _adaln(c_all, w_ada, b_ada):
    depth, d, n6 = w_ada.shape
    m = c_all.shape[0]
    tn = 1536
    return pl.pallas_call(
        _ada_kernel,
        grid=(depth, n6 // tn),
        in_specs=[pl.BlockSpec((m, d), lambda l, j: (0, 0)),
                  pl.BlockSpec((1, d, tn), lambda l, j: (l, 0, j)),
                  pl.BlockSpec((1, 1, tn), lambda l, j: (l, 0, j))],
        out_specs=pl.BlockSpec((1, m, tn), lambda l, j: (l, 0, j)),
        out_shape=jax.ShapeDtypeStruct((depth, m, n6), F32),
        compiler_params=_cparams(("arbitrary", "arbitrary")),
        name="adaln",
    )(c_all, w_ada, b_ada.reshape(depth, 1, n6))


def _pre_kernel(x_ref, g_ref, sh_ref, sc_ref, w_ref, cos_ref, sin_ref, bd_ref, qg_ref, kg_ref,
                ws_ref, bs_ref, *out_refs, chunked, d_a, d_b):
    if chunked:
        a_ref, q_ref, cmp_ref, slcb_ref, winb_ref, gate_ref, cmpt_ref, slct_ref, wint_ref = out_refs[N_BRANCH:]
        row_refs = (cmp_ref, None, None)
        bf_refs = (None, slcb_ref, winb_ref)
        t_refs = (cmpt_ref, slct_ref, wint_ref)
    else:
        a_ref, q_ref, cmp_ref, slc_ref, win_ref, gate_ref, v_ref = out_refs
        row_refs = (cmp_ref, slc_ref, win_ref)
        bf_refs = t_refs = (None, None, None)
    tm = x_ref.shape[0]
    x = x_ref[...]
    ms = jnp.mean(x * x, axis=-1, keepdims=True)
    h = x * lax.rsqrt(ms + EPS) * g_ref[...]
    h = h * (1.0 + sc_ref[...]) + sh_ref[...]
    z = _dot(h.astype(BF), w_ref[...])
    uv = jax.nn.gelu(z[:, :2 * d_a])
    u = uv[:, :d_a]
    v = uv[:, d_a:]
    bd = bd_ref[...]
    mu = _split_dot(v, bd)
    vc = v - mu
    vn = vc * lax.rsqrt(_split_dot(vc * vc, bd) + EPS)

    cos = cos_ref[...]
    sin = sin_ref[...]
    lane = lax.broadcasted_iota(jnp.int32, (1, LANES), 1)
    first_half = (lane % HEAD_DIM) < (HEAD_DIM // 2)

    def rope(t):
        swapped = jnp.where(first_half, pltpu.roll(t, LANES - HEAD_DIM // 2, 1),
                            pltpu.roll(t, HEAD_DIM // 2, 1))
        return t * cos + swapped * sin

    o = 2 * d_a
    zq = z[:, o:o + d_b]
    qn = zq * lax.rsqrt(_split_dot(zq * zq, bd) + EPS) * qg_ref[...]
    for c in range(d_b // LANES):
        q_ref[:, c * LANES:(c + 1) * LANES] = rope(qn[:, c * LANES:(c + 1) * LANES]) * (HEAD_DIM ** -0.5)
    o += d_b
    bd1 = bd_ref[0:LANES, 0:LANES]
    for br in range(N_BRANCH):
        kp = z[:, o:o + LANES]
        vp = z[:, o + LANES:o + 2 * LANES]
        kn = kp * lax.rsqrt(_split_dot(kp * kp, bd1) + EPS) * kg_ref[br:br + 1, :]
        kr = rope(kn)
        if row_refs[br] is not None:
            row_refs[br][:, 0:LANES] = kr
            row_refs[br][:, LANES:2 * LANES] = vp
        if t_refs[br] is not None:
            t_refs[br][0:LANES, :] = jnp.transpose(kr)
            t_refs[br][LANES:2 * LANES, :] = jnp.transpose(vp)
        if bf_refs[br] is not None:
            bf_refs[br][:, 0:LANES] = kr.astype(BF)
            bf_refs[br][:, LANES:2 * LANES] = vp.astype(BF)
        o += 2 * LANES
    gate_ref[...] = jax.nn.sigmoid(z[:, o:o + LANES])

    if chunked:
        low = lane < HEAD_DIM
        for c in range(tm // CHUNK):
            rs = slice(c * CHUNK, (c + 1) * CHUNK)
            for j in range(d_a // LANES):
                cs = slice(j * LANES, (j + 1) * LANES)
                vpair = vn[rs, cs].astype(BF)
                m0 = _dot(ws_ref[2 * j], vpair)
                m1 = _dot(ws_ref[2 * j + 1], vpair)
                mixed = jnp.where(low, m0, m1) + bs_ref[:, cs]
                a_ref[rs, cs] = u[rs, cs] * mixed
    else:
        a_ref[...] = u * (vn * ws_ref[...] + bs_ref[...])
        v_ref[...] = vn


def _pre_mix(x, norm_g, shift, scale, w_in_bf, cos, sin, bd, qg, kg, ws, bs, *, chunked, tokens_per_batch,
             kv_bufs=(), layer=0):
    n, d = x.shape
    d_a = d_b = d // 2
    tm = 256 if chunked else n
    nt = n // tm
    pw = w_in_bf.shape[1]
    full = lambda shp: pl.BlockSpec(shp, lambda i: (0,) * len(shp))
    row = lambda w: pl.BlockSpec((tm, w), lambda i: (i, 0))
    extra_in, aliases = [], {}
    if chunked:
        tpb = tokens_per_batch // tm
        mod_spec = pl.BlockSpec((None, 1, d), lambda i: (i // tpb, 0, 0))
        rot_spec = pl.BlockSpec((tm, LANES), lambda i: (i % tpb, 0))
        ws_spec, bs_spec = full(ws.shape), full(bs.shape)
        out_shape = [jax.ShapeDtypeStruct((n, d_a), F32), jax.ShapeDtypeStruct((n, d_b), F32),
                     jax.ShapeDtypeStruct((n, 2 * LANES), F32), jax.ShapeDtypeStruct((n, 2 * LANES), BF),
                     jax.ShapeDtypeStruct((n, 2 * LANES), BF), jax.ShapeDtypeStruct((n, LANES), F32)]
        out_specs = [row(d_a), row(d_b), row(2 * LANES), row(2 * LANES), row(2 * LANES), row(LANES)]
        t_spec = pl.BlockSpec((None, None, 2 * LANES, tm), lambda i: (layer, i // tpb, 0, i % tpb))
        n_in = 12
        for j, buf in enumerate(kv_bufs):
            extra_in.append(pl.BlockSpec(memory_space=pl.ANY))
            aliases[n_in + j] = len(out_shape)
            out_shape.append(jax.ShapeDtypeStruct(buf.shape, F32))
            out_specs.append(t_spec)
    else:
        mod_spec = row(d)
        rot_spec = full((1, LANES))
        ws_spec, bs_spec = full(ws.shape), full(bs.shape)
        out_shape = [jax.ShapeDtypeStruct((n, d_a), F32), jax.ShapeDtypeStruct((n, d_b), F32),
                     jax.ShapeDtypeStruct((n, 2 * LANES), F32), jax.ShapeDtypeStruct((n, 2 * LANES), F32),
                     jax.ShapeDtypeStruct((n, 2 * LANES), F32), jax.ShapeDtypeStruct((n, LANES), F32),
                     jax.ShapeDtypeStruct((n, d_a), F32)]
        out_specs = [row(d_a), row(d_b), row(2 * LANES), row(2 * LANES), row(2 * LANES), row(LANES), row(d_a)]
    return pl.pallas_call(
        functools.partial(_pre_kernel, chunked=chunked, d_a=d_a, d_b=d_b),
        grid=(nt,),
        in_specs=[row(d), full((1, d)), mod_spec, mod_spec, full((d, pw)), rot_spec, rot_spec,
                  full(bd.shape), full(qg.shape), full(kg.shape), ws_spec, bs_spec] + extra_in,
        out_specs=out_specs,
        out_shape=out_shape,
        input_output_aliases=aliases,
        compiler_params=_cparams(("arbitrary",)),
        name="pre_mix_prompt" if chunked else "pre_mix_sample",
    )(x, norm_g, shift, scale, w_in_bf, cos, sin, bd, qg, kg, ws, bs, *kv_bufs)


def _compress_pieces(load_piece_rows, w_ref, pe_ref, kv, n_piece, seg=None):
    assert CMP_BLOCK == 2 * CMP_STRIDE
    seg = n_piece if seg is None else seg
    acc_a = jnp.zeros((n_piece, LANES), F32)
    acc_b = jnp.zeros((n_piece, LANES), F32)
    for l in range(CMP_STRIDE):
        xl = load_piece_rows(l)
        l2 = CMP_STRIDE + l
        acc_a = acc_a + _dot((xl + pe_ref[kv, l:l + 1, :]).astype(BF), w_ref[kv, l])
        acc_b = acc_b + _dot((xl + pe_ref[kv, l2:l2 + 1, :]).astype(BF), w_ref[kv, l2])
    shifted = pltpu.roll(acc_b, n_piece - 1, 0)
    rows = lax.broadcasted_iota(jnp.int32, (n_piece, 1), 0)
    return jnp.where(rows % seg < seg - 1, acc_a + shifted, 0.0)


def _compress_kernel(xk_ref, xv_ref, w_ref, pe_ref, kc_ref, vc_ref):
    n_piece = kc_ref.shape[0]
    for kv, (x_ref, o_ref) in enumerate(((xk_ref, kc_ref), (xv_ref, vc_ref))):
        out = _compress_pieces(lambda l: x_ref[pl.ds(l, n_piece, stride=CMP_STRIDE), :], w_ref, pe_ref, kv, n_piece)
        o_ref[...] = out.astype(BF)


def _compress_prompt(cmp_rows, wbig, pebig, batch):
    n = cmp_rows.shape[0]
    t = n // batch
    n_piece = t // CMP_STRIDE
    out = jax.ShapeDtypeStruct((batch * n_piece, LANES), BF)
    return pl.pallas_call(
        _compress_kernel,
        grid=(batch,),
        in_specs=[pl.BlockSpec((t, LANES), lambda b: (b, 0)),
                  pl.BlockSpec((t, LANES), lambda b: (b, 1)),
                  pl.BlockSpec(wbig.shape, lambda b: (0, 0, 0, 0)),
                  pl.BlockSpec(pebig.shape, lambda b: (0, 0, 0))],
        out_specs=[pl.BlockSpec((n_piece, LANES), lambda b: (b, 0))] * 2,
        out_shape=[out, out],
        compiler_params=_cparams(("arbitrary",)),
        name="compress_prompt",
    )(cmp_rows, cmp_rows, wbig, pebig)


def _group_queries(q, g):
    m = q.shape[0]
    low = lax.broadcasted_iota(jnp.int32, (m, LANES), 1) < HEAD_DIM
    keep = low if g == 0 else jnp.logical_not(low)
    parts = []
    for r in range(GQA_R):
        h = GQA_R * g + r
        t = q[:, (h // 2) * LANES:(h // 2 + 1) * LANES]
        if h % 2 != g:
            t = pltpu.roll(t, HEAD_DIM, 1)
        parts.append(jnp.where(keep, t, 0.0))
    return parts


def _top_select(imp, blk_f):
    sel = jnp.zeros(imp.shape, F32)
    work = imp
    for _ in range(N_SELECT):
        mx = jnp.max(work, axis=-1, keepdims=True)
        first = jnp.min(jnp.where(work == mx, blk_f, float(LANES)), axis=-1, keepdims=True)
        hit = blk_f == first
        sel = jnp.where(hit, 1.0, sel)
        work = jnp.where(hit, -jnp.inf, work)
    return sel


def _block_scores(imp, qpos):
    blk = lax.broadcasted_iota(jnp.int32, imp.shape, 1)
    cur = qpos // SLC_BLOCK
    forced = (blk == 0) | (blk == cur) | (blk == cur - 1)
    valid = blk <= cur
    imp = jnp.where(forced, FORCE_SCORE, imp)
    imp = jnp.where(valid, imp, NEG_INF)
    return imp, blk.astype(F32), valid


def _merge_heads(res, m):
    low = lax.broadcasted_iota(jnp.int32, (m, LANES), 1) < HEAD_DIM
    cols = []
    for c in range(4):
        a, b = res[2 * c], res[2 * c + 1]
        if c // 2 == 0:
            cols.append(jnp.where(low, a, pltpu.roll(b, HEAD_DIM, 1)))
        else:
            cols.append(jnp.where(low, pltpu.roll(a, HEAD_DIM, 1), b))
    return jnp.concatenate(cols, axis=1)


def _nsa_prompt_kernel(q_ref, gate_ref, kc_ref, vc_ref, slc_ref, win_ref, cover_ref, e_ref, o_ref):
    i = pl.program_id(1)
    s0 = i * Q_BLOCK
    n_cmp = kc_ref.shape[0]
    rows4 = GQA_R * Q_BLOCK
    q = q_ref[...] * LOG2_E
    gates = gate_ref[...]
    tpos = s0 + lax.broadcasted_iota(jnp.int32, (Q_BLOCK, 1), 0)
    tpos4 = jnp.concatenate([tpos] * GQA_R, axis=0)
    tile4 = lambda x: jnp.concatenate([x] * GQA_R, axis=0)
    groups = range(B_KV_HEADS)
    qg = [jnp.concatenate(_group_queries(q, g), axis=0).astype(BF) for g in groups]

    kc = kc_ref[...]
    vc = vc_ref[...]
    c_last = lax.broadcasted_iota(jnp.int32, (1, n_cmp), 1) * CMP_STRIDE + (CMP_BLOCK - 1)
    c_bias = jnp.where(c_last <= tpos4, 0.0, NEG_INF)
    has_cmp = jnp.where(tpos4 >= CMP_BLOCK - 1, 1.0, 0.0)
    o_cmp, p_sums = [], []
    for g in groups:
        s = _dot_nt(qg[g], kc) + c_bias
        e = jnp.exp2(s - jnp.max(s, axis=-1, keepdims=True))
        p = e * (has_cmp / jnp.sum(e, axis=-1, keepdims=True))
        o_cmp.append(_dot(p.astype(BF), vc))
        p_sum = p[0:Q_BLOCK]
        for r in range(1, GQA_R):
            p_sum = p_sum + p[r * Q_BLOCK:(r + 1) * Q_BLOCK]
        p_sums.append(p_sum)

    imp = _split_dot(jnp.concatenate(p_sums, axis=0), cover_ref[...])
    imp, blk_f, valid = _block_scores(imp, jnp.concatenate([tpos] * B_KV_HEADS, axis=0))
    sel_all = jnp.where(valid, _top_select(imp, blk_f), 0.0).astype(BF)
    sel_b = [sel_all[g * Q_BLOCK:(g + 1) * Q_BLOCK] for g in groups]

    w_start = pl.multiple_of(jnp.maximum(s0 - WINDOW, 0), Q_BLOCK)
    w_len = WINDOW + Q_BLOCK
    wpos = w_start + lax.broadcasted_iota(jnp.int32, (1, w_len), 1)
    w_bias4 = tile4(jnp.where((wpos <= tpos) & (wpos > tpos - WINDOW), 0.0, NEG_INF))
    kw = win_ref[pl.ds(w_start, w_len), 0:LANES]
    vw = win_ref[pl.ds(w_start, w_len), LANES:2 * LANES]
    low_w = lax.broadcasted_iota(jnp.int32, (1, LANES), 1) < HEAD_DIM
    keep = [low_w, jnp.logical_not(low_w)]
    with_ones = lambda v, g: jnp.where(keep[g], v, jnp.ones_like(v))
    den_lane = [HEAD_DIM, 0]

    def normalise(acc, g):
        return acc / acc[:, den_lane[g]:den_lane[g] + 1]

    o_win = []
    for g in groups:
        sw = _dot_nt(qg[g], kw) + w_bias4
        ew = jnp.exp2(sw - jnp.max(sw, axis=-1, keepdims=True))
        o_win.append(_dot(ew.astype(BF), vw) / jnp.sum(ew, axis=-1, keepdims=True))

    def body(kt, carry):
        k0 = pl.multiple_of(kt * KEY_TILE, KEY_TILE)
        kk = slc_ref[pl.ds(k0, KEY_TILE), 0:LANES]
        vv = slc_ref[pl.ds(k0, KEY_TILE), LANES:2 * LANES]
        et = e_ref[:, pl.ds(k0, KEY_TILE)]
        kpos = k0 + lax.broadcasted_iota(jnp.int32, (1, KEY_TILE), 1)
        causal = kpos <= tpos
        out = []
        for g in groups:
            m, acc = carry[g]
            bias = jnp.where((_dot(sel_b[g], et) > 0.5) & causal, 0.0, NEG_INF)
            sc = _dot_nt(qg[g], kk) + tile4(bias)
            m_new = jnp.maximum(m, jnp.max(sc, axis=-1, keepdims=True))
            pe = jnp.exp2(sc - m_new)
            acc = jnp.exp2(m - m_new) * acc + _dot(pe.astype(BF), with_ones(vv, g))
            out.append((m_new, acc))
        return tuple(out)

    init = tuple((jnp.full((rows4, 1), NEG_INF, F32), jnp.zeros((rows4, LANES), F32)) for _ in groups)
    n_tiles = (s0 + Q_BLOCK + KEY_TILE - 1) // KEY_TILE
    fin = lax.fori_loop(0, n_tiles, body, init)

    res = []
    for g in groups:
        o_slc = normalise(fin[g][1], g)
        for r in range(GQA_R):
            h = GQA_R * g + r
            rs = slice(r * Q_BLOCK, (r + 1) * Q_BLOCK)
            res.append(gates[:, 3 * h:3 * h + 1] * o_cmp[g][rs] + gates[:, 3 * h + 1:3 * h + 2] * o_slc[rs]
                       + gates[:, 3 * h + 2:3 * h + 3] * o_win[g][rs])
    o_ref[...] = _merge_heads(res, Q_BLOCK)


def _nsa_prompt(q, gates, kc, vc, slc_bf, win_bf, cover, expand, batch):
    n, d_b = q.shape
    t = n // batch
    nq = t // Q_BLOCK
    n_cmp = kc.shape[0] // batch
    return pl.pallas_call(
        _nsa_prompt_kernel,
        grid=(batch, nq),
        in_specs=[pl.BlockSpec((Q_BLOCK, d_b), lambda b, i: (b * nq + i, 0)),
                  pl.BlockSpec((Q_BLOCK, LANES), lambda b, i: (b * nq + i, 0)),
                  pl.BlockSpec((n_cmp, LANES), lambda b, i: (b, 0)),
                  pl.BlockSpec((n_cmp, LANES), lambda b, i: (b, 0)),
                  pl.BlockSpec((t, 2 * LANES), lambda b, i: (b, 0)),
                  pl.BlockSpec((t, 2 * LANES), lambda b, i: (b, 0)),
                  pl.BlockSpec(cover.shape, lambda b, i: (0, 0)),
                  pl.BlockSpec(expand.shape, lambda b, i: (0, 0))],
        out_specs=pl.BlockSpec((Q_BLOCK, d_b), lambda b, i: (b * nq + i, 0)),
        out_shape=jax.ShapeDtypeStruct((n, d_b), F32),
        compiler_params=_cparams(("arbitrary", "arbitrary")),
        name="nsa_prompt",
    )(q, gates, kc, vc, slc_bf, win_bf, cover, expand)


def _sample_queries(q_row):
    sub = lax.broadcasted_iota(jnp.int32, (8, LANES), 0)
    out = jnp.zeros((8, LANES), F32)
    for g in range(B_KV_HEADS):
        for r, part in enumerate(_group_queries(q_row, g)):
            out = jnp.where(sub == GQA_R * g + r, jnp.broadcast_to(part, (8, LANES)), out)
    return out


def _s_cmp_kernel(pt_ref, q_ref, *refs, n_pages, n_samp, qpos):
    n_pg = n_samp * n_pages
    pages = refs[:n_pg]
    perm_ref, wbig_ref, pe_ref, cover_ref, ocmp_ref, imp_ref = refs[n_pg:]
    per_page = pages[0].shape[1] // CMP_STRIDE
    n_piece = n_pages * per_page

    def loader(kv):
        xp = [_dot_nt(perm_ref[...], pg[kv * LANES:(kv + 1) * LANES, :].astype(BF)) for pg in pages]
        return lambda l: jnp.concatenate([x[l * per_page:(l + 1) * per_page, :] for x in xp], axis=0)

    kc_all = _compress_pieces(loader(0), wbig_ref, pe_ref, 0, n_samp * n_piece, n_piece).astype(BF)
    vc_all = _compress_pieces(loader(1), wbig_ref, pe_ref, 1, n_samp * n_piece, n_piece).astype(BF)
    c_last = lax.broadcasted_iota(jnp.int32, (1, n_piece), 1) * CMP_STRIDE + (CMP_BLOCK - 1)
    cv = c_last <= qpos
    sub = lax.broadcasted_iota(jnp.int32, (8, n_piece), 0)
    for i in range(n_samp):
        kc = kc_all[i * n_piece:(i + 1) * n_piece]
        vc = vc_all[i * n_piece:(i + 1) * n_piece]
        qa = _sample_queries(q_ref[i])
        s = jnp.where(cv, _dot_nt(qa.astype(BF), kc), NEG_INF)
        e = jnp.where(cv, jnp.exp(s - jnp.max(s, axis=-1, keepdims=True)), 0.0)
        p = e / jnp.maximum(jnp.sum(e, axis=-1, keepdims=True), 1e-30)
        ocmp_ref[i] = _dot(p.astype(BF), vc)
        ps0 = jnp.sum(jnp.where(sub < GQA_R, p, 0.0), axis=0, keepdims=True)
        ps1 = jnp.sum(jnp.where(sub >= GQA_R, p, 0.0), axis=0, keepdims=True)
        p_sum = jnp.where(sub == 0, ps0, jnp.where(sub == 1, ps1, 0.0))
        imp_ref[i] = _split_dot(p_sum, cover_ref[...])


def _samples_per_step(db):
    return SAMPLES_PER_STEP if db % SAMPLES_PER_STEP == 0 else 1


def _page_specs(n_samp, n_pages, layer, rows, page, kvs):
    def spec(kv, i, j):
        if rows == LANES:
            return pl.BlockSpec((None, None, LANES, page), lambda b, pt: (layer, pt[b * n_samp + i, j], kv, 0))
        return pl.BlockSpec((None, None, rows, page), lambda b, pt: (layer, pt[b * n_samp + i, j], 0, 0))
    return [spec(kv, i, j) for kv in kvs for i in range(n_samp) for j in range(n_pages)]


def _s_cmp(page_table, q3, cache_l, layer, perm, wbig, pebig, cover, qpos):
    db = q3.shape[0]
    n_pages = page_table.shape[1]
    page = cache_l.shape[3]
    n_piece = n_pages * page // CMP_STRIDE
    n_samp = _samples_per_step(db)
    page_specs = _page_specs(n_samp, n_pages, layer, 2 * LANES, page, (0,))
    per_b = lambda r, w: pl.BlockSpec((n_samp, r, w), lambda b, pt: (b, 0, 0))
    gs = pltpu.PrefetchScalarGridSpec(
        num_scalar_prefetch=1,
        grid=(db // n_samp,),
        in_specs=[per_b(1, q3.shape[2])] + page_specs + [
            pl.BlockSpec(perm.shape, lambda b, pt: (0, 0)),
            pl.BlockSpec(wbig.shape, lambda b, pt: (0, 0, 0, 0)),
            pl.BlockSpec(pebig.shape, lambda b, pt: (0, 0, 0)),
            pl.BlockSpec(cover.shape, lambda b, pt: (0, 0))],
        out_specs=[per_b(8, LANES), per_b(8, LANES)],
    )
    assert cover.shape[0] == n_piece
    return pl.pallas_call(
        functools.partial(_s_cmp_kernel, n_pages=n_pages, n_samp=n_samp, qpos=qpos),
        grid_spec=gs,
        out_shape=[jax.ShapeDtypeStruct((db, 8, LANES), F32), jax.ShapeDtypeStruct((db, 8, LANES), F32)],
        compiler_params=_cparams(("arbitrary",)),
        name="nsa_sample_cmp",
    )(page_table, q3, *([cache_l] * (n_samp * n_pages)), perm, wbig, pebig, cover)


def _s_select_kernel(imp_ref, e_ref, o_ref, *, qpos):
    imp = imp_ref[...]
    pos = jnp.full((imp.shape[0], 1), qpos, jnp.int32)
    imp, blk_f, valid = _block_scores(imp, pos)
    sel = _top_select(imp, blk_f)
    sel_b = jnp.where(valid, sel, 0.0).astype(BF)
    o_ref[...] = _dot(sel_b, e_ref[...])


def _s_select(imp2, expand, qpos):
    m = imp2.shape[0]
    past = expand.shape[1]
    return pl.pallas_call(
        functools.partial(_s_select_kernel, qpos=qpos),
        grid=(1,),
        in_specs=[pl.BlockSpec(imp2.shape, lambda i: (0, 0)), pl.BlockSpec(expand.shape, lambda i: (0, 0))],
        out_specs=pl.BlockSpec((m, past), lambda i: (0, 0)),
        out_shape=jax.ShapeDtypeStruct((m, past), F32),
        compiler_params=_cparams(("arbitrary",)),
        name="nsa_sample_select",
    )(imp2, expand)


def _s_att_kernel(pt_ref, q_ref, gate_ref, mask_ref, ocmp_ref, slcn_ref, winn_ref, win_ref, *refs,
                  n_pages, n_samp, qpos, past):
    o_ref, wout_ref = refs[n_samp * n_pages + 1:]
    for i in range(n_samp):
        _s_att_one(q_ref.at[i], gate_ref.at[i], mask_ref.at[i], ocmp_ref.at[i], slcn_ref.at[i], winn_ref.at[i],
                   win_ref.at[i], refs[i * n_pages:(i + 1) * n_pages], o_ref.at[i], wout_ref.at[i],
                   qpos=qpos, past=past)


def _s_att_one(q_ref, gate_ref, mask_ref, ocmp_ref, slcn_ref, winn_ref, win_ref, pages, o_ref, wout_ref,
               *, qpos, past):
    page = pages[0].shape[1]
    qa = _sample_queries(q_ref[...])
    qb = qa.astype(BF)
    sub = lax.broadcasted_iota(jnp.int32, (8, 1), 0)
    mrow = jnp.where(sub < GQA_R, mask_ref[0:1, :], mask_ref[1:2, :])
    bias = jnp.where(mrow > 0.5, 0.0, NEG_INF)
    sc = jnp.concatenate([_dot(qb, pg[0:LANES, :].astype(BF)) for pg in pages], axis=1) + bias
    kn = slcn_ref[:, 0:LANES]
    vn = slcn_ref[:, LANES:2 * LANES]
    s_new = jnp.sum(qa * kn, axis=-1, keepdims=True)
    m = jnp.maximum(jnp.max(sc, axis=-1, keepdims=True), s_new)
    e = jnp.exp(sc - m)
    e_new = jnp.exp(s_new - m)
    acc = e_new * vn
    for j, pg in enumerate(pages):
        acc = acc + _dot_nt(e[:, j * page:(j + 1) * page].astype(BF), pg[LANES:2 * LANES, :].astype(BF))
    o_slc = acc / (jnp.sum(e, axis=-1, keepdims=True) + e_new)
    wb = win_ref.shape[1]
    wpos = (past - wb) + lax.broadcasted_iota(jnp.int32, (1, wb), 1)
    wvalid = (wpos <= qpos) & (wpos > qpos - WINDOW) & (wpos >= 0)
    sw = _dot(qb, win_ref[0:LANES, :].astype(BF)) + jnp.where(wvalid, 0.0, NEG_INF)
    kwn = winn_ref[:, 0:LANES]
    vwn = winn_ref[:, LANES:2 * LANES]
    sw_new = jnp.sum(qa * kwn, axis=-1, keepdims=True)
    mw = jnp.maximum(jnp.max(sw, axis=-1, keepdims=True), sw_new)
    ew = jnp.exp(sw - mw)
    ew_new = jnp.exp(sw_new - mw)
    o_win = (_dot_nt(ew.astype(BF), win_ref[LANES:2 * LANES, :].astype(BF)) + ew_new * vwn) / (
        jnp.sum(ew, axis=-1, keepdims=True) + ew_new)
    lane = lax.broadcasted_iota(jnp.int32, (8, LANES), 1)
    gt = jnp.broadcast_to(gate_ref[...], (8, LANES))
    gsel = [jnp.sum(jnp.where(lane == 3 * sub + br, gt, 0.0), axis=-1, keepdims=True) for br in range(N_BRANCH)]
    res = gsel[0] * ocmp_ref[...] + gsel[1] * o_slc + gsel[2] * o_win
    o_ref[...] = _merge_heads([res[h:h + 1, :] for h in range(8)], 1)
    rows = winn_ref.shape[1]
    diag = lax.broadcasted_iota(jnp.int32, (rows, rows), 0) == lax.broadcasted_iota(jnp.int32, (rows, rows), 1)
    col = jnp.sum(jnp.where(diag, jnp.broadcast_to(winn_ref[...], (rows, rows)), 0.0), axis=-1, keepdims=True)
    last = lax.broadcasted_iota(jnp.int32, (1, wb), 1) == wb - 1
    wout_ref[...] = jnp.where(last, col, pltpu.roll(win_ref[...], wb - 1, 1))


def _s_att(page_table, q3, gates3, mask3, ocmp, slc_new3, win_new3, win_l, win_buf, cache_l, layer, qpos, past):
    db = q3.shape[0]
    n_pages = page_table.shape[1]
    page = cache_l.shape[3]
    wb = win_l.shape[3]
    n_samp = _samples_per_step(db)
    page_specs = _page_specs(n_samp, n_pages, layer, 2 * LANES, page, (0,))
    per_b = lambda r, w: pl.BlockSpec((n_samp, r, w), lambda b, pt: (b, 0, 0))
    win_spec = pl.BlockSpec((None, n_samp, 2 * LANES, wb), lambda b, pt: (layer, b, 0, 0))
    in_specs = [per_b(1, q3.shape[2]), per_b(1, LANES), per_b(2, past), per_b(8, LANES), per_b(1, 2 * LANES),
                per_b(1, 2 * LANES), win_spec] + page_specs + [pl.BlockSpec(memory_space=pl.ANY)]
    gs = pltpu.PrefetchScalarGridSpec(
        num_scalar_prefetch=1,
        grid=(db // n_samp,),
        in_specs=in_specs,
        out_specs=[per_b(1, q3.shape[2]), win_spec],
    )
    return pl.pallas_call(
        functools.partial(_s_att_kernel, n_pages=n_pages, n_samp=n_samp, qpos=qpos, past=past),
        grid_spec=gs,
        out_shape=[jax.ShapeDtypeStruct(q3.shape, F32), jax.ShapeDtypeStruct(win_buf.shape, F32)],
        input_output_aliases={len(in_specs): 1},
        compiler_params=_cparams(("arbitrary",)),
        name="nsa_sample_att",
    )(page_table, q3, gates3, mask3, ocmp, slc_new3, win_new3, win_l, *([cache_l] * (n_samp * n_pages)), win_buf)


def _untile(ref, tm):
    return jnp.concatenate([ref[pl.ds(c, tm, stride=TOKEN_TILE_ROWS), :] for c in range(TOKEN_TILE_ROWS)], axis=1)


def _store_tiled(ref, val):
    tm = val.shape[0]
    for c in range(TOKEN_TILE_ROWS):
        ref[pl.ds(c, tm, stride=TOKEN_TILE_ROWS), :] = val[:, c * LANES:(c + 1) * LANES]


def _post_kernel(a_ref, o_ref, x_ref, g1_ref, og_ref, wout_ref, bd_ref, ng_ref, sh2_ref, sc2_ref, y_ref, h2_ref,
                 *, tiled_h2):
    bd = bd_ref[...]
    d_a = a_ref.shape[1]

    def head_norm(t, g):
        return t * lax.rsqrt(_split_dot(t * t, bd) + EPS) * g

    a = head_norm(a_ref[...], og_ref[:, :d_a])
    o = head_norm(o_ref[...], og_ref[:, d_a:])
    mix = _dot(a.astype(BF), wout_ref[:d_a, :]) + _dot(o.astype(BF), wout_ref[d_a:, :])
    y = x_ref[...] + g1_ref[...] * mix
    y_ref[...] = y
    ms = jnp.mean(y * y, axis=-1, keepdims=True)
    h2 = y * lax.rsqrt(ms + EPS) * ng_ref[...]
    h2 = h2 * (1.0 + sc2_ref[...]) + sh2_ref[...]
    if tiled_h2:
        _store_tiled(h2_ref, h2)
    else:
        h2_ref[...] = h2


def _mod_spec(per_batch, tm, d, tokens_per_batch):
    if per_batch:
        tpb = tokens_per_batch // tm
        return pl.BlockSpec((None, 1, d), lambda i: (i // tpb, 0, 0))
    return pl.BlockSpec((tm, d), lambda i: (i, 0))


def _post_mix(a, o, x, gate1, out_g, wout_bf, bd, ffn_g, shift2, scale2, *, per_batch, tokens_per_batch, tiled_h2):
    n, d = x.shape
    tm = 512 if per_batch else n
    full = lambda shp: pl.BlockSpec(shp, lambda i: (0,) * len(shp))
    row = lambda w: pl.BlockSpec((tm, w), lambda i: (i, 0))
    ms = _mod_spec(per_batch, tm, d, tokens_per_batch)
    if tiled_h2:
        assert d == TOKEN_TILE_ROWS * LANES
        h2_spec = pl.BlockSpec((tm * TOKEN_TILE_ROWS, LANES), lambda i: (i, 0))
        h2_shape = jax.ShapeDtypeStruct((n * TOKEN_TILE_ROWS, LANES), F32)
    else:
        h2_spec, h2_shape = row(d), jax.ShapeDtypeStruct((n, d), F32)
    return pl.pallas_call(
        functools.partial(_post_kernel, tiled_h2=tiled_h2),
        grid=(n // tm,),
        in_specs=[row(a.shape[1]), row(o.shape[1]), row(d), ms, full((1, d)), full(wout_bf.shape), full(bd.shape),
                  full((1, d)), ms, ms],
        out_specs=[row(d), h2_spec],
        out_shape=[jax.ShapeDtypeStruct((n, d), F32), h2_shape],
        compiler_params=_cparams(("arbitrary",)),
        name="post_mix",
    )(a, o, x, gate1, out_g, wout_bf, bd, ffn_g, shift2, scale2)


def _ffn_kernel(h_ref, y_ref, g2_ref, wg_ref, wu_ref, wd_ref, o_ref, *, f_tile):
    hb = h_ref[...].astype(BF)
    d_ff = wg_ref.shape[1]
    acc = jnp.zeros(o_ref.shape, F32)
    for f in range(d_ff // f_tile):
        fs = slice(f * f_tile, (f + 1) * f_tile)
        gp = _dot(hb, wg_ref[:, fs])
        up = _dot(hb, wu_ref[:, fs])
        act = gp * jax.nn.sigmoid(gp) * up
        acc = acc + _dot(act.astype(BF), wd_ref[fs, :])
    o_ref[...] = y_ref[...] + g2_ref[...] * acc


def _ff_tile(d_ff):
    for cand in (1408, 1024, 512, 256, 128):
        if d_ff % cand == 0:
            return cand
    return d_ff


def _ffn_dense(h2, y, gate2, wg, wu, wd, *, per_batch, tokens_per_batch):
    n, d = y.shape
    tm = 512 if per_batch else n
    full = lambda shp: pl.BlockSpec(shp, lambda i: (0,) * len(shp))
    row = lambda w: pl.BlockSpec((tm, w), lambda i: (i, 0))
    return pl.pallas_call(
        functools.partial(_ffn_kernel, f_tile=_ff_tile(wg.shape[1])),
        grid=(n // tm,),
        in_specs=[row(d), row(d), _mod_spec(per_batch, tm, d, tokens_per_batch), full(wg.shape), full(wu.shape),
                  full(wd.shape)],
        out_specs=row(d),
        out_shape=jax.ShapeDtypeStruct((n, d), F32),
        compiler_params=_cparams(("arbitrary",)),
        name="ffn_dense",
    )(h2, y, gate2, wg, wu, wd)


def _router_kernel(h_ref, rwh_ref, rwl_ref, rb_ref, o_ref, *, n_experts):
    h = _untile(h_ref, o_ref.shape[0])
    hi = h.astype(BF)
    lo = (h - hi.astype(F32)).astype(BF)
    logits = _dot(hi, rwh_ref[...]) + _dot(lo, rwh_ref[...]) + _dot(hi, rwl_ref[...]) + rb_ref[...]
    lane = lax.broadcasted_iota(jnp.int32, logits.shape, 1)
    lane_f = lane.astype(F32)
    l1 = jnp.where(lane < n_experts, logits, -jnp.inf)
    m1 = jnp.max(l1, axis=-1, keepdims=True)
    i1 = jnp.min(jnp.where(l1 == m1, lane_f, float(LANES)), axis=-1, keepdims=True)
    l2 = jnp.where(lane_f == i1, -jnp.inf, l1)
    m2 = jnp.max(l2, axis=-1, keepdims=True)
    i2 = jnp.min(jnp.where(l2 == m2, lane_f, float(LANES)), axis=-1, keepdims=True)
    e2 = jnp.exp(m2 - m1)
    den = 1.0 + e2
    o_ref[...] = jnp.where(lane == 0, i1, jnp.where(lane == 1, i2, jnp.where(lane == 2, 1.0 / den,
                           jnp.where(lane == 3, e2 / den, 0.0))))


def _router(h2t, rw_hi, rw_lo, rb, n_experts):
    n = h2t.shape[0] // TOKEN_TILE_ROWS
    tm = 512 if n % 512 == 0 else n
    full = lambda shp: pl.BlockSpec(shp, lambda i: (0,) * len(shp))
    return pl.pallas_call(
        functools.partial(_router_kernel, n_experts=n_experts),
        grid=(n // tm,),
        in_specs=[pl.BlockSpec((tm * TOKEN_TILE_ROWS, LANES), lambda i: (i, 0)), full(rw_hi.shape),
                  full(rw_lo.shape), full(rb.shape)],
        out_specs=pl.BlockSpec((tm, LANES), lambda i: (i, 0)),
        out_shape=jax.ShapeDtypeStruct((n, LANES), F32),
        compiler_params=_cparams(("arbitrary",)),
        name="moe_router",
    )(h2t, rw_hi, rw_lo, rb)


def _gather_kernel(idx_ref, src_ref, out_ref, sem, *, rows_per_step):
    base = pl.program_id(0) * rows_per_step

    def row_copy(j):
        return pltpu.make_async_copy(src_ref.at[idx_ref[base + j]], out_ref.at[j], sem)

    def start(j, c):
        row_copy(j).start()
        return c

    def wait(j, c):
        row_copy(j).wait()
        return c

    lax.fori_loop(0, rows_per_step, start, 0, unroll=8)
    lax.fori_loop(0, rows_per_step, wait, 0, unroll=8)


def _gather_rows(src_t, idx, rows_per_step=512):
    m = idx.shape[0]
    while m % rows_per_step:
        rows_per_step //= 2
    src3 = src_t.reshape(-1, TOKEN_TILE_ROWS, LANES)
    gs = pltpu.PrefetchScalarGridSpec(
        num_scalar_prefetch=1,
        grid=(m // rows_per_step,),
        in_specs=[pl.BlockSpec(memory_space=pl.ANY)],
        out_specs=pl.BlockSpec((rows_per_step, TOKEN_TILE_ROWS, LANES), lambda i, idx_ref: (i, 0, 0)),
        scratch_shapes=[pltpu.SemaphoreType.DMA(())],
    )
    out = pl.pallas_call(
        functools.partial(_gather_kernel, rows_per_step=rows_per_step),
        grid_spec=gs,
        out_shape=jax.ShapeDtypeStruct((m, TOKEN_TILE_ROWS, LANES), src_t.dtype),
        compiler_params=pltpu.CompilerParams(dimension_semantics=("arbitrary",), vmem_limit_bytes=VMEM_LIMIT),
        name="gather_rows",
    )(idx, src3)
    return out.reshape(m * TOKEN_TILE_ROWS, LANES)


def _moe_block_kernel(be_ref, nu_ref, x_ref, wg_ref, wu_ref, wd_ref, o_ref, *, f_tile):
    i = pl.program_id(0)

    @pl.when(i < nu_ref[0])
    def _():
        hb = _untile(x_ref, MOE_BLOCK).astype(BF)
        d_ff = wg_ref.shape[1]
        acc = jnp.zeros((MOE_BLOCK, wd_ref.shape[1]), F32)
        for f in range(d_ff // f_tile):
            fs = slice(f * f_tile, (f + 1) * f_tile)
            gp = _dot(hb, wg_ref[:, fs])
            up = _dot(hb, wu_ref[:, fs])
            act = gp * jax.nn.sigmoid(gp) * up
            acc = acc + _dot(act.astype(BF), wd_ref[fs, :])
        _store_tiled(o_ref, acc)

    @pl.when(i >= nu_ref[0])
    def _():
        o_ref[...] = jnp.zeros(o_ref.shape, F32)


def _moe_blocks(blk_e, n_used, xb_t, wg, wu, wd):
    nb = xb_t.shape[0] // (MOE_BLOCK * TOKEN_TILE_ROWS)
    d, d_ff = wg.shape[1], wg.shape[2]
    blk = pl.BlockSpec((MOE_BLOCK * TOKEN_TILE_ROWS, LANES), lambda i, be, nu: (i, 0))
    gs = pltpu.PrefetchScalarGridSpec(
        num_scalar_prefetch=2,
        grid=(nb,),
        in_specs=[blk,
                  pl.BlockSpec((None, d, d_ff), lambda i, be, nu: (be[i], 0, 0)),
                  pl.BlockSpec((None, d, d_ff), lambda i, be, nu: (be[i], 0, 0)),
                  pl.BlockSpec((None, d_ff, d), lambda i, be, nu: (be[i], 0, 0))],
        out_specs=blk,
    )
    return pl.pallas_call(
        functools.partial(_moe_block_kernel, f_tile=_ff_tile(d_ff)),
        grid_spec=gs,
        out_shape=jax.ShapeDtypeStruct(xb_t.shape, F32),
        compiler_params=_cparams(("arbitrary",)),
        name="moe_blocks",
    )(blk_e, n_used, xb_t, wg, wu, wd)


def _combine_kernel(ya0_ref, ya1_ref, r_ref, y_ref, g2_ref, o_ref):
    tm = o_ref.shape[0]
    w1 = r_ref[:, 2:3]
    w2 = r_ref[:, 3:4]
    o_ref[...] = y_ref[...] + g2_ref[...] * (w1 * _untile(ya0_ref, tm) + w2 * _untile(ya1_ref, tm))


def _moe_combine(ya_t, route, y, gate2, *, per_batch, tokens_per_batch):
    n, d = y.shape
    tm = 512 if per_batch else n
    nt = n // tm
    row = lambda w: pl.BlockSpec((tm, w), lambda i: (i, 0))
    return pl.pallas_call(
        _combine_kernel,
        grid=(nt,),
        in_specs=[pl.BlockSpec((tm * TOKEN_TILE_ROWS, LANES), lambda i: (i, 0)),
                  pl.BlockSpec((tm * TOKEN_TILE_ROWS, LANES), lambda i: (i + nt, 0)), row(LANES), row(d),
                  _mod_spec(per_batch, tm, d, tokens_per_batch)],
        out_specs=row(d),
        out_shape=jax.ShapeDtypeStruct((n, d), F32),
        compiler_params=_cparams(("arbitrary",)),
        name="moe_combine",
    )(ya_t, ya_t, route, y, gate2)


def _moe_ffn(h2t, y, gate2, rw_hi, rw_lo, rb, wg, wu, wd, *, per_batch, tokens_per_batch):
    n, d = y.shape
    n_experts = wg.shape[0]
    route = _router(h2t, rw_hi, rw_lo, rb, n_experts)
    flat_e = route[:, 0:TOP_K].astype(jnp.int32).reshape(-1)
    onehot = (flat_e[:, None] == jnp.arange(n_experts, dtype=jnp.int32)[None, :]).astype(jnp.int32)
    csum = jnp.cumsum(onehot, axis=0)
    rank = jnp.sum(onehot * csum, axis=1) - 1
    counts = csum[-1]
    padded = (counts + MOE_BLOCK - 1) // MOE_BLOCK * MOE_BLOCK
    pad_end = jnp.cumsum(padded)
    pad_start = pad_end - padded
    slot = (pad_start[flat_e] + rank).astype(jnp.int32)
    nb = -(-(n * TOP_K) // MOE_BLOCK) + n_experts
    src = jnp.zeros((nb * MOE_BLOCK,), jnp.int32).at[slot].set(jnp.arange(n * TOP_K, dtype=jnp.int32) // TOP_K)
    blk_e = jnp.minimum(jnp.sum(pad_end[None, :] <= (jnp.arange(nb) * MOE_BLOCK)[:, None], axis=-1),
                        n_experts - 1).astype(jnp.int32)
    n_used = (pad_end[-1:] // MOE_BLOCK).astype(jnp.int32)
    xb = _gather_rows(h2t, src)
    yb = _moe_blocks(blk_e, n_used, xb, wg, wu, wd)
    ya = _gather_rows(yb, jnp.concatenate([slot[0::TOP_K], slot[1::TOP_K]]))
    return _moe_combine(ya, route, y, gate2, per_batch=per_batch, tokens_per_batch=tokens_per_batch)


def _rope_tables(pos):
    half = HEAD_DIM // 2
    inv = ROPE_THETA ** (-jnp.arange(half, dtype=F32) / half)
    ang = pos.astype(F32)[:, None] * inv[None, :]
    cos, sin = jnp.cos(ang), jnp.sin(ang)
    cos_l = jnp.concatenate([cos, cos] * (LANES // HEAD_DIM), axis=1)
    sin_l = jnp.concatenate([-sin, sin] * (LANES // HEAD_DIM), axis=1)
    return cos_l, sin_l


def _cover_matrix(n_rows, n_cmp, n_slc):
    cs = np.arange(n_cmp) * CMP_STRIDE
    ss = np.arange(n_slc) * SLC_BLOCK
    cov = np.clip(np.minimum(cs[:, None] + CMP_BLOCK, ss[None, :] + SLC_BLOCK)
                  - np.maximum(cs[:, None], ss[None, :]), 0, None).astype(np.float32) / CMP_BLOCK
    out = np.zeros((n_rows, LANES), np.float32)
    out[:n_cmp, :n_slc] = cov
    return jnp.asarray(out, BF)


def _expand_matrix(n_keys):
    blk = np.arange(LANES)[:, None]
    key = np.arange(n_keys)[None, :] // SLC_BLOCK
    return jnp.asarray((blk == key).astype(np.float32), BF)


def _piece_row_permutation(page):
    per_page = page // CMP_STRIDE
    r = np.arange(page)
    src = (r % per_page) * CMP_STRIDE + r // per_page
    return jnp.asarray((src[:, None] == np.arange(page)[None, :]).astype(np.float32), BF)


def _block_diag_mean(width):
    idx = np.arange(width) // HEAD_DIM
    return jnp.asarray((idx[:, None] == idx[None, :]).astype(np.float32) / HEAD_DIM, BF)


def _compress_weights(cmp_w_l, cmp_pe_l):
    z = jnp.zeros_like(cmp_w_l)
    wbig = jnp.concatenate([jnp.concatenate([cmp_w_l, z], axis=3), jnp.concatenate([z, cmp_w_l], axis=3)],
                           axis=2).astype(BF)
    pebig = jnp.concatenate([cmp_pe_l, cmp_pe_l], axis=2)
    return wbig, pebig


def kernel(x_prompt, x_sample, cache_cmp, cache_slc, state_win, page_table, c_prompt, c_sample, norm_mix_g, norm_ffn_g, w_ada, b_ada, w_in, w_spatial, b_spatial, q_norm_g, k_norm_g, cmp_pe, cmp_w, out_norm_g, w_out, ffn_w_gate, ffn_w_up, ffn_w_down, router_w, router_b, moe_w_gate, moe_w_up, moe_w_down):
    batch, t, d = x_prompt.shape
    db, tn, _ = x_sample.shape
    depth = w_ada.shape[0]
    n_pool, page = cache_cmp.shape[1], cache_cmp.shape[2]
    n_pages = page_table.shape[1]
    past = n_pages * page
    wb = state_win.shape[2]
    d_a = d // 2
    a_groups = d_a // HEAD_DIM
    assert tn == 1 and past % CHUNK == 0 and t % KEY_TILE == 0 and d_a % LANES == 0
    assert d - d_a == B_KV_HEADS * GQA_R * HEAD_DIM and B_KV_HEADS * HEAD_DIM == LANES
    n_p = batch * t
    kvw = 2 * LANES
    p_main = 2 * d_a + (d - d_a) + N_BRANCH * kvw
    n_gate = w_in.shape[2] - p_main
    qpos_s = past

    m_rows = batch + db
    m_pad = -(-m_rows // 8) * 8
    c_all = jnp.concatenate([c_prompt, c_sample, jnp.zeros((m_pad - m_rows, d), F32)], axis=0)
    mods = _adaln(c_all, w_ada, b_ada)

    bd = _block_diag_mean(d_a)
    cos_p, sin_p = _rope_tables(jnp.arange(t, dtype=jnp.int32))
    cos_s, sin_s = _rope_tables(jnp.full((1,), qpos_s, jnp.int32))
    np_p = t // CMP_STRIDE
    cover_p = _cover_matrix(np_p, np_p - 1, t // SLC_BLOCK)
    expand_p = _expand_matrix(t)
    np_s = past // CMP_STRIDE
    ns_s = -(-(past + tn) // SLC_BLOCK)
    cover_s = _cover_matrix(np_s, np_s - 1, ns_s)
    expand_s = _expand_matrix(past)
    tril = jnp.tril(jnp.ones((CHUNK, CHUNK), dtype=bool))

    cache_cmp4 = jnp.transpose(cache_cmp, (0, 1, 3, 4, 5, 2)).reshape(depth, n_pool, kvw, page)
    perm_s = _piece_row_permutation(page)
    cache_slc4 = jnp.transpose(cache_slc, (0, 1, 3, 4, 5, 2)).reshape(depth, n_pool, kvw, page)
    state_win4 = jnp.transpose(state_win, (0, 1, 3, 4, 5, 2)).reshape(depth, db, kvw, wb)

    yp = x_prompt.reshape(n_p, d)
    ys = x_sample.reshape(db, d)
    outs = {k: [] for k in ("cmp_s", "slc_s", "chv_s")}
    win_buf = jnp.zeros((depth, db, kvw, wb), F32)
    kv_bufs = [jnp.zeros((depth, batch, kvw, t), F32) for _ in range(N_BRANCH)]
    for l in range(depth):
        mod_p = [mods[l, :batch, k * d:(k + 1) * d].reshape(batch, 1, d) for k in range(6)]
        mod_s = [mods[l, batch:batch + db, k * d:(k + 1) * d] for k in range(6)]
        w_in_bf = jnp.concatenate([w_in[l], jnp.zeros((d, LANES - n_gate), F32)], axis=1).astype(BF)
        qg = jnp.tile(q_norm_g[l], (d - d_a) // HEAD_DIM)[None, :]
        kg = jnp.tile(k_norm_g[l], (1, LANES // HEAD_DIM))
        ws_p = jnp.where(tril, w_spatial[l], 0.0).astype(BF)
        bs_p = jnp.repeat(b_spatial[l].T, HEAD_DIM, axis=1)
        ws_s = jnp.repeat(w_spatial[l][:, 0, 0], HEAD_DIM)[None, :]
        bs_s = jnp.repeat(b_spatial[l][:, 0], HEAD_DIM)[None, :]
        wbig, pebig = _compress_weights(cmp_w[l], cmp_pe[l])
        out_g = out_norm_g[l].reshape(1, d)
        wout_bf = w_out[l].astype(BF)
        ng1 = norm_mix_g[l][None, :]
        ng2 = norm_ffn_g[l][None, :]

        a, q, cmp_r, slc_bf, win_bf, gates, *kv_bufs = _pre_mix(
            yp, ng1, mod_p[0], mod_p[1], w_in_bf, cos_p, sin_p, bd, qg, kg, ws_p, bs_p,
            chunked=True, tokens_per_batch=t, kv_bufs=kv_bufs, layer=l)
        kc, vc = _compress_prompt(cmp_r, wbig, pebig, batch)
        o = _nsa_prompt(q, gates, kc, vc, slc_bf, win_bf, cover_p, expand_p, batch)
        yp, h2p = _post_mix(a, o, yp, mod_p[2], out_g, wout_bf, bd, ng2, mod_p[3], mod_p[4],
                            per_batch=True, tokens_per_batch=t, tiled_h2=(l % 2 == 1))

        a_s, q_s, cmp_s, slc_s, win_s, gates_s, v_s = _pre_mix(
            ys, ng1, mod_s[0], mod_s[1], w_in_bf, cos_s, sin_s, bd, qg, kg, ws_s, bs_s,
            chunked=False, tokens_per_batch=1)
        q3 = q_s.reshape(db, 1, d - d_a)
        ocmp, imp = _s_cmp(page_table, q3, cache_cmp4, l, perm_s, wbig, pebig, cover_s, qpos_s)
        mask = _s_select(imp[:, 0:B_KV_HEADS, :].reshape(db * B_KV_HEADS, LANES), expand_s, qpos_s)
        o_s, win_buf = _s_att(page_table, q3, gates_s.reshape(db, 1, LANES), mask.reshape(db, B_KV_HEADS, past),
                              ocmp, slc_s.reshape(db, 1, kvw), win_s.reshape(db, 1, kvw), state_win4, win_buf,
                              cache_slc4, l, qpos_s, past)
        ys, h2s = _post_mix(a_s, o_s.reshape(db, d - d_a), ys, mod_s[2], out_g, wout_bf, bd, ng2, mod_s[3],
                            mod_s[4], per_batch=False, tokens_per_batch=1, tiled_h2=(l % 2 == 1))
        outs["cmp_s"].append(cmp_s.reshape(db, tn, 2, B_KV_HEADS, HEAD_DIM))
        outs["slc_s"].append(slc_s.reshape(db, tn, 2, B_KV_HEADS, HEAD_DIM))
        outs["chv_s"].append(v_s.reshape(db, tn, d_a))

        i = l // 2
        if l % 2 == 0:
            wg, wu, wd = ffn_w_gate[i].astype(BF), ffn_w_up[i].astype(BF), ffn_w_down[i].astype(BF)
            yp = _ffn_dense(h2p, yp, mod_p[5], wg, wu, wd, per_batch=True, tokens_per_batch=t)
            ys = _ffn_dense(h2s, ys, mod_s[5], wg, wu, wd, per_batch=False, tokens_per_batch=1)
        else:
            wg, wu, wd = moe_w_gate[i].astype(BF), moe_w_up[i].astype(BF), moe_w_down[i].astype(BF)
            n_exp = router_w.shape[2]
            rw = jnp.concatenate([router_w[i], jnp.zeros((d, LANES - n_exp), F32)], axis=1)
            rw_hi = rw.astype(BF)
            rw_lo = (rw - rw_hi.astype(F32)).astype(BF)
            rb = jnp.concatenate([router_b[i], jnp.zeros((LANES - n_exp,), F32)])[None, :]
            yp = _moe_ffn(h2p, yp, mod_p[5], rw_hi, rw_lo, rb, wg, wu, wd, per_batch=True, tokens_per_batch=t)
            ys = _moe_ffn(h2s, ys, mod_s[5], rw_hi, rw_lo, rb, wg, wu, wd, per_batch=False, tokens_per_batch=1)

    st = lambda k: jnp.stack(outs[k])

    def rows_view(buf):
        return jnp.transpose(buf.reshape(buf.shape[0], buf.shape[1], 2, B_KV_HEADS, HEAD_DIM, buf.shape[3]),
                             (0, 1, 5, 2, 3, 4))

    wbp = min(WINDOW, t)
    return (yp.reshape(batch, t, d), ys.reshape(db, tn, d), rows_view(kv_bufs[0]), st("cmp_s"),
            rows_view(kv_bufs[1]), st("slc_s"), rows_view(kv_bufs[2][:, :, :, t - wbp:]), rows_view(win_buf),
            st("chv_s"))
```

```python
import functools

import numpy as np
import jax
import jax.numpy as jnp
from jax import lax
from jax.experimental import pallas as pl
from jax.experimental.pallas import tpu as pltpu

F32 = jnp.float32
BF = jnp.bfloat16

HEAD_DIM = 64
B_KV_HEADS = 2
GQA_R = 4
N_BRANCH = 3
CHUNK = 128
CMP_BLOCK = 32
CMP_STRIDE = 16
SLC_BLOCK = 64
N_SELECT = 16
WINDOW = 512
Q_BLOCK = 128
ROPE_THETA = 10000.0
TOP_K = 2
MOE_BLOCK = 128
NEG_INF = -1e30
FORCE_SCORE = 1e30
EPS = 1e-6
LOG2_E = 1.4426950408889634

LANES = 128
TOKEN_TILE_ROWS = 8
SAMPLES_PER_STEP = 4
KEY_TILE = 1024
VMEM_LIMIT = 56 * 1024 * 1024


def _cparams(sem):
    return pltpu.CompilerParams(dimension_semantics=sem, vmem_limit_bytes=VMEM_LIMIT)


def _dot(a, b):
    return jnp.dot(a, b, preferred_element_type=F32)


def _dot_nt(a, b):
    return lax.dot_general(a, b, (((1,), (1,)), ((), ())), preferred_element_type=F32)


def _split_dot(x, w):
    hi = x.astype(BF)
    lo = (x - hi.astype(F32)).astype(BF)
    return _dot(hi, w) + _dot(lo, w)


def _ada_kernel(c_ref, w_ref, b_ref, o_ref):
    c = c_ref[...]
    s = c * jax.nn.sigmoid(c)
    o_ref[0] = _dot(s.astype(BF), w_ref[0].astype(BF)) + b_ref[0]


def _adaln(c_all, w_ada, b_ada):
    depth, d, n6 = w_ada.shape
    m = c_all.shape[0]
    tn = 1536
    return pl.pallas_call(
        _ada_kernel,
        grid=(depth, n6 // tn),
        in_specs=[pl.BlockSpec((m, d), lambda l, j: (0, 0)),
                  pl.BlockSpec((1, d, tn), lambda l, j: (l, 0, j)),
                  pl.BlockSpec((1, 1, tn), lambda l, j: (l, 0, j))],
        out_specs=pl.BlockSpec((1, m, tn), lambda l, j: (l, 0, j)),
        out_shape=jax.ShapeDtypeStruct((depth, m, n6), F32),
        compiler_params=_cparams(("arbitrary", "arbitrary")),
        name="adaln",
    )(c_all, w_ada, b_ada.reshape(depth, 1, n6))


def _pre_kernel(x_ref, g_ref, sh_ref, sc_ref, w_ref, cos_ref, sin_ref, bd_ref, qg_ref, kg_ref,
                ws_ref, bs_ref, *out_refs, chunked, d_a, d_b):
    if chunked:
        a_ref, q_ref, cmp_ref, slcb_ref, winb_ref, gate_ref, cmpt_ref, slct_ref, wint_ref = out_refs[N_BRANCH:]
        row_refs = (cmp_ref, None, None)
        bf_refs = (None, slcb_ref, winb_ref)
        t_refs = (cmpt_ref, slct_ref, wint_ref)
    else:
        a_ref, q_ref, cmp_ref, slc_ref, win_ref, gate_ref, v_ref = out_refs
        row_refs = (cmp_ref, slc_ref, win_ref)
        bf_refs = t_refs = (None, None, None)
    tm = x_ref.shape[0]
    x = x_ref[...]
    ms = jnp.mean(x * x, axis=-1, keepdims=True)
    h = x * lax.rsqrt(ms + EPS) * g_ref[...]
    h = h * (1.0 + sc_ref[...]) + sh_ref[...]
    z = _dot(h.astype(BF), w_ref[...])
    uv = jax.nn.gelu(z[:, :2 * d_a])
    u = uv[:, :d_a]
    v = uv[:, d_a:]
    bd = bd_ref[...]
    mu = _split_dot(v, bd)
    vc = v - mu
    vn = vc * lax.rsqrt(_split_dot(vc * vc, bd) + EPS)

    cos = cos_ref[...]
    sin = sin_ref[...]
    lane = lax.broadcasted_iota(jnp.int32, (1, LANES), 1)
    first_half = (lane % HEAD_DIM) < (HEAD_DIM // 2)

    def rope(t):
        swapped = jnp.where(first_half, pltpu.roll(t, LANES - HEAD_DIM // 2, 1),
                            pltpu.roll(t, HEAD_DIM // 2, 1))
        return t * cos + swapped * sin

    o = 2 * d_a
    zq = z[:, o:o + d_b]
    qn = zq * lax.rsqrt(_split_dot(zq * zq, bd) + EPS) * qg_ref[...]
    for c in range(d_b // LANES):
        q_ref[:, c * LANES:(c + 1) * LANES] = rope(qn[:, c * LANES:(c + 1) * LANES]) * (HEAD_DIM ** -0.5)
    o += d_b
    bd1 = bd_ref[0:LANES, 0:LANES]
    for br in range(N_BRANCH):
        kp = z[:, o:o + LANES]
        vp = z[:, o + LANES:o + 2 * LANES]
        kn = kp * lax.rsqrt(_split_dot(kp * kp, bd1) + EPS) * kg_ref[br:br + 1, :]
        kr = rope(kn)
        if row_refs[br] is not None:
            row_refs[br][:, 0:LANES] = kr
            row_refs[br][:, LANES:2 * LANES] = vp
        if t_refs[br] is not None:
            t_refs[br][0:LANES, :] = jnp.transpose(kr)
            t_refs[br][LANES:2 * LANES, :] = jnp.transpose(vp)
        if bf_refs[br] is not None:
            bf_refs[br][:, 0:LANES] = kr.astype(BF)
            bf_refs[br][:, LANES:2 * LANES] = vp.astype(BF)
        o += 2 * LANES
    gate_ref[...] = jax.nn.sigmoid(z[:, o:o + LANES])

    if chunked:
        low = lane < HEAD_DIM
        for c in range(tm // CHUNK):
            rs = slice(c * CHUNK, (c + 1) * CHUNK)
            for j in range(d_a // LANES):
                cs = slice(j * LANES, (j + 1) * LANES)
                vpair = vn[rs, cs].astype(BF)
                m0 = _dot(ws_ref[2 * j], vpair)
                m1 = _dot(ws_ref[2 * j + 1], vpair)
                mixed = jnp.where(low, m0, m1) + bs_ref[:, cs]
                a_ref[rs, cs] = u[rs, cs] * mixed
    else:
        a_ref[...] = u * (vn * ws_ref[...] + bs_ref[...])
        v_ref[...] = vn


def _pre_mix(x, norm_g, shift, scale, w_in_bf, cos, sin, bd, qg, kg, ws, bs, *, chunked, tokens_per_batch,
             kv_bufs=(), layer=0):
    n, d = x.shape
    d_a = d_b = d // 2
    tm = 256 if chunked else n
    nt = n // tm
    pw = w_in_bf.shape[1]
    full = lambda shp: pl.BlockSpec(shp, lambda i: (0,) * len(shp))
    row = lambda w: pl.BlockSpec((tm, w), lambda i: (i, 0))
    extra_in, aliases = [], {}
    if chunked:
        tpb = tokens_per_batch // tm
        mod_spec = pl.BlockSpec((None, 1, d), lambda i: (i // tpb, 0, 0))
        rot_spec = pl.BlockSpec((tm, LANES), lambda i: (i % tpb, 0))
        ws_spec, bs_spec = full(ws.shape), full(bs.shape)
        out_shape = [jax.ShapeDtypeStruct((n, d_a), F32), jax.ShapeDtypeStruct((n, d_b), F32),
                     jax.ShapeDtypeStruct((n, 2 * LANES), F32), jax.ShapeDtypeStruct((n, 2 * LANES), BF),
                     jax.ShapeDtypeStruct((n, 2 * LANES), BF), jax.ShapeDtypeStruct((n, LANES), F32)]
        out_specs = [row(d_a), row(d_b), row(2 * LANES), row(2 * LANES), row(2 * LANES), row(LANES)]
        t_spec = pl.BlockSpec((None, None, 2 * LANES, tm), lambda i: (layer, i // tpb, 0, i % tpb))
        n_in = 12
        for j, buf in enumerate(kv_bufs):
            extra_in.append(pl.BlockSpec(memory_space=pl.ANY))
            aliases[n_in + j] = len(out_shape)
            out_shape.append(jax.ShapeDtypeStruct(buf.shape, F32))
            out_specs.append(t_spec)
    else:
        mod_spec = row(d)
        rot_spec = full((1, LANES))
        ws_spec, bs_spec = full(ws.shape), full(bs.shape)
        out_shape = [jax.ShapeDtypeStruct((n, d_a), F32), jax.ShapeDtypeStruct((n, d_b), F32),
                     jax.ShapeDtypeStruct((n, 2 * LANES), F32), jax.ShapeDtypeStruct((n, 2 * LANES), F32),
                     jax.ShapeDtypeStruct((n, 2 * LANES), F32), jax.ShapeDtypeStruct((n, LANES), F32),
                     jax.ShapeDtypeStruct((n, d_a), F32)]
        out_specs = [row(d_a), row(d_b), row(2 * LANES), row(2 * LANES), row(2 * LANES), row(LANES), row(d_a)]
    return pl.pallas_call(
        functools.partial(_pre_kernel, chunked=chunked, d_a=d_a, d_b=d_b),
        grid=(nt,),
        in_specs=[row(d), full((1, d)), mod_spec, mod_spec, full((d, pw)), rot_spec, rot_spec,
                  full(bd.shape), full(qg.shape), full(kg.shape), ws_spec, bs_spec] + extra_in,
        out_specs=out_specs,
        out_shape=out_shape,
        input_output_aliases=aliases,
        compiler_params=_cparams(("arbitrary",)),
        name="pre_mix_prompt" if chunked else "pre_mix_sample",
    )(x, norm_g, shift, scale, w_in_bf, cos, sin, bd, qg, kg, ws, bs, *kv_bufs)


def _compress_pieces(load_piece_rows, w_ref, pe_ref, kv, n_piece, seg=None):
    assert CMP_BLOCK == 2 * CMP_STRIDE
    seg = n_piece if seg is None else seg
    acc_a = jnp.zeros((n_piece, LANES), F32)
    acc_b = jnp.zeros((n_piece, LANES), F32)
    for l in range(CMP_STRIDE):
        xl = load_piece_rows(l)
        l2 = CMP_STRIDE + l
        acc_a = acc_a + _dot((xl + pe_ref[kv, l:l + 1, :]).astype(BF), w_ref[kv, l])
        acc_b = acc_b + _dot((xl + pe_ref[kv, l2:l2 + 1, :]).astype(BF), w_ref[kv, l2])
    shifted = pltpu.roll(acc_b, n_piece - 1, 0)
    rows = lax.broadcasted_iota(jnp.int32, (n_piece, 1), 0)
    return jnp.where(rows % seg < seg - 1, acc_a + shifted, 0.0)


def _compress_kernel(xk_ref, xv_ref, w_ref, pe_ref, kc_ref, vc_ref):
    n_piece = kc_ref.shape[0]
    for kv, (x_ref, o_ref) in enumerate(((xk_ref, kc_ref), (xv_ref, vc_ref))):
        out = _compress_pieces(lambda l: x_ref[pl.ds(l, n_piece, stride=CMP_STRIDE), :], w_ref, pe_ref, kv, n_piece)
        o_ref[...] = out.astype(BF)


def _compress_prompt(cmp_rows, wbig, pebig, batch):
    n = cmp_rows.shape[0]
    t = n // batch
    n_piece = t // CMP_STRIDE
    out = jax.ShapeDtypeStruct((batch * n_piece, LANES), BF)
    return pl.pallas_call(
        _compress_kernel,
        grid=(batch,),
        in_specs=[pl.BlockSpec((t, LANES), lambda b: (b, 0)),
                  pl.BlockSpec((t, LANES), lambda b: (b, 1)),
                  pl.BlockSpec(wbig.shape, lambda b: (0, 0, 0, 0)),
                  pl.BlockSpec(pebig.shape, lambda b: (0, 0, 0))],
        out_specs=[pl.BlockSpec((n_piece, LANES), lambda b: (b, 0))] * 2,
        out_shape=[out, out],
        compiler_params=_cparams(("arbitrary",)),
        name="compress_prompt",
    )(cmp_rows, cmp_rows, wbig, pebig)


def _group_queries(q, g):
    m = q.shape[0]
    low = lax.broadcasted_iota(jnp.int32, (m, LANES), 1) < HEAD_DIM
    keep = low if g == 0 else jnp.logical_not(low)
    parts = []
    for r in range(GQA_R):
        h = GQA_R * g + r
        t = q[:, (h // 2) * LANES:(h // 2 + 1) * LANES]
        if h % 2 != g:
            t = pltpu.roll(t, HEAD_DIM, 1)
        parts.append(jnp.where(keep, t, 0.0))
    return parts


def _top_select(imp, blk_f):
    sel = jnp.zeros(imp.shape, F32)
    work = imp
    for _ in range(N_SELECT):
        mx = jnp.max(work, axis=-1, keepdims=True)
        first = jnp.min(jnp.where(work == mx, blk_f, float(LANES)), axis=-1, keepdims=True)
        hit = blk_f == first
        sel = jnp.where(hit, 1.0, sel)
        work = jnp.where(hit, -jnp.inf, work)
    return sel


def _block_scores(imp, qpos):
    blk = lax.broadcasted_iota(jnp.int32, imp.shape, 1)
    cur = qpos // SLC_BLOCK
    forced = (blk == 0) | (blk == cur) | (blk == cur - 1)
    valid = blk <= cur
    imp = jnp.where(forced, FORCE_SCORE, imp)
    imp = jnp.where(valid, imp, NEG_INF)
    return imp, blk.astype(F32), valid


def _merge_heads(res, m):
    low = lax.broadcasted_iota(jnp.int32, (m, LANES), 1) < HEAD_DIM
    cols = []
    for c in range(4):
        a, b = res[2 * c], res[2 * c + 1]
        if c // 2 == 0:
            cols.append(jnp.where(low, a, pltpu.roll(b, HEAD_DIM, 1)))
        else:
            cols.append(jnp.where(low, pltpu.roll(a, HEAD_DIM, 1), b))
    return jnp.concatenate(cols, axis=1)


def _nsa_prompt_kernel(q_ref, gate_ref, kc_ref, vc_ref, slc_ref, win_ref, cover_ref, e_ref, o_ref):
    i = pl.program_id(1)
    s0 = i * Q_BLOCK
    n_cmp = kc_ref.shape[0]
    rows4 = GQA_R * Q_BLOCK
    q = q_ref[...] * LOG2_E
    gates = gate_ref[...]
    tpos = s0 + lax.broadcasted_iota(jnp.int32, (Q_BLOCK, 1), 0)
    tpos4 = jnp.concatenate([tpos] * GQA_R, axis=0)
    tile4 = lambda x: jnp.concatenate([x] * GQA_R, axis=0)
    groups = range(B_KV_HEADS)
    qg = [jnp.concatenate(_group_queries(q, g), axis=0).astype(BF) for g in groups]

    kc = kc_ref[...]
    vc = vc_ref[...]
    c_last = lax.broadcasted_iota(jnp.int32, (1, n_cmp), 1) * CMP_STRIDE + (CMP_BLOCK - 1)
    c_bias = jnp.where(c_last <= tpos4, 0.0, NEG_INF)
    has_cmp = jnp.where(tpos4 >= CMP_BLOCK - 1, 1.0, 0.0)
    o_cmp, p_sums = [], []
    for g in groups:
        s = _dot_nt(qg[g], kc) + c_bias
        e = jnp.exp2(s - jnp.max(s, axis=-1, keepdims=True))
        p = e * (has_cmp / jnp.sum(e, axis=-1, keepdims=True))
        o_cmp.append(_dot(p.astype(BF), vc))
        p_sum = p[0:Q_BLOCK]
        for r in range(1, GQA_R):
            p_sum = p_sum + p[r * Q_BLOCK:(r + 1) * Q_BLOCK]
        p_sums.append(p_sum)

    imp = _split_dot(jnp.concatenate(p_sums, axis=0), cover_ref[...])
    imp, blk_f, valid = _block_scores(imp, jnp.concatenate([tpos] * B_KV_HEADS, axis=0))
    sel_all = jnp.where(valid, _top_select(imp, blk_f), 0.0).astype(BF)
    sel_b = [sel_all[g * Q_BLOCK:(g + 1) * Q_BLOCK] for g in groups]

    w_start = pl.multiple_of(jnp.maximum(s0 - WINDOW, 0), Q_BLOCK)
    w_len = WINDOW + Q_BLOCK
    wpos = w_start + lax.broadcasted_iota(jnp.int32, (1, w_len), 1)
    w_bias4 = tile4(jnp.where((wpos <= tpos) & (wpos > tpos - WINDOW), 0.0, NEG_INF))
    kw = win_ref[pl.ds(w_start, w_len), 0:LANES]
    vw = win_ref[pl.ds(w_start, w_len), LANES:2 * LANES]
    low_w = lax.broadcasted_iota(jnp.int32, (1, LANES), 1) < HEAD_DIM
    keep = [low_w, jnp.logical_not(low_w)]
    with_ones = lambda v, g: jnp.where(keep[g], v, jnp.ones_like(v))
    den_lane = [HEAD_DIM, 0]

    def normalise(acc, g):
        return acc / acc[:, den_lane[g]:den_lane[g] + 1]

    o_win = []
    for g in groups:
        sw = _dot_nt(qg[g], kw) + w_bias4
        ew = jnp.exp2(sw - jnp.max(sw, axis=-1, keepdims=True))
        o_win.append(_dot(ew.astype(BF), vw) / jnp.sum(ew, axis=-1, keepdims=True))

    def body(kt, carry):
        k0 = pl.multiple_of(kt * KEY_TILE, KEY_TILE)
        kk = slc_ref[pl.ds(k0, KEY_TILE), 0:LANES]
        vv = slc_ref[pl.ds(k0, KEY_TILE), LANES:2 * LANES]
        et = e_ref[:, pl.ds(k0, KEY_TILE)]
        kpos = k0 + lax.broadcasted_iota(jnp.int32, (1, KEY_TILE), 1)
        causal = kpos <= tpos
        out = []
        for g in groups:
            m, acc = carry[g]
            bias = jnp.where((_dot(sel_b[g], et) > 0.5) & causal, 0.0, NEG_INF)
            sc = _dot_nt(qg[g], kk) + tile4(bias)
            m_new = jnp.maximum(m, jnp.max(sc, axis=-1, keepdims=True))
            pe = jnp.exp2(sc - m_new)
            acc = jnp.exp2(m - m_new) * acc + _dot(pe.astype(BF), with_ones(vv, g))
            out.append((m_new, acc))
        return tuple(out)

    init = tuple((jnp.full((rows4, 1), NEG_INF, F32), jnp.zeros((rows4, LANES), F32)) for _ in groups)
    n_tiles = (s0 + Q_BLOCK + KEY_TILE - 1) // KEY_TILE
    fin = lax.fori_loop(0, n_tiles, body, init)

    res = []
    for g in groups:
        o_slc = normalise(fin[g][1], g)
        for r in range(GQA_R):
            h = GQA_R * g + r
            rs = slice(r * Q_BLOCK, (r + 1) * Q_BLOCK)
            res.append(gates[:, 3 * h:3 * h + 1] * o_cmp[g][rs] + gates[:, 3 * h + 1:3 * h + 2] * o_slc[rs]
                       + gates[:, 3 * h + 2:3 * h + 3] * o_win[g][rs])
    o_ref[...] = _merge_heads(res, Q_BLOCK)


def _nsa_prompt(q, gates, kc, vc, slc_bf, win_bf, cover, expand, batch):
    n, d_b = q.shape
    t = n // batch
    nq = t // Q_BLOCK
    n_cmp = kc.shape[0] // batch
    return pl.pallas_call(
        _nsa_prompt_kernel,
        grid=(batch, nq),
        in_specs=[pl.BlockSpec((Q_BLOCK, d_b), lambda b, i: (b * nq + i, 0)),
                  pl.BlockSpec((Q_BLOCK, LANES), lambda b, i: (b * nq + i, 0)),
                  pl.BlockSpec((n_cmp, LANES), lambda b, i: (b, 0)),
                  pl.BlockSpec((n_cmp, LANES), lambda b, i: (b, 0)),
                  pl.BlockSpec((t, 2 * LANES), lambda b, i: (b, 0)),
                  pl.BlockSpec((t, 2 * LANES), lambda b, i: (b, 0)),
                  pl.BlockSpec(cover.shape, lambda b, i: (0, 0)),
                  pl.BlockSpec(expand.shape, lambda b, i: (0, 0))],
        out_specs=pl.BlockSpec((Q_BLOCK, d_b), lambda b, i: (b * nq + i, 0)),
        out_shape=jax.ShapeDtypeStruct((n, d_b), F32),
        compiler_params=_cparams(("arbitrary", "arbitrary")),
        name="nsa_prompt",
    )(q, gates, kc, vc, slc_bf, win_bf, cover, expand)


def _sample_queries(q_row):
    sub = lax.broadcasted_iota(jnp.int32, (8, LANES), 0)
    out = jnp.zeros((8, LANES), F32)
    for g in range(B_KV_HEADS):
        for r, part in enumerate(_group_queries(q_row, g)):
            out = jnp.where(sub == GQA_R * g + r, jnp.broadcast_to(part, (8, LANES)), out)
    return out


def _s_cmp_kernel(pt_ref, q_ref, *refs, n_pages, n_samp, qpos):
    n_pg = n_samp * n_pages
    pages = refs[:n_pg]
    perm_ref, wbig_ref, pe_ref, cover_ref, ocmp_ref, imp_ref = refs[n_pg:]
    per_page = pages[0].shape[1] // CMP_STRIDE
    n_piece = n_pages * per_page

    def loader(kv):
        xp = [_dot_nt(perm_ref[...], pg[kv * LANES:(kv + 1) * LANES, :].astype(BF)) for pg in pages]
        return lambda l: jnp.concatenate([x[l * per_page:(l + 1) * per_page, :] for x in xp], axis=0)

    kc_all = _compress_pieces(loader(0), wbig_ref, pe_ref, 0, n_samp * n_piece, n_piece).astype(BF)
    vc_all = _compress_pieces(loader(1), wbig_ref, pe_ref, 1, n_samp * n_piece, n_piece).astype(BF)
    c_last = lax.broadcasted_iota(jnp.int32, (1, n_piece), 1) * CMP_STRIDE + (CMP_BLOCK - 1)
    cv = c_last <= qpos
    sub = lax.broadcasted_iota(jnp.int32, (8, n_piece), 0)
    for i in range(n_samp):
        kc = kc_all[i * n_piece:(i + 1) * n_piece]
        vc = vc_all[i * n_piece:(i + 1) * n_piece]
        qa = _sample_queries(q_ref[i])
        s = jnp.where(cv, _dot_nt(qa.astype(BF), kc), NEG_INF)
        e = jnp.where(cv, jnp.exp(s - jnp.max(s, axis=-1, keepdims=True)), 0.0)
        p = e / jnp.maximum(jnp.sum(e, axis=-1, keepdims=True), 1e-30)
        ocmp_ref[i] = _dot(p.astype(BF), vc)
        ps0 = jnp.sum(jnp.where(sub < GQA_R, p, 0.0), axis=0, keepdims=True)
        ps1 = jnp.sum(jnp.where(sub >= GQA_R, p, 0.0), axis=0, keepdims=True)
        p_sum = jnp.where(sub == 0, ps0, jnp.where(sub == 1, ps1, 0.0))
        imp_ref[i] = _split_dot(p_sum, cover_ref[...])


def _samples_per_step(db):
    return SAMPLES_PER_STEP if db % SAMPLES_PER_STEP == 0 else 1


def _page_specs(n_samp, n_pages, layer, rows, page, kvs):
    def spec(kv, i, j):
        if rows == LANES:
            return pl.BlockSpec((None, None, LANES, page), lambda b, pt: (layer, pt[b * n_samp + i, j], kv, 0))
        return pl.BlockSpec((None, None, rows, page), lambda b, pt: (layer, pt[b * n_samp + i, j], 0, 0))
    return [spec(kv, i, j) for kv in kvs for i in range(n_samp) for j in range(n_pages)]


def _s_cmp(page_table, q3, cache_l, layer, perm, wbig, pebig, cover, qpos):
    db = q3.shape[0]
    n_pages = page_table.shape[1]
    page = cache_l.shape[3]
    n_piece = n_pages * page // CMP_STRIDE
    n_samp = _samples_per_step(db)
    page_specs = _page_specs(n_samp, n_pages, layer, 2 * LANES, page, (0,))
    per_b = lambda r, w: pl.BlockSpec((n_samp, r, w), lambda b, pt: (b, 0, 0))
    gs = pltpu.PrefetchScalarGridSpec(
        num_scalar_prefetch=1,
        grid=(db // n_samp,),
        in_specs=[per_b(1, q3.shape[2])] + page_specs + [
            pl.BlockSpec(perm.shape, lambda b, pt: (0, 0)),
            pl.BlockSpec(wbig.shape, lambda b, pt: (0, 0, 0, 0)),
            pl.BlockSpec(pebig.shape, lambda b, pt: (0, 0, 0)),
            pl.BlockSpec(cover.shape, lambda b, pt: (0, 0))],
        out_specs=[per_b(8, LANES), per_b(8, LANES)],
    )
    assert cover.shape[0] == n_piece
    return pl.pallas_call(
        functools.partial(_s_cmp_kernel, n_pages=n_pages, n_samp=n_samp, qpos=qpos),
        grid_spec=gs,
        out_shape=[jax.ShapeDtypeStruct((db, 8, LANES), F32), jax.ShapeDtypeStruct((db, 8, LANES), F32)],
        compiler_params=_cparams(("arbitrary",)),
        name="nsa_sample_cmp",
    )(page_table, q3, *([cache_l] * (n_samp * n_pages)), perm, wbig, pebig, cover)


def _s_select_kernel(imp_ref, e_ref, o_ref, *, qpos):
    imp = imp_ref[...]
    pos = jnp.full((imp.shape[0], 1), qpos, jnp.int32)
    imp, blk_f, valid = _block_scores(imp, pos)
    sel = _top_select(imp, blk_f)
    sel_b = jnp.where(valid, sel, 0.0).astype(BF)
    o_ref[...] = _dot(sel_b, e_ref[...])


def _s_select(imp2, expand, qpos):
    m = imp2.shape[0]
    past = expand.shape[1]
    return pl.pallas_call(
        functools.partial(_s_select_kernel, qpos=qpos),
        grid=(1,),
        in_specs=[pl.BlockSpec(imp2.shape, lambda i: (0, 0)), pl.BlockSpec(expand.shape, lambda i: (0, 0))],
        out_specs=pl.BlockSpec((m, past), lambda i: (0, 0)),
        out_shape=jax.ShapeDtypeStruct((m, past), F32),
        compiler_params=_cparams(("arbitrary",)),
        name="nsa_sample_select",
    )(imp2, expand)


def _s_att_kernel(pt_ref, q_ref, gate_ref, mask_ref, ocmp_ref, slcn_ref, winn_ref, win_ref, *refs,
                  n_pages, n_samp, qpos, past):
    o_ref, wout_ref = refs[n_samp * n_pages + 1:]
    for i in range(n_samp):
        _s_att_one(q_ref.at[i], gate_ref.at[i], mask_ref.at[i], ocmp_ref.at[i], slcn_ref.at[i], winn_ref.at[i],
                   win_ref.at[i], refs[i * n_pages:(i + 1) * n_pages], o_ref.at[i], wout_ref.at[i],
                   qpos=qpos, past=past)


def _s_att_one(q_ref, gate_ref, mask_ref, ocmp_ref, slcn_ref, winn_ref, win_ref, pages, o_ref, wout_ref,
               *, qpos, past):
    page = pages[0].shape[1]
    qa = _sample_queries(q_ref[...])
    qb = qa.astype(BF)
    sub = lax.broadcasted_iota(jnp.int32, (8, 1), 0)
    mrow = jnp.where(sub < GQA_R, mask_ref[0:1, :], mask_ref[1:2, :])
    bias = jnp.where(mrow > 0.5, 0.0, NEG_INF)
    sc = jnp.concatenate([_dot(qb, pg[0:LANES, :].astype(BF)) for pg in pages], axis=1) + bias
    kn = slcn_ref[:, 0:LANES]
    vn = slcn_ref[:, LANES:2 * LANES]
    s_new = jnp.sum(qa * kn, axis=-1, keepdims=True)
    m = jnp.maximum(jnp.max(sc, axis=-1, keepdims=True), s_new)
    e = jnp.exp(sc - m)
    e_new = jnp.exp(s_new - m)
    acc = e_new * vn
    for j, pg in enumerate(pages):
        acc = acc + _dot_nt(e[:, j * page:(j + 1) * page].astype(BF), pg[LANES:2 * LANES, :].astype(BF))
    o_slc = acc / (jnp.sum(e, axis=-1, keepdims=True) + e_new)
    wb = win_ref.shape[1]
    wpos = (past - wb) + lax.broadcasted_iota(jnp.int32, (1, wb), 1)
    wvalid = (wpos <= qpos) & (wpos > qpos - WINDOW) & (wpos >= 0)
    sw = _dot(qb, win_ref[0:LANES, :].astype(BF)) + jnp.where(wvalid, 0.0, NEG_INF)
    kwn = winn_ref[:, 0:LANES]
    vwn = winn_ref[:, LANES:2 * LANES]
    sw_new = jnp.sum(qa * kwn, axis=-1, keepdims=True)
    mw = jnp.maximum(jnp.max(sw, axis=-1, keepdims=True), sw_new)
    ew = jnp.exp(sw - mw)
    ew_new = jnp.exp(sw_new - mw)
    o_win = (_dot_nt(ew.astype(BF), win_ref[LANES:2 * LANES, :].astype(BF)) + ew_new * vwn) / (
        jnp.sum(ew, axis=-1, keepdims=True) + ew_new)
    lane = lax.broadcasted_iota(jnp.int32, (8, LANES), 1)
    gt = jnp.broadcast_to(gate_ref[...], (8, LANES))
    gsel = [jnp.sum(jnp.where(lane == 3 * sub + br, gt, 0.0), axis=-1, keepdims=True) for br in range(N_BRANCH)]
    res = gsel[0] * ocmp_ref[...] + gsel[1] * o_slc + gsel[2] * o_win
    o_ref[...] = _merge_heads([res[h:h + 1, :] for h in range(8)], 1)
    rows = winn_ref.shape[1]
    diag = lax.broadcasted_iota(jnp.int32, (rows, rows), 0) == lax.broadcasted_iota(jnp.int32, (rows, rows), 1)
    col = jnp.sum(jnp.where(diag, jnp.broadcast_to(winn_ref[...], (rows, rows)), 0.0), axis=-1, keepdims=True)
    last = lax.broadcasted_iota(jnp.int32, (1, wb), 1) == wb - 1
    wout_ref[...] = jnp.where(last, col, pltpu.roll(win_ref[...], wb - 1, 1))


def _s_att(page_table, q3, gates3, mask3, ocmp, slc_new3, win_new3, win_l, win_buf, cache_l, layer, qpos, past):
    db = q3.shape[0]
    n_pages = page_table.shape[1]
    page = cache_l.shape[3]
    wb = win_l.shape[3]
    n_samp = _samples_per_step(db)
    page_specs = _page_specs(n_samp, n_pages, layer, 2 * LANES, page, (0,))
    per_b = lambda r, w: pl.BlockSpec((n_samp, r, w), lambda b, pt: (b, 0, 0))
    win_spec = pl.BlockSpec((None, n_samp, 2 * LANES, wb), lambda b, pt: (layer, b, 0, 0))
    in_specs = [per_b(1, q3.shape[2]), per_b(1, LANES), per_b(2, past), per_b(8, LANES), per_b(1, 2 * LANES),
                per_b(1, 2 * LANES), win_spec] + page_specs + [pl.BlockSpec(memory_space=pl.ANY)]
    gs = pltpu.PrefetchScalarGridSpec(
        num_scalar_prefetch=1,
        grid=(db // n_samp,),
        in_specs=in_specs,
        out_specs=[per_b(1, q3.shape[2]), win_spec],
    )
    return pl.pallas_call(
        functools.partial(_s_att_kernel, n_pages=n_pages, n_samp=n_samp, qpos=qpos, past=past),
        grid_spec=gs,
        out_shape=[jax.ShapeDtypeStruct(q3.shape, F32), jax.ShapeDtypeStruct(win_buf.shape, F32)],
        input_output_aliases={len(in_specs): 1},
        compiler_params=_cparams(("arbitrary",)),
        name="nsa_sample_att",
    )(page_table, q3, gates3, mask3, ocmp, slc_new3, win_new3, win_l, *([cache_l] * (n_samp * n_pages)), win_buf)


def _untile(ref, tm):
    return jnp.concatenate([ref[pl.ds(c, tm, stride=TOKEN_TILE_ROWS), :] for c in range(TOKEN_TILE_ROWS)], axis=1)


def _store_tiled(ref, val):
    tm = val.shape[0]
    for c in range(TOKEN_TILE_ROWS):
        ref[pl.ds(c, tm, stride=TOKEN_TILE_ROWS), :] = val[:, c * LANES:(c + 1) * LANES]


def _post_kernel(a_ref, o_ref, x_ref, g1_ref, og_ref, wout_ref, bd_ref, ng_ref, sh2_ref, sc2_ref, y_ref, h2_ref,
                 *, tiled_h2):
    bd = bd_ref[...]
    d_a = a_ref.shape[1]

    def head_norm(t, g):
        return t * lax.rsqrt(_split_dot(t * t, bd) + EPS) * g

    a = head_norm(a_ref[...], og_ref[:, :d_a])
    o = head_norm(o_ref[...], og_ref[:, d_a:])
    mix = _dot(a.astype(BF), wout_ref[:d_a, :]) + _dot(o.astype(BF), wout_ref[d_a:, :])
    y = x_ref[...] + g1_ref[...] * mix
    y_ref[...] = y
    ms = jnp.mean(y * y, axis=-1, keepdims=True)
    h2 = y * lax.rsqrt(ms + EPS) * ng_ref[...]
    h2 = h2 * (1.0 + sc2_ref[...]) + sh2_ref[...]
    if tiled_h2:
        _store_tiled(h2_ref, h2)
    else:
        h2_ref[...] = h2


def _mod_spec(per_batch, tm, d, tokens_per_batch):
    if per_batch:
        tpb = tokens_per_batch // tm
        return pl.BlockSpec((None, 1, d), lambda i: (i // tpb, 0, 0))
    return pl.BlockSpec((tm, d), lambda i: (i, 0))


def _post_mix(a, o, x, gate1, out_g, wout_bf, bd, ffn_g, shift2, scale2, *, per_batch, tokens_per_batch, tiled_h2):
    n, d = x.shape
    tm = 512 if per_batch else n
    full = lambda shp: pl.BlockSpec(shp, lambda i: (0,) * len(shp))
    row = lambda w: pl.BlockSpec((tm, w), lambda i: (i, 0))
    ms = _mod_spec(per_batch, tm, d, tokens_per_batch)
    if tiled_h2:
        assert d == TOKEN_TILE_ROWS * LANES
        h2_spec = pl.BlockSpec((tm * TOKEN_TILE_ROWS, LANES), lambda i: (i, 0))
        h2_shape = jax.ShapeDtypeStruct((n * TOKEN_TILE_ROWS, LANES), F32)
    else:
        h2_spec, h2_shape = row(d), jax.ShapeDtypeStruct((n, d), F32)
    return pl.pallas_call(
        functools.partial(_post_kernel, tiled_h2=tiled_h2),
        grid=(n // tm,),
        in_specs=[row(a.shape[1]), row(o.shape[1]), row(d), ms, full((1, d)), full(wout_bf.shape), full(bd.shape),
                  full((1, d)), ms, ms],
        out_specs=[row(d), h2_spec],
        out_shape=[jax.ShapeDtypeStruct((n, d), F32), h2_shape],
        compiler_params=_cparams(("arbitrary",)),
        name="post_mix",
    )(a, o, x, gate1, out_g, wout_bf, bd, ffn_g, shift2, scale2)


def _ffn_kernel(h_ref, y_ref, g2_ref, wg_ref, wu_ref, wd_ref, o_ref, *, f_tile):
    hb = h_ref[...].astype(BF)
    d_ff = wg_ref.shape[1]
    acc = jnp.zeros(o_ref.shape, F32)
    for f in range(d_ff // f_tile):
        fs = slice(f * f_tile, (f + 1) * f_tile)
        gp = _dot(hb, wg_ref[:, fs])
        up = _dot(hb, wu_ref[:, fs])
        act = gp * jax.nn.sigmoid(gp) * up
        acc = acc + _dot(act.astype(BF), wd_ref[fs, :])
    o_ref[...] = y_ref[...] + g2_ref[...] * acc


def _ff_tile(d_ff):
    for cand in (1408, 1024, 512, 256, 128):
        if d_ff % cand == 0:
            return cand
    return d_ff


def _ffn_dense(h2, y, gate2, wg, wu, wd, *, per_batch, tokens_per_batch):
    n, d = y.shape
    tm = 512 if per_batch else n
    full = lambda shp: pl.BlockSpec(shp, lambda i: (0,) * len(shp))
    row = lambda w: pl.BlockSpec((tm, w), lambda i: (i, 0))
    return pl.pallas_call(
        functools.partial(_ffn_kernel, f_tile=_ff_tile(wg.shape[1])),
        grid=(n // tm,),
        in_specs=[row(d), row(d), _mod_spec(per_batch, tm, d, tokens_per_batch), full(wg.shape), full(wu.shape),
                  full(wd.shape)],
        out_specs=row(d),
        out_shape=jax.ShapeDtypeStruct((n, d), F32),
        compiler_params=_cparams(("arbitrary",)),
        name="ffn_dense",
    )(h2, y, gate2, wg, wu, wd)


def _router_kernel(h_ref, rwh_ref, rwl_ref, rb_ref, o_ref, *, n_experts):
    h = _untile(h_ref, o_ref.shape[0])
    hi = h.astype(BF)
    lo = (h - hi.astype(F32)).astype(BF)
    logits = _dot(hi, rwh_ref[...]) + _dot(lo, rwh_ref[...]) + _dot(hi, rwl_ref[...]) + rb_ref[...]
    lane = lax.broadcasted_iota(jnp.int32, logits.shape, 1)
    lane_f = lane.astype(F32)
    l1 = jnp.where(lane < n_experts, logits, -jnp.inf)
    m1 = jnp.max(l1, axis=-1, keepdims=True)
    i1 = jnp.min(jnp.where(l1 == m1, lane_f, float(LANES)), axis=-1, keepdims=True)
    l2 = jnp.where(lane_f == i1, -jnp.inf, l1)
    m2 = jnp.max(l2, axis=-1, keepdims=True)
    i2 = jnp.min(jnp.where(l2 == m2, lane_f, float(LANES)), axis=-1, keepdims=True)
    e2 = jnp.exp(m2 - m1)
    den = 1.0 + e2
    o_ref[...] = jnp.where(lane == 0, i1, jnp.where(lane == 1, i2, jnp.where(lane == 2, 1.0 / den,
                           jnp.where(lane == 3, e2 / den, 0.0))))


def _router(h2t, rw_hi, rw_lo, rb, n_experts):
    n = h2t.shape[0] // TOKEN_TILE_ROWS
    tm = 512 if n % 512 == 0 else n
    full = lambda shp: pl.BlockSpec(shp, lambda i: (0,) * len(shp))
    return pl.pallas_call(
        functools.partial(_router_kernel, n_experts=n_experts),
        grid=(n // tm,),
        in_specs=[pl.BlockSpec((tm * TOKEN_TILE_ROWS, LANES), lambda i: (i, 0)), full(rw_hi.shape),
                  full(rw_lo.shape), full(rb.shape)],
        out_specs=pl.BlockSpec((tm, LANES), lambda i: (i, 0)),
        out_shape=jax.ShapeDtypeStruct((n, LANES), F32),
        compiler_params=_cparams(("arbitrary",)),
        name="moe_router",
    )(h2t, rw_hi, rw_lo, rb)


def _gather_kernel(idx_ref, src_ref, out_ref, sem, *, rows_per_step):
    base = pl.program_id(0) * rows_per_step

    def row_copy(j):
        return pltpu.make_async_copy(src_ref.at[idx_ref[base + j]], out_ref.at[j], sem)

    def start(j, c):
        row_copy(j).start()
        return c

    def wait(j, c):
        row_copy(j).wait()
        return c

    lax.fori_loop(0, rows_per_step, start, 0, unroll=16)
    lax.fori_loop(0, rows_per_step, wait, 0, unroll=16)


def _gather_rows(src_t, idx, rows_per_step=512):
    m = idx.shape[0]
    while m % rows_per_step:
        rows_per_step //= 2
    src3 = src_t.reshape(-1, TOKEN_TILE_ROWS, LANES)
    gs = pltpu.PrefetchScalarGridSpec(
        num_scalar_prefetch=1,
        grid=(m // rows_per_step,),
        in_specs=[pl.BlockSpec(memory_space=pl.ANY)],
        out_specs=pl.BlockSpec((rows_per_step, TOKEN_TILE_ROWS, LANES), lambda i, idx_ref: (i, 0, 0)),
        scratch_shapes=[pltpu.SemaphoreType.DMA(())],
    )
    out = pl.pallas_call(
        functools.partial(_gather_kernel, rows_per_step=rows_per_step),
        grid_spec=gs,
        out_shape=jax.ShapeDtypeStruct((m, TOKEN_TILE_ROWS, LANES), src_t.dtype),
        compiler_params=pltpu.CompilerParams(dimension_semantics=("arbitrary",), vmem_limit_bytes=VMEM_LIMIT),
        name="gather_rows",
    )(idx, src3)
    return out.reshape(m * TOKEN_TILE_ROWS, LANES)


def _moe_block_kernel(be_ref, nu_ref, x_ref, wg_ref, wu_ref, wd_ref, o_ref, *, f_tile):
    i = pl.program_id(0)

    @pl.when(i < nu_ref[0])
    def _():
        hb = _untile(x_ref, MOE_BLOCK).astype(BF)
        d_ff = wg_ref.shape[1]
        acc = jnp.zeros((MOE_BLOCK, wd_ref.shape[1]), F32)
        for f in range(d_ff // f_tile):
            fs = slice(f * f_tile, (f + 1) * f_tile)
            gp = _dot(hb, wg_ref[:, fs])
            up = _dot(hb, wu_ref[:, fs])
            act = gp * jax.nn.sigmoid(gp) * up
            acc = acc + _dot(act.astype(BF), wd_ref[fs, :])
        _store_tiled(o_ref, acc)

    @pl.when(i >= nu_ref[0])
    def _():
        o_ref[...] = jnp.zeros(o_ref.shape, F32)


def _moe_blocks(blk_e, n_used, xb_t, wg, wu, wd):
    nb = xb_t.shape[0] // (MOE_BLOCK * TOKEN_TILE_ROWS)
    d, d_ff = wg.shape[1], wg.shape[2]
    blk = pl.BlockSpec((MOE_BLOCK * TOKEN_TILE_ROWS, LANES), lambda i, be, nu: (i, 0))
    gs = pltpu.PrefetchScalarGridSpec(
        num_scalar_prefetch=2,
        grid=(nb,),
        in_specs=[blk,
                  pl.BlockSpec((None, d, d_ff), lambda i, be, nu: (be[i], 0, 0)),
                  pl.BlockSpec((None, d, d_ff), lambda i, be, nu: (be[i], 0, 0)),
                  pl.BlockSpec((None, d_ff, d), lambda i, be, nu: (be[i], 0, 0))],
        out_specs=blk,
    )
    return pl.pallas_call(
        functools.partial(_moe_block_kernel, f_tile=_ff_tile(d_ff)),
        grid_spec=gs,
        out_shape=jax.ShapeDtypeStruct(xb_t.shape, F32),
        compiler_params=_cparams(("arbitrary",)),
        name="moe_blocks",
    )(blk_e, n_used, xb_t, wg, wu, wd)


def _combine_kernel(ya0_ref, ya1_ref, r_ref, y_ref, g2_ref, o_ref):
    tm = o_ref.shape[0]
    w1 = r_ref[:, 2:3]
    w2 = r_ref[:, 3:4]
    o_ref[...] = y_ref[...] + g2_ref[...] * (w1 * _untile(ya0_ref, tm) + w2 * _untile(ya1_ref, tm))


def _moe_combine(ya_t, route, y, gate2, *, per_batch, tokens_per_batch):
    n, d = y.shape
    tm = 512 if per_batch else n
    nt = n // tm
    row = lambda w: pl.BlockSpec((tm, w), lambda i: (i, 0))
    return pl.pallas_call(
        _combine_kernel,
        grid=(nt,),
        in_specs=[pl.BlockSpec((tm * TOKEN_TILE_ROWS, LANES), lambda i: (i, 0)),
                  pl.BlockSpec((tm * TOKEN_TILE_ROWS, LANES), lambda i: (i + nt, 0)), row(LANES), row(d),
                  _mod_spec(per_batch, tm, d, tokens_per_batch)],
        out_specs=row(d),
        out_shape=jax.ShapeDtypeStruct((n, d), F32),
        compiler_params=_cparams(("arbitrary",)),
        name="moe_combine",
    )(ya_t, ya_t, route, y, gate2)


def _moe_ffn(h2t, y, gate2, rw_hi, rw_lo, rb, wg, wu, wd, *, per_batch, tokens_per_batch):
    n, d = y.shape
    n_experts = wg.shape[0]
    route = _router(h2t, rw_hi, rw_lo, rb, n_experts)
    flat_e = route[:, 0:TOP_K].astype(jnp.int32).reshape(-1)
    onehot = (flat_e[:, None] == jnp.arange(n_experts, dtype=jnp.int32)[None, :]).astype(jnp.int32)
    csum = jnp.cumsum(onehot, axis=0)
    rank = jnp.sum(onehot * csum, axis=1) - 1
    counts = csum[-1]
    padded = (counts + MOE_BLOCK - 1) // MOE_BLOCK * MOE_BLOCK
    pad_end = jnp.cumsum(padded)
    pad_start = pad_end - padded
    slot = (pad_start[flat_e] + rank).astype(jnp.int32)
    nb = -(-(n * TOP_K) // MOE_BLOCK) + n_experts
    src = jnp.zeros((nb * MOE_BLOCK,), jnp.int32).at[slot].set(jnp.arange(n * TOP_K, dtype=jnp.int32) // TOP_K)
    blk_e = jnp.minimum(jnp.sum(pad_end[None, :] <= (jnp.arange(nb) * MOE_BLOCK)[:, None], axis=-1),
                        n_experts - 1).astype(jnp.int32)
    n_used = (pad_end[-1:] // MOE_BLOCK).astype(jnp.int32)
    xb = _gather_rows(h2t, src)
    yb = _moe_blocks(blk_e, n_used, xb, wg, wu, wd)
    ya = _gather_rows(yb, jnp.concatenate([slot[0::TOP_K], slot[1::TOP_K]]))
    return _moe_combine(ya, route, y, gate2, per_batch=per_batch, tokens_per_batch=tokens_per_batch)


def _rope_tables(pos):
    half = HEAD_DIM // 2
    inv = ROPE_THETA ** (-jnp.arange(half, dtype=F32) / half)
    ang = pos.astype(F32)[:, None] * inv[None, :]
    cos, sin = jnp.cos(ang), jnp.sin(ang)
    cos_l = jnp.concatenate([cos, cos] * (LANES // HEAD_DIM), axis=1)
    sin_l = jnp.concatenate([-sin, sin] * (LANES // HEAD_DIM), axis=1)
    return cos_l, sin_l


def _cover_matrix(n_rows, n_cmp, n_slc):
    cs = np.arange(n_cmp) * CMP_STRIDE
    ss = np.arange(n_slc) * SLC_BLOCK
    cov = np.clip(np.minimum(cs[:, None] + CMP_BLOCK, ss[None, :] + SLC_BLOCK)
                  - np.maximum(cs[:, None], ss[None, :]), 0, None).astype(np.float32) / CMP_BLOCK
    out = np.zeros((n_rows, LANES), np.float32)
    out[:n_cmp, :n_slc] = cov
    return jnp.asarray(out, BF)


def _expand_matrix(n_keys):
    blk = np.arange(LANES)[:, None]
    key = np.arange(n_keys)[None, :] // SLC_BLOCK
    return jnp.asarray((blk == key).astype(np.float32), BF)


def _piece_row_permutation(page):
    per_page = page // CMP_STRIDE
    r = np.arange(page)
    src = (r % per_page) * CMP_STRIDE + r // per_page
    return jnp.asarray((src[:, None] == np.arange(page)[None, :]).astype(np.float32), BF)


def _block_diag_mean(width):
    idx = np.arange(width) // HEAD_DIM
    return jnp.asarray((idx[:, None] == idx[None, :]).astype(np.float32) / HEAD_DIM, BF)


def _compress_weights(cmp_w_l, cmp_pe_l):
    z = jnp.zeros_like(cmp_w_l)
    wbig = jnp.concatenate([jnp.concatenate([cmp_w_l, z], axis=3), jnp.concatenate([z, cmp_w_l], axis=3)],
                           axis=2).astype(BF)
    pebig = jnp.concatenate([cmp_pe_l, cmp_pe_l], axis=2)
    return wbig, pebig


def kernel(x_prompt, x_sample, cache_cmp, cache_slc, state_win, page_table, c_prompt, c_sample, norm_mix_g, norm_ffn_g, w_ada, b_ada, w_in, w_spatial, b_spatial, q_norm_g, k_norm_g, cmp_pe, cmp_w, out_norm_g, w_out, ffn_w_gate, ffn_w_up, ffn_w_down, router_w, router_b, moe_w_gate, moe_w_up, moe_w_down):
    batch, t, d = x_prompt.shape
    db, tn, _ = x_sample.shape
    depth = w_ada.shape[0]
    n_pool, page = cache_cmp.shape[1], cache_cmp.shape[2]
    n_pages = page_table.shape[1]
    past = n_pages * page
    wb = state_win.shape[2]
    d_a = d // 2
    a_groups = d_a // HEAD_DIM
    assert tn == 1 and past % CHUNK == 0 and t % KEY_TILE == 0 and d_a % LANES == 0
    assert d - d_a == B_KV_HEADS * GQA_R * HEAD_DIM and B_KV_HEADS * HEAD_DIM == LANES
    n_p = batch * t
    kvw = 2 * LANES
    p_main = 2 * d_a + (d - d_a) + N_BRANCH * kvw
    n_gate = w_in.shape[2] - p_main
    qpos_s = past

    m_rows = batch + db
    m_pad = -(-m_rows // 8) * 8
    c_all = jnp.concatenate([c_prompt, c_sample, jnp.zeros((m_pad - m_rows, d), F32)], axis=0)
    mods = _adaln(c_all, w_ada, b_ada)

    bd = _block_diag_mean(d_a)
    cos_p, sin_p = _rope_tables(jnp.arange(t, dtype=jnp.int32))
    cos_s, sin_s = _rope_tables(jnp.full((1,), qpos_s, jnp.int32))
    np_p = t // CMP_STRIDE
    cover_p = _cover_matrix(np_p, np_p - 1, t // SLC_BLOCK)
    expand_p = _expand_matrix(t)
    np_s = past // CMP_STRIDE
    ns_s = -(-(past + tn) // SLC_BLOCK)
    cover_s = _cover_matrix(np_s, np_s - 1, ns_s)
    expand_s = _expand_matrix(past)
    tril = jnp.tril(jnp.ones((CHUNK, CHUNK), dtype=bool))

    cache_cmp4 = jnp.transpose(cache_cmp, (0, 1, 3, 4, 5, 2)).reshape(depth, n_pool, kvw, page)
    perm_s = _piece_row_permutation(page)
    cache_slc4 = jnp.transpose(cache_slc, (0, 1, 3, 4, 5, 2)).reshape(depth, n_pool, kvw, page)
    state_win4 = jnp.transpose(state_win, (0, 1, 3, 4, 5, 2)).reshape(depth, db, kvw, wb)

    yp = x_prompt.reshape(n_p, d)
    ys = x_sample.reshape(db, d)
    outs = {k: [] for k in ("cmp_s", "slc_s", "chv_s")}
    win_buf = jnp.zeros((depth, db, kvw, wb), F32)
    kv_bufs = [jnp.zeros((depth, batch, kvw, t), F32) for _ in range(N_BRANCH)]
    for l in range(depth):
        mod_p = [mods[l, :batch, k * d:(k + 1) * d].reshape(batch, 1, d) for k in range(6)]
        mod_s = [mods[l, batch:batch + db, k * d:(k + 1) * d] for k in range(6)]
        w_in_bf = jnp.concatenate([w_in[l], jnp.zeros((d, LANES - n_gate), F32)], axis=1).astype(BF)
        qg = jnp.tile(q_norm_g[l], (d - d_a) // HEAD_DIM)[None, :]
        kg = jnp.tile(k_norm_g[l], (1, LANES // HEAD_DIM))
        ws_p = jnp.where(tril, w_spatial[l], 0.0).astype(BF)
        bs_p = jnp.repeat(b_spatial[l].T, HEAD_DIM, axis=1)
        ws_s = jnp.repeat(w_spatial[l][:, 0, 0], HEAD_DIM)[None, :]
        bs_s = jnp.repeat(b_spatial[l][:, 0], HEAD_DIM)[None, :]
        wbig, pebig = _compress_weights(cmp_w[l], cmp_pe[l])
        out_g = out_norm_g[l].reshape(1, d)
        wout_bf = w_out[l].astype(BF)
        ng1 = norm_mix_g[l][None, :]
        ng2 = norm_ffn_g[l][None, :]

        a, q, cmp_r, slc_bf, win_bf, gates, *kv_bufs = _pre_mix(
            yp, ng1, mod_p[0], mod_p[1], w_in_bf, cos_p, sin_p, bd, qg, kg, ws_p, bs_p,
            chunked=True, tokens_per_batch=t, kv_bufs=kv_bufs, layer=l)
        kc, vc = _compress_prompt(cmp_r, wbig, pebig, batch)
        o = _nsa_prompt(q, gates, kc, vc, slc_bf, win_bf, cover_p, expand_p, batch)
        yp, h2p = _post_mix(a, o, yp, mod_p[2], out_g, wout_bf, bd, ng2, mod_p[3], mod_p[4],
                            per_batch=True, tokens_per_batch=t, tiled_h2=(l % 2 == 1))

        a_s, q_s, cmp_s, slc_s, win_s, gates_s, v_s = _pre_mix(
            ys, ng1, mod_s[0], mod_s[1], w_in_bf, cos_s, sin_s, bd, qg, kg, ws_s, bs_s,
            chunked=False, tokens_per_batch=1)
        q3 = q_s.reshape(db, 1, d - d_a)
        ocmp, imp = _s_cmp(page_table, q3, cache_cmp4, l, perm_s, wbig, pebig, cover_s, qpos_s)
        mask = _s_select(imp[:, 0:B_KV_HEADS, :].reshape(db * B_KV_HEADS, LANES), expand_s, qpos_s)
        o_s, win_buf = _s_att(page_table, q3, gates_s.reshape(db, 1, LANES), mask.reshape(db, B_KV_HEADS, past),
                              ocmp, slc_s.reshape(db, 1, kvw), win_s.reshape(db, 1, kvw), state_win4, win_buf,
                              cache_slc4, l, qpos_s, past)
        ys, h2s = _post_mix(a_s, o_s.reshape(db, d - d_a), ys, mod_s[2], out_g, wout_bf, bd, ng2, mod_s[3],
                            mod_s[4], per_batch=False, tokens_per_batch=1, tiled_h2=(l % 2 == 1))
        outs["cmp_s"].append(cmp_s.reshape(db, tn, 2, B_KV_HEADS, HEAD_DIM))
        outs["slc_s"].append(slc_s.reshape(db, tn, 2, B_KV_HEADS, HEAD_DIM))
        outs["chv_s"].append(v_s.reshape(db, tn, d_a))

        i = l // 2
        if l % 2 == 0:
            wg, wu, wd = ffn_w_gate[i].astype(BF), ffn_w_up[i].astype(BF), ffn_w_down[i].astype(BF)
            yp = _ffn_dense(h2p, yp, mod_p[5], wg, wu, wd, per_batch=True, tokens_per_batch=t)
            ys = _ffn_dense(h2s, ys, mod_s[5], wg, wu, wd, per_batch=False, tokens_per_batch=1)
        else:
            wg, wu, wd = moe_w_gate[i].astype(BF), moe_w_up[i].astype(BF), moe_w_down[i].astype(BF)
            n_exp = router_w.shape[2]
            rw = jnp.concatenate([router_w[i], jnp.zeros((d, LANES - n_exp), F32)], axis=1)
            rw_hi = rw.astype(BF)
            rw_lo = (rw - rw_hi.astype(F32)).astype(BF)
            rb = jnp.concatenate([router_b[i], jnp.zeros((LANES - n_exp,), F32)])[None, :]
            yp = _moe_ffn(h2p, yp, mod_p[5], rw_hi, rw_lo, rb, wg, wu, wd, per_batch=True, tokens_per_batch=t)
            ys = _moe_ffn(h2s, ys, mod_s[5], rw_hi, rw_lo, rb, wg, wu, wd, per_batch=False, tokens_per_batch=1)

    st = lambda k: jnp.stack(outs[k])

    def rows_view(buf):
        return jnp.transpose(buf.reshape(buf.shape[0], buf.shape[1], 2, B_KV_HEADS, HEAD_DIM, buf.shape[3]),
                             (0, 1, 5, 2, 3, 4))

    wbp = min(WINDOW, t)
    return (yp.reshape(batch, t, d), ys.reshape(db, tn, d), rows_view(kv_bufs[0]), st("cmp_s"),
            rows_view(kv_bufs[1]), st("slc_s"), rows_view(kv_bufs[2][:, :, :, t - wbp:]), rows_view(win_buf),
            st("chv_s"))
```

```python
import functools

import numpy as np
import jax
import jax.numpy as jnp
from jax import lax
from jax.experimental import pallas as pl
from jax.experimental.pallas import tpu as pltpu

F32 = jnp.float32
BF = jnp.bfloat16

HEAD_DIM = 64
B_KV_HEADS = 2
GQA_R = 4
N_BRANCH = 3
CHUNK = 128
CMP_BLOCK = 32
CMP_STRIDE = 16
SLC_BLOCK = 64
N_SELECT = 16
WINDOW = 512
Q_BLOCK = 128
ROPE_THETA = 10000.0
TOP_K = 2
MOE_BLOCK = 128
NEG_INF = -1e30
FORCE_SCORE = 1e30
EPS = 1e-6
LOG2_E = 1.4426950408889634

LANES = 128
TOKEN_TILE_ROWS = 8
GATHER_GROUP = 16
SAMPLES_PER_STEP = 4
KEY_TILE = 1024
VMEM_LIMIT = 56 * 1024 * 1024


def _cparams(sem):
    return pltpu.CompilerParams(dimension_semantics=sem, vmem_limit_bytes=VMEM_LIMIT)


def _dot(a, b):
    return jnp.dot(a, b, preferred_element_type=F32)


def _dot_nt(a, b):
    return lax.dot_general(a, b, (((1,), (1,)), ((), ())), preferred_element_type=F32)


def _split_dot(x, w):
    hi = x.astype(BF)
    lo = (x - hi.astype(F32)).astype(BF)
    return _dot(hi, w) + _dot(lo, w)


def _ada_kernel(c_ref, w_ref, b_ref, o_ref):
    c = c_ref[...]
    s = c * jax.nn.sigmoid(c)
    o_ref[0] = _dot(s.astype(BF), w_ref[0].astype(BF)) + b_ref[0]


def _adaln(c_all, w_ada, b_ada):
    depth, d, n6 = w_ada.shape
    m = c_all.shape[0]
    tn = 1536
    return pl.pallas_call(
        _ada_kernel,
        grid=(depth, n6 // tn),
        in_specs=[pl.BlockSpec((m, d), lambda l, j: (0, 0)),
                  pl.BlockSpec((1, d, tn), lambda l, j: (l, 0, j)),
                  pl.BlockSpec((1, 1, tn), lambda l, j: (l, 0, j))],
        out_specs=pl.BlockSpec((1, m, tn), lambda l, j: (l, 0, j)),
        out_shape=jax.ShapeDtypeStruct((depth, m, n6), F32),
        compiler_params=_cparams(("arbitrary", "arbitrary")),
        name="adaln",
    )(c_all, w_ada, b_ada.reshape(depth, 1, n6))


def _pre_kernel(x_ref, g_ref, sh_ref, sc_ref, w_ref, cos_ref, sin_ref, bd_ref, qg_ref, kg_ref,
                ws_ref, bs_ref, *out_refs, chunked, d_a, d_b):
    if chunked:
        a_ref, q_ref, cmp_ref, slcb_ref, winb_ref, gate_ref, cmpt_ref, slct_ref, wint_ref = out_refs[N_BRANCH:]
        row_refs = (cmp_ref, None, None)
        bf_refs = (None, slcb_ref, winb_ref)
        t_refs = (cmpt_ref, slct_ref, wint_ref)
    else:
        a_ref, q_ref, cmp_ref, slc_ref, win_ref, gate_ref, v_ref = out_refs
        row_refs = (cmp_ref, slc_ref, win_ref)
        bf_refs = t_refs = (None, None, None)
    tm = x_ref.shape[0]
    x = x_ref[...]
    ms = jnp.mean(x * x, axis=-1, keepdims=True)
    h = x * lax.rsqrt(ms + EPS) * g_ref[...]
    h = h * (1.0 + sc_ref[...]) + sh_ref[...]
    z = _dot(h.astype(BF), w_ref[...])
    uv = jax.nn.gelu(z[:, :2 * d_a])
    u = uv[:, :d_a]
    v = uv[:, d_a:]
    bd = bd_ref[...]
    mu = _split_dot(v, bd)
    vc = v - mu
    vn = vc * lax.rsqrt(_split_dot(vc * vc, bd) + EPS)

    cos = cos_ref[...]
    sin = sin_ref[...]
    lane = lax.broadcasted_iota(jnp.int32, (1, LANES), 1)
    first_half = (lane % HEAD_DIM) < (HEAD_DIM // 2)

    def rope(t):
        swapped = jnp.where(first_half, pltpu.roll(t, LANES - HEAD_DIM // 2, 1),
                            pltpu.roll(t, HEAD_DIM // 2, 1))
        return t * cos + swapped * sin

    o = 2 * d_a
    zq = z[:, o:o + d_b]
    qn = zq * lax.rsqrt(_split_dot(zq * zq, bd) + EPS) * qg_ref[...]
    for c in range(d_b // LANES):
        q_ref[:, c * LANES:(c + 1) * LANES] = rope(qn[:, c * LANES:(c + 1) * LANES]) * (HEAD_DIM ** -0.5)
    o += d_b
    bd1 = bd_ref[0:LANES, 0:LANES]
    for br in range(N_BRANCH):
        kp = z[:, o:o + LANES]
        vp = z[:, o + LANES:o + 2 * LANES]
        kn = kp * lax.rsqrt(_split_dot(kp * kp, bd1) + EPS) * kg_ref[br:br + 1, :]
        kr = rope(kn)
        if row_refs[br] is not None:
            row_refs[br][:, 0:LANES] = kr
            row_refs[br][:, LANES:2 * LANES] = vp
        if t_refs[br] is not None:
            t_refs[br][0:LANES, :] = jnp.transpose(kr)
            t_refs[br][LANES:2 * LANES, :] = jnp.transpose(vp)
        if bf_refs[br] is not None:
            bf_refs[br][:, 0:LANES] = kr.astype(BF)
            bf_refs[br][:, LANES:2 * LANES] = vp.astype(BF)
        o += 2 * LANES
    gate_ref[...] = jax.nn.sigmoid(z[:, o:o + LANES])

    if chunked:
        low = lane < HEAD_DIM
        for c in range(tm // CHUNK):
            rs = slice(c * CHUNK, (c + 1) * CHUNK)
            for j in range(d_a // LANES):
                cs = slice(j * LANES, (j + 1) * LANES)
                vpair = vn[rs, cs].astype(BF)
                m0 = _dot(ws_ref[2 * j], vpair)
                m1 = _dot(ws_ref[2 * j + 1], vpair)
                mixed = jnp.where(low, m0, m1) + bs_ref[:, cs]
                a_ref[rs, cs] = u[rs, cs] * mixed
    else:
        a_ref[...] = u * (vn * ws_ref[...] + bs_ref[...])
        v_ref[...] = vn


def _pre_mix(x, norm_g, shift, scale, w_in_bf, cos, sin, bd, qg, kg, ws, bs, *, chunked, tokens_per_batch,
             kv_bufs=(), layer=0):
    n, d = x.shape
    d_a = d_b = d // 2
    tm = 256 if chunked else n
    nt = n // tm
    pw = w_in_bf.shape[1]
    full = lambda shp: pl.BlockSpec(shp, lambda i: (0,) * len(shp))
    row = lambda w: pl.BlockSpec((tm, w), lambda i: (i, 0))
    extra_in, aliases = [], {}
    if chunked:
        tpb = tokens_per_batch // tm
        mod_spec = pl.BlockSpec((None, 1, d), lambda i: (i // tpb, 0, 0))
        rot_spec = pl.BlockSpec((tm, LANES), lambda i: (i % tpb, 0))
        ws_spec, bs_spec = full(ws.shape), full(bs.shape)
        out_shape = [jax.ShapeDtypeStruct((n, d_a), F32), jax.ShapeDtypeStruct((n, d_b), F32),
                     jax.ShapeDtypeStruct((n, 2 * LANES), F32), jax.ShapeDtypeStruct((n, 2 * LANES), BF),
                     jax.ShapeDtypeStruct((n, 2 * LANES), BF), jax.ShapeDtypeStruct((n, LANES), F32)]
        out_specs = [row(d_a), row(d_b), row(2 * LANES), row(2 * LANES), row(2 * LANES), row(LANES)]
        t_spec = pl.BlockSpec((None, None, 2 * LANES, tm), lambda i: (layer, i // tpb, 0, i % tpb))
        n_in = 12
        for j, buf in enumerate(kv_bufs):
            extra_in.append(pl.BlockSpec(memory_space=pl.ANY))
            aliases[n_in + j] = len(out_shape)
            out_shape.append(jax.ShapeDtypeStruct(buf.shape, F32))
            out_specs.append(t_spec)
    else:
        mod_spec = row(d)
        rot_spec = full((1, LANES))
        ws_spec, bs_spec = full(ws.shape), full(bs.shape)
        out_shape = [jax.ShapeDtypeStruct((n, d_a), F32), jax.ShapeDtypeStruct((n, d_b), F32),
                     jax.ShapeDtypeStruct((n, 2 * LANES), F32), jax.ShapeDtypeStruct((n, 2 * LANES), F32),
                     jax.ShapeDtypeStruct((n, 2 * LANES), F32), jax.ShapeDtypeStruct((n, LANES), F32),
                     jax.ShapeDtypeStruct((n, d_a), F32)]
        out_specs = [row(d_a), row(d_b), row(2 * LANES), row(2 * LANES), row(2 * LANES), row(LANES), row(d_a)]
    return pl.pallas_call(
        functools.partial(_pre_kernel, chunked=chunked, d_a=d_a, d_b=d_b),
        grid=(nt,),
        in_specs=[row(d), full((1, d)), mod_spec, mod_spec, full((d, pw)), rot_spec, rot_spec,
                  full(bd.shape), full(qg.shape), full(kg.shape), ws_spec, bs_spec] + extra_in,
        out_specs=out_specs,
        out_shape=out_shape,
        input_output_aliases=aliases,
        compiler_params=_cparams(("arbitrary",)),
        name="pre_mix_prompt" if chunked else "pre_mix_sample",
    )(x, norm_g, shift, scale, w_in_bf, cos, sin, bd, qg, kg, ws, bs, *kv_bufs)


def _compress_pieces(load_piece_rows, w_ref, pe_ref, kv, n_piece, seg=None):
    assert CMP_BLOCK == 2 * CMP_STRIDE
    seg = n_piece if seg is None else seg
    acc_a = jnp.zeros((n_piece, LANES), F32)
    acc_b = jnp.zeros((n_piece, LANES), F32)
    for l in range(CMP_STRIDE):
        xl = load_piece_rows(l)
        l2 = CMP_STRIDE + l
        acc_a = acc_a + _dot((xl + pe_ref[kv, l:l + 1, :]).astype(BF), w_ref[kv, l])
        acc_b = acc_b + _dot((xl + pe_ref[kv, l2:l2 + 1, :]).astype(BF), w_ref[kv, l2])
    shifted = pltpu.roll(acc_b, n_piece - 1, 0)
    rows = lax.broadcasted_iota(jnp.int32, (n_piece, 1), 0)
    return jnp.where(rows % seg < seg - 1, acc_a + shifted, 0.0)


def _compress_kernel(xk_ref, xv_ref, w_ref, pe_ref, kc_ref, vc_ref):
    n_piece = kc_ref.shape[0]
    for kv, (x_ref, o_ref) in enumerate(((xk_ref, kc_ref), (xv_ref, vc_ref))):
        out = _compress_pieces(lambda l: x_ref[pl.ds(l, n_piece, stride=CMP_STRIDE), :], w_ref, pe_ref, kv, n_piece)
        o_ref[...] = out.astype(BF)


def _compress_prompt(cmp_rows, wbig, pebig, batch):
    n = cmp_rows.shape[0]
    t = n // batch
    n_piece = t // CMP_STRIDE
    out = jax.ShapeDtypeStruct((batch * n_piece, LANES), BF)
    return pl.pallas_call(
        _compress_kernel,
        grid=(batch,),
        in_specs=[pl.BlockSpec((t, LANES), lambda b: (b, 0)),
                  pl.BlockSpec((t, LANES), lambda b: (b, 1)),
                  pl.BlockSpec(wbig.shape, lambda b: (0, 0, 0, 0)),
                  pl.BlockSpec(pebig.shape, lambda b: (0, 0, 0))],
        out_specs=[pl.BlockSpec((n_piece, LANES), lambda b: (b, 0))] * 2,
        out_shape=[out, out],
        compiler_params=_cparams(("arbitrary",)),
        name="compress_prompt",
    )(cmp_rows, cmp_rows, wbig, pebig)


def _group_queries(q, g):
    m = q.shape[0]
    low = lax.broadcasted_iota(jnp.int32, (m, LANES), 1) < HEAD_DIM
    keep = low if g == 0 else jnp.logical_not(low)
    parts = []
    for r in range(GQA_R):
        h = GQA_R * g + r
        t = q[:, (h // 2) * LANES:(h // 2 + 1) * LANES]
        if h % 2 != g:
            t = pltpu.roll(t, HEAD_DIM, 1)
        parts.append(jnp.where(keep, t, 0.0))
    return parts


def _top_select(imp, blk_f):
    sel = jnp.zeros(imp.shape, F32)
    work = imp
    for _ in range(N_SELECT):
        mx = jnp.max(work, axis=-1, keepdims=True)
        first = jnp.min(jnp.where(work == mx, blk_f, float(LANES)), axis=-1, keepdims=True)
        hit = blk_f == first
        sel = jnp.where(hit, 1.0, sel)
        work = jnp.where(hit, -jnp.inf, work)
    return sel


def _block_scores(imp, qpos):
    blk = lax.broadcasted_iota(jnp.int32, imp.shape, 1)
    cur = qpos // SLC_BLOCK
    forced = (blk == 0) | (blk == cur) | (blk == cur - 1)
    valid = blk <= cur
    imp = jnp.where(forced, FORCE_SCORE, imp)
    imp = jnp.where(valid, imp, NEG_INF)
    return imp, blk.astype(F32), valid


def _merge_heads(res, m):
    low = lax.broadcasted_iota(jnp.int32, (m, LANES), 1) < HEAD_DIM
    cols = []
    for c in range(4):
        a, b = res[2 * c], res[2 * c + 1]
        if c // 2 == 0:
            cols.append(jnp.where(low, a, pltpu.roll(b, HEAD_DIM, 1)))
        else:
            cols.append(jnp.where(low, pltpu.roll(a, HEAD_DIM, 1), b))
    return jnp.concatenate(cols, axis=1)


def _nsa_prompt_kernel(q_ref, gate_ref, kc_ref, vc_ref, slc_ref, win_ref, cover_ref, e_ref, o_ref):
    i = pl.program_id(1)
    s0 = i * Q_BLOCK
    n_cmp = kc_ref.shape[0]
    rows4 = GQA_R * Q_BLOCK
    q = q_ref[...] * LOG2_E
    gates = gate_ref[...]
    tpos = s0 + lax.broadcasted_iota(jnp.int32, (Q_BLOCK, 1), 0)
    tpos4 = jnp.concatenate([tpos] * GQA_R, axis=0)
    tile4 = lambda x: jnp.concatenate([x] * GQA_R, axis=0)
    groups = range(B_KV_HEADS)
    qg = [jnp.concatenate(_group_queries(q, g), axis=0).astype(BF) for g in groups]

    kc = kc_ref[...]
    vc = vc_ref[...]
    c_last = lax.broadcasted_iota(jnp.int32, (1, n_cmp), 1) * CMP_STRIDE + (CMP_BLOCK - 1)
    c_bias = jnp.where(c_last <= tpos4, 0.0, NEG_INF)
    has_cmp = jnp.where(tpos4 >= CMP_BLOCK - 1, 1.0, 0.0)
    o_cmp, p_sums = [], []
    for g in groups:
        s = _dot_nt(qg[g], kc) + c_bias
        e = jnp.exp2(s - jnp.max(s, axis=-1, keepdims=True))
        p = e * (has_cmp / jnp.sum(e, axis=-1, keepdims=True))
        o_cmp.append(_dot(p.astype(BF), vc))
        p_sum = p[0:Q_BLOCK]
        for r in range(1, GQA_R):
            p_sum = p_sum + p[r * Q_BLOCK:(r + 1) * Q_BLOCK]
        p_sums.append(p_sum)

    imp = _split_dot(jnp.concatenate(p_sums, axis=0), cover_ref[...])
    imp, blk_f, valid = _block_scores(imp, jnp.concatenate([tpos] * B_KV_HEADS, axis=0))
    sel_all = jnp.where(valid, _top_select(imp, blk_f), 0.0).astype(BF)
    sel_b = [sel_all[g * Q_BLOCK:(g + 1) * Q_BLOCK] for g in groups]

    w_start = pl.multiple_of(jnp.maximum(s0 - WINDOW, 0), Q_BLOCK)
    w_len = WINDOW + Q_BLOCK
    wpos = w_start + lax.broadcasted_iota(jnp.int32, (1, w_len), 1)
    w_bias4 = tile4(jnp.where((wpos <= tpos) & (wpos > tpos - WINDOW), 0.0, NEG_INF))
    kw = win_ref[pl.ds(w_start, w_len), 0:LANES]
    vw = win_ref[pl.ds(w_start, w_len), LANES:2 * LANES]
    low_w = lax.broadcasted_iota(jnp.int32, (1, LANES), 1) < HEAD_DIM
    keep = [low_w, jnp.logical_not(low_w)]
    with_ones = lambda v, g: jnp.where(keep[g], v, jnp.ones_like(v))
    den_lane = [HEAD_DIM, 0]

    def normalise(acc, g):
        return acc / acc[:, den_lane[g]:den_lane[g] + 1]

    o_win = []
    for g in groups:
        sw = _dot_nt(qg[g], kw) + w_bias4
        ew = jnp.exp2(sw - jnp.max(sw, axis=-1, keepdims=True))
        o_win.append(_dot(ew.astype(BF), vw) / jnp.sum(ew, axis=-1, keepdims=True))

    def body(kt, carry):
        k0 = pl.multiple_of(kt * KEY_TILE, KEY_TILE)
        kk = slc_ref[pl.ds(k0, KEY_TILE), 0:LANES]
        vv = slc_ref[pl.ds(k0, KEY_TILE), LANES:2 * LANES]
        et = e_ref[:, pl.ds(k0, KEY_TILE)]
        kpos = k0 + lax.broadcasted_iota(jnp.int32, (1, KEY_TILE), 1)
        causal = kpos <= tpos
        out = []
        for g in groups:
            m, acc = carry[g]
            bias = jnp.where((_dot(sel_b[g], et) > 0.5) & causal, 0.0, NEG_INF)
            sc = _dot_nt(qg[g], kk) + tile4(bias)
            m_new = jnp.maximum(m, jnp.max(sc, axis=-1, keepdims=True))
            pe = jnp.exp2(sc - m_new)
            acc = jnp.exp2(m - m_new) * acc + _dot(pe.astype(BF), with_ones(vv, g))
            out.append((m_new, acc))
        return tuple(out)

    init = tuple((jnp.full((rows4, 1), NEG_INF, F32), jnp.zeros((rows4, LANES), F32)) for _ in groups)
    n_tiles = (s0 + Q_BLOCK + KEY_TILE - 1) // KEY_TILE
    fin = lax.fori_loop(0, n_tiles, body, init)

    res = []
    for g in groups:
        o_slc = normalise(fin[g][1], g)
        for r in range(GQA_R):
            h = GQA_R * g + r
            rs = slice(r * Q_BLOCK, (r + 1) * Q_BLOCK)
            res.append(gates[:, 3 * h:3 * h + 1] * o_cmp[g][rs] + gates[:, 3 * h + 1:3 * h + 2] * o_slc[rs]
                       + gates[:, 3 * h + 2:3 * h + 3] * o_win[g][rs])
    o_ref[...] = _merge_heads(res, Q_BLOCK)


def _nsa_prompt(q, gates, kc, vc, slc_bf, win_bf, cover, expand, batch):
    n, d_b = q.shape
    t = n // batch
    nq = t // Q_BLOCK
    n_cmp = kc.shape[0] // batch
    return pl.pallas_call(
        _nsa_prompt_kernel,
        grid=(batch, nq),
        in_specs=[pl.BlockSpec((Q_BLOCK, d_b), lambda b, i: (b * nq + i, 0)),
                  pl.BlockSpec((Q_BLOCK, LANES), lambda b, i: (b * nq + i, 0)),
                  pl.BlockSpec((n_cmp, LANES), lambda b, i: (b, 0)),
                  pl.BlockSpec((n_cmp, LANES), lambda b, i: (b, 0)),
                  pl.BlockSpec((t, 2 * LANES), lambda b, i: (b, 0)),
                  pl.BlockSpec((t, 2 * LANES), lambda b, i: (b, 0)),
                  pl.BlockSpec(cover.shape, lambda b, i: (0, 0)),
                  pl.BlockSpec(expand.shape, lambda b, i: (0, 0))],
        out_specs=pl.BlockSpec((Q_BLOCK, d_b), lambda b, i: (b * nq + i, 0)),
        out_shape=jax.ShapeDtypeStruct((n, d_b), F32),
        compiler_params=_cparams(("arbitrary", "arbitrary")),
        name="nsa_prompt",
    )(q, gates, kc, vc, slc_bf, win_bf, cover, expand)


def _sample_queries(q_row):
    sub = lax.broadcasted_iota(jnp.int32, (8, LANES), 0)
    out = jnp.zeros((8, LANES), F32)
    for g in range(B_KV_HEADS):
        for r, part in enumerate(_group_queries(q_row, g)):
            out = jnp.where(sub == GQA_R * g + r, jnp.broadcast_to(part, (8, LANES)), out)
    return out


def _s_cmp_kernel(pt_ref, q_ref, *refs, n_pages, n_samp, qpos):
    n_pg = n_samp * n_pages
    pages = refs[:n_pg]
    perm_ref, wbig_ref, pe_ref, cover_ref, ocmp_ref, imp_ref = refs[n_pg:]
    per_page = pages[0].shape[1] // CMP_STRIDE
    n_piece = n_pages * per_page

    def loader(kv):
        xp = [_dot_nt(perm_ref[...], pg[kv * LANES:(kv + 1) * LANES, :].astype(BF)) for pg in pages]
        return lambda l: jnp.concatenate([x[l * per_page:(l + 1) * per_page, :] for x in xp], axis=0)

    kc_all = _compress_pieces(loader(0), wbig_ref, pe_ref, 0, n_samp * n_piece, n_piece).astype(BF)
    vc_all = _compress_pieces(loader(1), wbig_ref, pe_ref, 1, n_samp * n_piece, n_piece).astype(BF)
    c_last = lax.broadcasted_iota(jnp.int32, (1, n_piece), 1) * CMP_STRIDE + (CMP_BLOCK - 1)
    cv = c_last <= qpos
    sub = lax.broadcasted_iota(jnp.int32, (8, n_piece), 0)
    for i in range(n_samp):
        kc = kc_all[i * n_piece:(i + 1) * n_piece]
        vc = vc_all[i * n_piece:(i + 1) * n_piece]
        qa = _sample_queries(q_ref[i])
        s = jnp.where(cv, _dot_nt(qa.astype(BF), kc), NEG_INF)
        e = jnp.where(cv, jnp.exp(s - jnp.max(s, axis=-1, keepdims=True)), 0.0)
        p = e / jnp.maximum(jnp.sum(e, axis=-1, keepdims=True), 1e-30)
        ocmp_ref[i] = _dot(p.astype(BF), vc)
        ps0 = jnp.sum(jnp.where(sub < GQA_R, p, 0.0), axis=0, keepdims=True)
        ps1 = jnp.sum(jnp.where(sub >= GQA_R, p, 0.0), axis=0, keepdims=True)
        p_sum = jnp.where(sub == 0, ps0, jnp.where(sub == 1, ps1, 0.0))
        imp_ref[i] = _split_dot(p_sum, cover_ref[...])


def _samples_per_step(db):
    return SAMPLES_PER_STEP if db % SAMPLES_PER_STEP == 0 else 1


def _page_specs(n_samp, n_pages, layer, rows, page, kvs):
    def spec(kv, i, j):
        if rows == LANES:
            return pl.BlockSpec((None, None, LANES, page), lambda b, pt: (layer, pt[b * n_samp + i, j], kv, 0))
        return pl.BlockSpec((None, None, rows, page), lambda b, pt: (layer, pt[b * n_samp + i, j], 0, 0))
    return [spec(kv, i, j) for kv in kvs for i in range(n_samp) for j in range(n_pages)]


def _s_cmp(page_table, q3, cache_l, layer, perm, wbig, pebig, cover, qpos):
    db = q3.shape[0]
    n_pages = page_table.shape[1]
    page = cache_l.shape[3]
    n_piece = n_pages * page // CMP_STRIDE
    n_samp = _samples_per_step(db)
    page_specs = _page_specs(n_samp, n_pages, layer, 2 * LANES, page, (0,))
    per_b = lambda r, w: pl.BlockSpec((n_samp, r, w), lambda b, pt: (b, 0, 0))
    gs = pltpu.PrefetchScalarGridSpec(
        num_scalar_prefetch=1,
        grid=(db // n_samp,),
        in_specs=[per_b(1, q3.shape[2])] + page_specs + [
            pl.BlockSpec(perm.shape, lambda b, pt: (0, 0)),
            pl.BlockSpec(wbig.shape, lambda b, pt: (0, 0, 0, 0)),
            pl.BlockSpec(pebig.shape, lambda b, pt: (0, 0, 0)),
            pl.BlockSpec(cover.shape, lambda b, pt: (0, 0))],
        out_specs=[per_b(8, LANES), per_b(8, LANES)],
    )
    assert cover.shape[0] == n_piece
    return pl.pallas_call(
        functools.partial(_s_cmp_kernel, n_pages=n_pages, n_samp=n_samp, qpos=qpos),
        grid_spec=gs,
        out_shape=[jax.ShapeDtypeStruct((db, 8, LANES), F32), jax.ShapeDtypeStruct((db, 8, LANES), F32)],
        compiler_params=_cparams(("arbitrary",)),
        name="nsa_sample_cmp",
    )(page_table, q3, *([cache_l] * (n_samp * n_pages)), perm, wbig, pebig, cover)


def _s_select_kernel(imp_ref, e_ref, o_ref, *, qpos):
    imp = imp_ref[...]
    pos = jnp.full((imp.shape[0], 1), qpos, jnp.int32)
    imp, blk_f, valid = _block_scores(imp, pos)
    sel = _top_select(imp, blk_f)
    sel_b = jnp.where(valid, sel, 0.0).astype(BF)
    o_ref[...] = _dot(sel_b, e_ref[...])


def _s_select(imp2, expand, qpos):
    m = imp2.shape[0]
    past = expand.shape[1]
    return pl.pallas_call(
        functools.partial(_s_select_kernel, qpos=qpos),
        grid=(1,),
        in_specs=[pl.BlockSpec(imp2.shape, lambda i: (0, 0)), pl.BlockSpec(expand.shape, lambda i: (0, 0))],
        out_specs=pl.BlockSpec((m, past), lambda i: (0, 0)),
        out_shape=jax.ShapeDtypeStruct((m, past), F32),
        compiler_params=_cparams(("arbitrary",)),
        name="nsa_sample_select",
    )(imp2, expand)


def _s_att_kernel(pt_ref, q_ref, gate_ref, mask_ref, ocmp_ref, slcn_ref, winn_ref, win_ref, *refs,
                  n_pages, n_samp, qpos, past):
    o_ref, wout_ref = refs[n_samp * n_pages + 1:]
    for i in range(n_samp):
        _s_att_one(q_ref.at[i], gate_ref.at[i], mask_ref.at[i], ocmp_ref.at[i], slcn_ref.at[i], winn_ref.at[i],
                   win_ref.at[i], refs[i * n_pages:(i + 1) * n_pages], o_ref.at[i], wout_ref.at[i],
                   qpos=qpos, past=past)


def _s_att_one(q_ref, gate_ref, mask_ref, ocmp_ref, slcn_ref, winn_ref, win_ref, pages, o_ref, wout_ref,
               *, qpos, past):
    page = pages[0].shape[1]
    qa = _sample_queries(q_ref[...])
    qb = qa.astype(BF)
    sub = lax.broadcasted_iota(jnp.int32, (8, 1), 0)
    mrow = jnp.where(sub < GQA_R, mask_ref[0:1, :], mask_ref[1:2, :])
    bias = jnp.where(mrow > 0.5, 0.0, NEG_INF)
    sc = jnp.concatenate([_dot(qb, pg[0:LANES, :].astype(BF)) for pg in pages], axis=1) + bias
    kn = slcn_ref[:, 0:LANES]
    vn = slcn_ref[:, LANES:2 * LANES]
    s_new = jnp.sum(qa * kn, axis=-1, keepdims=True)
    m = jnp.maximum(jnp.max(sc, axis=-1, keepdims=True), s_new)
    e = jnp.exp(sc - m)
    e_new = jnp.exp(s_new - m)
    acc = e_new * vn
    for j, pg in enumerate(pages):
        acc = acc + _dot_nt(e[:, j * page:(j + 1) * page].astype(BF), pg[LANES:2 * LANES, :].astype(BF))
    o_slc = acc / (jnp.sum(e, axis=-1, keepdims=True) + e_new)
    wb = win_ref.shape[1]
    wpos = (past - wb) + lax.broadcasted_iota(jnp.int32, (1, wb), 1)
    wvalid = (wpos <= qpos) & (wpos > qpos - WINDOW) & (wpos >= 0)
    sw = _dot(qb, win_ref[0:LANES, :].astype(BF)) + jnp.where(wvalid, 0.0, NEG_INF)
    kwn = winn_ref[:, 0:LANES]
    vwn = winn_ref[:, LANES:2 * LANES]
    sw_new = jnp.sum(qa * kwn, axis=-1, keepdims=True)
    mw = jnp.maximum(jnp.max(sw, axis=-1, keepdims=True), sw_new)
    ew = jnp.exp(sw - mw)
    ew_new = jnp.exp(sw_new - mw)
    o_win = (_dot_nt(ew.astype(BF), win_ref[LANES:2 * LANES, :].astype(BF)) + ew_new * vwn) / (
        jnp.sum(ew, axis=-1, keepdims=True) + ew_new)
    lane = lax.broadcasted_iota(jnp.int32, (8, LANES), 1)
    gt = jnp.broadcast_to(gate_ref[...], (8, LANES))
    gsel = [jnp.sum(jnp.where(lane == 3 * sub + br, gt, 0.0), axis=-1, keepdims=True) for br in range(N_BRANCH)]
    res = gsel[0] * ocmp_ref[...] + gsel[1] * o_slc + gsel[2] * o_win
    o_ref[...] = _merge_heads([res[h:h + 1, :] for h in range(8)], 1)
    rows = winn_ref.shape[1]
    diag = lax.broadcasted_iota(jnp.int32, (rows, rows), 0) == lax.broadcasted_iota(jnp.int32, (rows, rows), 1)
    col = jnp.sum(jnp.where(diag, jnp.broadcast_to(winn_ref[...], (rows, rows)), 0.0), axis=-1, keepdims=True)
    last = lax.broadcasted_iota(jnp.int32, (1, wb), 1) == wb - 1
    wout_ref[...] = jnp.where(last, col, pltpu.roll(win_ref[...], wb - 1, 1))


def _s_att(page_table, q3, gates3, mask3, ocmp, slc_new3, win_new3, win_l, win_buf, cache_l, layer, qpos, past):
    db = q3.shape[0]
    n_pages = page_table.shape[1]
    page = cache_l.shape[3]
    wb = win_l.shape[3]
    n_samp = _samples_per_step(db)
    page_specs = _page_specs(n_samp, n_pages, layer, 2 * LANES, page, (0,))
    per_b = lambda r, w: pl.BlockSpec((n_samp, r, w), lambda b, pt: (b, 0, 0))
    win_spec = pl.BlockSpec((None, n_samp, 2 * LANES, wb), lambda b, pt: (layer, b, 0, 0))
    in_specs = [per_b(1, q3.shape[2]), per_b(1, LANES), per_b(2, past), per_b(8, LANES), per_b(1, 2 * LANES),
                per_b(1, 2 * LANES), win_spec] + page_specs + [pl.BlockSpec(memory_space=pl.ANY)]
    gs = pltpu.PrefetchScalarGridSpec(
        num_scalar_prefetch=1,
        grid=(db // n_samp,),
        in_specs=in_specs,
        out_specs=[per_b(1, q3.shape[2]), win_spec],
    )
    return pl.pallas_call(
        functools.partial(_s_att_kernel, n_pages=n_pages, n_samp=n_samp, qpos=qpos, past=past),
        grid_spec=gs,
        out_shape=[jax.ShapeDtypeStruct(q3.shape, F32), jax.ShapeDtypeStruct(win_buf.shape, F32)],
        input_output_aliases={len(in_specs): 1},
        compiler_params=_cparams(("arbitrary",)),
        name="nsa_sample_att",
    )(page_table, q3, gates3, mask3, ocmp, slc_new3, win_new3, win_l, *([cache_l] * (n_samp * n_pages)), win_buf)


def _untile(ref, tm):
    return jnp.concatenate([ref[pl.ds(c, tm, stride=TOKEN_TILE_ROWS), :] for c in range(TOKEN_TILE_ROWS)], axis=1)


def _store_tiled(ref, val):
    tm = val.shape[0]
    for c in range(TOKEN_TILE_ROWS):
        ref[pl.ds(c, tm, stride=TOKEN_TILE_ROWS), :] = val[:, c * LANES:(c + 1) * LANES]


def _post_kernel(a_ref, o_ref, x_ref, g1_ref, og_ref, wout_ref, bd_ref, ng_ref, sh2_ref, sc2_ref, y_ref, h2_ref,
                 *, tiled_h2):
    bd = bd_ref[...]
    d_a = a_ref.shape[1]

    def head_norm(t, g):
        return t * lax.rsqrt(_split_dot(t * t, bd) + EPS) * g

    a = head_norm(a_ref[...], og_ref[:, :d_a])
    o = head_norm(o_ref[...], og_ref[:, d_a:])
    mix = _dot(a.astype(BF), wout_ref[:d_a, :]) + _dot(o.astype(BF), wout_ref[d_a:, :])
    y = x_ref[...] + g1_ref[...] * mix
    y_ref[...] = y
    ms = jnp.mean(y * y, axis=-1, keepdims=True)
    h2 = y * lax.rsqrt(ms + EPS) * ng_ref[...]
    h2 = h2 * (1.0 + sc2_ref[...]) + sh2_ref[...]
    if tiled_h2:
        _store_tiled(h2_ref, h2)
    else:
        h2_ref[...] = h2


def _mod_spec(per_batch, tm, d, tokens_per_batch):
    if per_batch:
        tpb = tokens_per_batch // tm
        return pl.BlockSpec((None, 1, d), lambda i: (i // tpb, 0, 0))
    return pl.BlockSpec((tm, d), lambda i: (i, 0))


def _post_mix(a, o, x, gate1, out_g, wout_bf, bd, ffn_g, shift2, scale2, *, per_batch, tokens_per_batch, tiled_h2):
    n, d = x.shape
    tm = 512 if per_batch else n
    full = lambda shp: pl.BlockSpec(shp, lambda i: (0,) * len(shp))
    row = lambda w: pl.BlockSpec((tm, w), lambda i: (i, 0))
    ms = _mod_spec(per_batch, tm, d, tokens_per_batch)
    if tiled_h2:
        assert d == TOKEN_TILE_ROWS * LANES
        h2_spec = pl.BlockSpec((tm * TOKEN_TILE_ROWS, LANES), lambda i: (i, 0))
        h2_shape = jax.ShapeDtypeStruct((n * TOKEN_TILE_ROWS, LANES), F32)
    else:
        h2_spec, h2_shape = row(d), jax.ShapeDtypeStruct((n, d), F32)
    return pl.pallas_call(
        functools.partial(_post_kernel, tiled_h2=tiled_h2),
        grid=(n // tm,),
        in_specs=[row(a.shape[1]), row(o.shape[1]), row(d), ms, full((1, d)), full(wout_bf.shape), full(bd.shape),
                  full((1, d)), ms, ms],
        out_specs=[row(d), h2_spec],
        out_shape=[jax.ShapeDtypeStruct((n, d), F32), h2_shape],
        compiler_params=_cparams(("arbitrary",)),
        name="post_mix",
    )(a, o, x, gate1, out_g, wout_bf, bd, ffn_g, shift2, scale2)


def _ffn_kernel(h_ref, y_ref, g2_ref, wg_ref, wu_ref, wd_ref, o_ref, *, f_tile):
    hb = h_ref[...].astype(BF)
    d_ff = wg_ref.shape[1]
    acc = jnp.zeros(o_ref.shape, F32)
    for f in range(d_ff // f_tile):
        fs = slice(f * f_tile, (f + 1) * f_tile)
        gp = _dot(hb, wg_ref[:, fs])
        up = _dot(hb, wu_ref[:, fs])
        act = gp * jax.nn.sigmoid(gp) * up
        acc = acc + _dot(act.astype(BF), wd_ref[fs, :])
    o_ref[...] = y_ref[...] + g2_ref[...] * acc


def _ff_tile(d_ff):
    for cand in (1408, 1024, 512, 256, 128):
        if d_ff % cand == 0:
            return cand
    return d_ff


def _ffn_dense(h2, y, gate2, wg, wu, wd, *, per_batch, tokens_per_batch):
    n, d = y.shape
    tm = 512 if per_batch else n
    full = lambda shp: pl.BlockSpec(shp, lambda i: (0,) * len(shp))
    row = lambda w: pl.BlockSpec((tm, w), lambda i: (i, 0))
    return pl.pallas_call(
        functools.partial(_ffn_kernel, f_tile=_ff_tile(wg.shape[1])),
        grid=(n // tm,),
        in_specs=[row(d), row(d), _mod_spec(per_batch, tm, d, tokens_per_batch), full(wg.shape), full(wu.shape),
                  full(wd.shape)],
        out_specs=row(d),
        out_shape=jax.ShapeDtypeStruct((n, d), F32),
        compiler_params=_cparams(("arbitrary",)),
        name="ffn_dense",
    )(h2, y, gate2, wg, wu, wd)


def _router_kernel(h_ref, rwh_ref, rwl_ref, rb_ref, o_ref, *, n_experts):
    h = _untile(h_ref, o_ref.shape[0])
    hi = h.astype(BF)
    lo = (h - hi.astype(F32)).astype(BF)
    logits = _dot(hi, rwh_ref[...]) + _dot(lo, rwh_ref[...]) + _dot(hi, rwl_ref[...]) + rb_ref[...]
    lane = lax.broadcasted_iota(jnp.int32, logits.shape, 1)
    lane_f = lane.astype(F32)
    l1 = jnp.where(lane < n_experts, logits, -jnp.inf)
    m1 = jnp.max(l1, axis=-1, keepdims=True)
    i1 = jnp.min(jnp.where(l1 == m1, lane_f, float(LANES)), axis=-1, keepdims=True)
    l2 = jnp.where(lane_f == i1, -jnp.inf, l1)
    m2 = jnp.max(l2, axis=-1, keepdims=True)
    i2 = jnp.min(jnp.where(l2 == m2, lane_f, float(LANES)), axis=-1, keepdims=True)
    e2 = jnp.exp(m2 - m1)
    den = 1.0 + e2
    o_ref[...] = jnp.where(lane == 0, i1, jnp.where(lane == 1, i2, jnp.where(lane == 2, 1.0 / den,
                           jnp.where(lane == 3, e2 / den, 0.0))))


def _router(h2t, rw_hi, rw_lo, rb, n_experts):
    n = h2t.shape[0] // TOKEN_TILE_ROWS
    tm = 512 if n % 512 == 0 else n
    full = lambda shp: pl.BlockSpec(shp, lambda i: (0,) * len(shp))
    return pl.pallas_call(
        functools.partial(_router_kernel, n_experts=n_experts),
        grid=(n // tm,),
        in_specs=[pl.BlockSpec((tm * TOKEN_TILE_ROWS, LANES), lambda i: (i, 0)), full(rw_hi.shape),
                  full(rw_lo.shape), full(rb.shape)],
        out_specs=pl.BlockSpec((tm, LANES), lambda i: (i, 0)),
        out_shape=jax.ShapeDtypeStruct((n, LANES), F32),
        compiler_params=_cparams(("arbitrary",)),
        name="moe_router",
    )(h2t, rw_hi, rw_lo, rb)


def _gather_kernel(idx_ref, src_ref, out_ref, sem, *, rows_per_step):
    base = pl.program_id(0) * rows_per_step

    def row_copy(j):
        return pltpu.make_async_copy(src_ref.at[idx_ref[base + j]], out_ref.at[j], sem)

    def start(g, c):
        for k in range(GATHER_GROUP):
            row_copy(g * GATHER_GROUP + k).start(priority=k % 2)
        return c

    def wait(j, c):
        row_copy(j).wait()
        return c

    lax.fori_loop(0, rows_per_step // GATHER_GROUP, start, 0)
    lax.fori_loop(0, rows_per_step, wait, 0, unroll=GATHER_GROUP)


def _gather_rows(src_t, idx, rows_per_step=512):
    m = idx.shape[0]
    while m % rows_per_step:
        rows_per_step //= 2
    src3 = src_t.reshape(-1, TOKEN_TILE_ROWS, LANES)
    gs = pltpu.PrefetchScalarGridSpec(
        num_scalar_prefetch=1,
        grid=(m // rows_per_step,),
        in_specs=[pl.BlockSpec(memory_space=pl.ANY)],
        out_specs=pl.BlockSpec((rows_per_step, TOKEN_TILE_ROWS, LANES), lambda i, idx_ref: (i, 0, 0)),
        scratch_shapes=[pltpu.SemaphoreType.DMA(())],
    )
    out = pl.pallas_call(
        functools.partial(_gather_kernel, rows_per_step=rows_per_step),
        grid_spec=gs,
        out_shape=jax.ShapeDtypeStruct((m, TOKEN_TILE_ROWS, LANES), src_t.dtype),
        compiler_params=pltpu.CompilerParams(dimension_semantics=("arbitrary",), vmem_limit_bytes=VMEM_LIMIT),
        name="gather_rows",
    )(idx, src3)
    return out.reshape(m * TOKEN_TILE_ROWS, LANES)


def _moe_block_kernel(be_ref, nu_ref, x_ref, wg_ref, wu_ref, wd_ref, o_ref, *, f_tile):
    i = pl.program_id(0)

    @pl.when(i < nu_ref[0])
    def _():
        hb = _untile(x_ref, MOE_BLOCK).astype(BF)
        d_ff = wg_ref.shape[1]
        acc = jnp.zeros((MOE_BLOCK, wd_ref.shape[1]), F32)
        for f in range(d_ff // f_tile):
            fs = slice(f * f_tile, (f + 1) * f_tile)
            gp = _dot(hb, wg_ref[:, fs])
            up = _dot(hb, wu_ref[:, fs])
            act = gp * jax.nn.sigmoid(gp) * up
            acc = acc + _dot(act.astype(BF), wd_ref[fs, :])
        _store_tiled(o_ref, acc)

    @pl.when(i >= nu_ref[0])
    def _():
        o_ref[...] = jnp.zeros(o_ref.shape, F32)


def _moe_blocks(blk_e, n_used, xb_t, wg, wu, wd):
    nb = xb_t.shape[0] // (MOE_BLOCK * TOKEN_TILE_ROWS)
    d, d_ff = wg.shape[1], wg.shape[2]
    blk = pl.BlockSpec((MOE_BLOCK * TOKEN_TILE_ROWS, LANES), lambda i, be, nu: (i, 0))
    gs = pltpu.PrefetchScalarGridSpec(
        num_scalar_prefetch=2,
        grid=(nb,),
        in_specs=[blk,
                  pl.BlockSpec((None, d, d_ff), lambda i, be, nu: (be[i], 0, 0)),
                  pl.BlockSpec((None, d, d_ff), lambda i, be, nu: (be[i], 0, 0)),
                  pl.BlockSpec((None, d_ff, d), lambda i, be, nu: (be[i], 0, 0))],
        out_specs=blk,
    )
    return pl.pallas_call(
        functools.partial(_moe_block_kernel, f_tile=_ff_tile(d_ff)),
        grid_spec=gs,
        out_shape=jax.ShapeDtypeStruct(xb_t.shape, F32),
        compiler_params=_cparams(("arbitrary",)),
        name="moe_blocks",
    )(blk_e, n_used, xb_t, wg, wu, wd)


def _combine_kernel(ya0_ref, ya1_ref, r_ref, y_ref, g2_ref, o_ref):
    tm = o_ref.shape[0]
    w1 = r_ref[:, 2:3]
    w2 = r_ref[:, 3:4]
    o_ref[...] = y_ref[...] + g2_ref[...] * (w1 * _untile(ya0_ref, tm) + w2 * _untile(ya1_ref, tm))


def _moe_combine(ya_t, route, y, gate2, *, per_batch, tokens_per_batch):
    n, d = y.shape
    tm = 512 if per_batch else n
    nt = n // tm
    row = lambda w: pl.BlockSpec((tm, w), lambda i: (i, 0))
    return pl.pallas_call(
        _combine_kernel,
        grid=(nt,),
        in_specs=[pl.BlockSpec((tm * TOKEN_TILE_ROWS, LANES), lambda i: (i, 0)),
                  pl.BlockSpec((tm * TOKEN_TILE_ROWS, LANES), lambda i: (i + nt, 0)), row(LANES), row(d),
                  _mod_spec(per_batch, tm, d, tokens_per_batch)],
        out_specs=row(d),
        out_shape=jax.ShapeDtypeStruct((n, d), F32),
        compiler_params=_cparams(("arbitrary",)),
        name="moe_combine",
    )(ya_t, ya_t, route, y, gate2)


def _moe_ffn(h2t, y, gate2, rw_hi, rw_lo, rb, wg, wu, wd, *, per_batch, tokens_per_batch):
    n, d = y.shape
    n_experts = wg.shape[0]
    route = _router(h2t, rw_hi, rw_lo, rb, n_experts)
    flat_e = route[:, 0:TOP_K].astype(jnp.int32).reshape(-1)
    onehot = (flat_e[:, None] == jnp.arange(n_experts, dtype=jnp.int32)[None, :]).astype(jnp.int32)
    csum = jnp.cumsum(onehot, axis=0)
    rank = jnp.sum(onehot * csum, axis=1) - 1
    counts = csum[-1]
    padded = (counts + MOE_BLOCK - 1) // MOE_BLOCK * MOE_BLOCK
    pad_end = jnp.cumsum(padded)
    pad_start = pad_end - padded
    slot = (pad_start[flat_e] + rank).astype(jnp.int32)
    nb = -(-(n * TOP_K) // MOE_BLOCK) + n_experts
    src = jnp.zeros((nb * MOE_BLOCK,), jnp.int32).at[slot].set(jnp.arange(n * TOP_K, dtype=jnp.int32) // TOP_K)
    blk_e = jnp.minimum(jnp.sum(pad_end[None, :] <= (jnp.arange(nb) * MOE_BLOCK)[:, None], axis=-1),
                        n_experts - 1).astype(jnp.int32)
    n_used = (pad_end[-1:] // MOE_BLOCK).astype(jnp.int32)
    xb = _gather_rows(h2t, src)
    yb = _moe_blocks(blk_e, n_used, xb, wg, wu, wd)
    ya = _gather_rows(yb, jnp.concatenate([slot[0::TOP_K], slot[1::TOP_K]]))
    return _moe_combine(ya, route, y, gate2, per_batch=per_batch, tokens_per_batch=tokens_per_batch)


def _rope_tables(pos):
    half = HEAD_DIM // 2
    inv = ROPE_THETA ** (-jnp.arange(half, dtype=F32) / half)
    ang = pos.astype(F32)[:, None] * inv[None, :]
    cos, sin = jnp.cos(ang), jnp.sin(ang)
    cos_l = jnp.concatenate([cos, cos] * (LANES // HEAD_DIM), axis=1)
    sin_l = jnp.concatenate([-sin, sin] * (LANES // HEAD_DIM), axis=1)
    return cos_l, sin_l


def _cover_matrix(n_rows, n_cmp, n_slc):
    cs = np.arange(n_cmp) * CMP_STRIDE
    ss = np.arange(n_slc) * SLC_BLOCK
    cov = np.clip(np.minimum(cs[:, None] + CMP_BLOCK, ss[None, :] + SLC_BLOCK)
                  - np.maximum(cs[:, None], ss[None, :]), 0, None).astype(np.float32) / CMP_BLOCK
    out = np.zeros((n_rows, LANES), np.float32)
    out[:n_cmp, :n_slc] = cov
    return jnp.asarray(out, BF)


def _expand_matrix(n_keys):
    blk = np.arange(LANES)[:, None]
    key = np.arange(n_keys)[None, :] // SLC_BLOCK
    return jnp.asarray((blk == key).astype(np.float32), BF)


def _piece_row_permutation(page):
    per_page = page // CMP_STRIDE
    r = np.arange(page)
    src = (r % per_page) * CMP_STRIDE + r // per_page
    return jnp.asarray((src[:, None] == np.arange(page)[None, :]).astype(np.float32), BF)


def _block_diag_mean(width):
    idx = np.arange(width) // HEAD_DIM
    return jnp.asarray((idx[:, None] == idx[None, :]).astype(np.float32) / HEAD_DIM, BF)


def _compress_weights(cmp_w_l, cmp_pe_l):
    z = jnp.zeros_like(cmp_w_l)
    wbig = jnp.concatenate([jnp.concatenate([cmp_w_l, z], axis=3), jnp.concatenate([z, cmp_w_l], axis=3)],
                           axis=2).astype(BF)
    pebig = jnp.concatenate([cmp_pe_l, cmp_pe_l], axis=2)
    return wbig, pebig


def kernel(x_prompt, x_sample, cache_cmp, cache_slc, state_win, page_table, c_prompt, c_sample, norm_mix_g, norm_ffn_g, w_ada, b_ada, w_in, w_spatial, b_spatial, q_norm_g, k_norm_g, cmp_pe, cmp_w, out_norm_g, w_out, ffn_w_gate, ffn_w_up, ffn_w_down, router_w, router_b, moe_w_gate, moe_w_up, moe_w_down):
    batch, t, d = x_prompt.shape
    db, tn, _ = x_sample.shape
    depth = w_ada.shape[0]
    n_pool, page = cache_cmp.shape[1], cache_cmp.shape[2]
    n_pages = page_table.shape[1]
    past = n_pages * page
    wb = state_win.shape[2]
    d_a = d // 2
    a_groups = d_a // HEAD_DIM
    assert tn == 1 and past % CHUNK == 0 and t % KEY_TILE == 0 and d_a % LANES == 0
    assert d - d_a == B_KV_HEADS * GQA_R * HEAD_DIM and B_KV_HEADS * HEAD_DIM == LANES
    n_p = batch * t
    kvw = 2 * LANES
    p_main = 2 * d_a + (d - d_a) + N_BRANCH * kvw
    n_gate = w_in.shape[2] - p_main
    qpos_s = past

    m_rows = batch + db
    m_pad = -(-m_rows // 8) * 8
    c_all = jnp.concatenate([c_prompt, c_sample, jnp.zeros((m_pad - m_rows, d), F32)], axis=0)
    mods = _adaln(c_all, w_ada, b_ada)

    bd = _block_diag_mean(d_a)
    cos_p, sin_p = _rope_tables(jnp.arange(t, dtype=jnp.int32))
    cos_s, sin_s = _rope_tables(jnp.full((1,), qpos_s, jnp.int32))
    np_p = t // CMP_STRIDE
    cover_p = _cover_matrix(np_p, np_p - 1, t // SLC_BLOCK)
    expand_p = _expand_matrix(t)
    np_s = past // CMP_STRIDE
    ns_s = -(-(past + tn) // SLC_BLOCK)
    cover_s = _cover_matrix(np_s, np_s - 1, ns_s)
    expand_s = _expand_matrix(past)
    tril = jnp.tril(jnp.ones((CHUNK, CHUNK), dtype=bool))

    cache_cmp4 = jnp.transpose(cache_cmp, (0, 1, 3, 4, 5, 2)).reshape(depth, n_pool, kvw, page)
    perm_s = _piece_row_permutation(page)
    cache_slc4 = jnp.transpose(cache_slc, (0, 1, 3, 4, 5, 2)).reshape(depth, n_pool, kvw, page)
    state_win4 = jnp.transpose(state_win, (0, 1, 3, 4, 5, 2)).reshape(depth, db, kvw, wb)

    yp = x_prompt.reshape(n_p, d)
    ys = x_sample.reshape(db, d)
    outs = {k: [] for k in ("cmp_s", "slc_s", "chv_s")}
    win_buf = jnp.zeros((depth, db, kvw, wb), F32)
    kv_bufs = [jnp.zeros((depth, batch, kvw, t), F32) for _ in range(N_BRANCH)]
    for l in range(depth):
        mod_p = [mods[l, :batch, k * d:(k + 1) * d].reshape(batch, 1, d) for k in range(6)]
        mod_s = [mods[l, batch:batch + db, k * d:(k + 1) * d] for k in range(6)]
        w_in_bf = jnp.concatenate([w_in[l], jnp.zeros((d, LANES - n_gate), F32)], axis=1).astype(BF)
        qg = jnp.tile(q_norm_g[l], (d - d_a) // HEAD_DIM)[None, :]
        kg = jnp.tile(k_norm_g[l], (1, LANES // HEAD_DIM))
        ws_p = jnp.where(tril, w_spatial[l], 0.0).astype(BF)
        bs_p = jnp.repeat(b_spatial[l].T, HEAD_DIM, axis=1)
        ws_s = jnp.repeat(w_spatial[l][:, 0, 0], HEAD_DIM)[None, :]
        bs_s = jnp.repeat(b_spatial[l][:, 0], HEAD_DIM)[None, :]
        wbig, pebig = _compress_weights(cmp_w[l], cmp_pe[l])
        out_g = out_norm_g[l].reshape(1, d)
        wout_bf = w_out[l].astype(BF)
        ng1 = norm_mix_g[l][None, :]
        ng2 = norm_ffn_g[l][None, :]

        a, q, cmp_r, slc_bf, win_bf, gates, *kv_bufs = _pre_mix(
            yp, ng1, mod_p[0], mod_p[1], w_in_bf, cos_p, sin_p, bd, qg, kg, ws_p, bs_p,
            chunked=True, tokens_per_batch=t, kv_bufs=kv_bufs, layer=l)
        kc, vc = _compress_prompt(cmp_r, wbig, pebig, batch)
        o = _nsa_prompt(q, gates, kc, vc, slc_bf, win_bf, cover_p, expand_p, batch)
        yp, h2p = _post_mix(a, o, yp, mod_p[2], out_g, wout_bf, bd, ng2, mod_p[3], mod_p[4],
                            per_batch=True, tokens_per_batch=t, tiled_h2=(l % 2 == 1))

        a_s, q_s, cmp_s, slc_s, win_s, gates_s, v_s = _pre_mix(
            ys, ng1, mod_s[0], mod_s[1], w_in_bf, cos_s, sin_s, bd, qg, kg, ws_s, bs_s,
            chunked=False, tokens_per_batch=1)
        q3 = q_s.reshape(db, 1, d - d_a)
        ocmp, imp = _s_cmp(page_table, q3, cache_cmp4, l, perm_s, wbig, pebig, cover_s, qpos_s)
        mask = _s_select(imp[:, 0:B_KV_HEADS, :].reshape(db * B_KV_HEADS, LANES), expand_s, qpos_s)
        o_s, win_buf = _s_att(page_table, q3, gates_s.reshape(db, 1, LANES), mask.reshape(db, B_KV_HEADS, past),
                              ocmp, slc_s.reshape(db, 1, kvw), win_s.reshape(db, 1, kvw), state_win4, win_buf,
                              cache_slc4, l, qpos_s, past)
        ys, h2s = _post_mix(a_s, o_s.reshape(db, d - d_a), ys, mod_s[2], out_g, wout_bf, bd, ng2, mod_s[3],
                            mod_s[4], per_batch=False, tokens_per_batch=1, tiled_h2=(l % 2 == 1))
        outs["cmp_s"].append(cmp_s.reshape(db, tn, 2, B_KV_HEADS, HEAD_DIM))
        outs["slc_s"].append(slc_s.reshape(db, tn, 2, B_KV_HEADS, HEAD_DIM))
        outs["chv_s"].append(v_s.reshape(db, tn, d_a))

        i = l // 2
        if l % 2 == 0:
            wg, wu, wd = ffn_w_gate[i].astype(BF), ffn_w_up[i].astype(BF), ffn_w_down[i].astype(BF)
            yp = _ffn_dense(h2p, yp, mod_p[5], wg, wu, wd, per_batch=True, tokens_per_batch=t)
            ys = _ffn_dense(h2s, ys, mod_s[5], wg, wu, wd, per_batch=False, tokens_per_batch=1)
        else:
            wg, wu, wd = moe_w_gate[i].astype(BF), moe_w_up[i].astype(BF), moe_w_down[i].astype(BF)
            n_exp = router_w.shape[2]
            rw = jnp.concatenate([router_w[i], jnp.zeros((d, LANES - n_exp), F32)], axis=1)
            rw_hi = rw.astype(BF)
            rw_lo = (rw - rw_hi.astype(F32)).astype(BF)
            rb = jnp.concatenate([router_b[i], jnp.zeros((LANES - n_exp,), F32)])[None, :]
            yp = _moe_ffn(h2p, yp, mod_p[5], rw_hi, rw_lo, rb, wg, wu, wd, per_batch=True, tokens_per_batch=t)
            ys = _moe_ffn(h2s, ys, mod_s[5], rw_hi, rw_lo, rb, wg, wu, wd, per_batch=False, tokens_per_batch=1)

    st = lambda k: jnp.stack(outs[k])

    def rows_view(buf):
        return jnp.transpose(buf.reshape(buf.shape[0], buf.shape[1], 2, B_KV_HEADS, HEAD_DIM, buf.shape[3]),
                             (0, 1, 5, 2, 3, 4))

    wbp = min(WINDOW, t)
    return (yp.reshape(batch, t, d), ys.reshape(db, tn, d), rows_view(kv_bufs[0]), st("cmp_s"),
            rows_view(kv_bufs[1]), st("slc_s"), rows_view(kv_bufs[2][:, :, :, t - wbp:]), rows_view(win_buf),
            st("chv_s"))
```
